```python
import math
import jax, jax.numpy as jnp
from jax import lax
import numpy as np

D_MODEL = 2048
BATCH = 1
SEQ = 8192
DEPTH = 4
DEC_BATCH = 32
DEC_SEQ = 16
PAST_LEN = 1024

CHUNK = 64
N_A_LAYERS = DEPTH // 2
N_B_LAYERS = DEPTH - N_A_LAYERS
POOL_WINDOWS = (2, 4, 8, 16)
N_POOL_GROUPS = len(POOL_WINDOWS)
POOL_GROUP = D_MODEL // N_POOL_GROUPS
POOL_HIST = max(POOL_WINDOWS) - 1
N_HEADS = 16
QK_DIM = 64
V_DIM = 2 * QK_DIM
Q_WIDTH = N_HEADS * 2 * QK_DIM
K_WIDTH = N_HEADS * 2 * QK_DIM
V_WIDTH = N_HEADS * V_DIM
D_FF = 5632
CONV_W = 3
Q_BLOCK = 128
LN_EPS = 1e-5
ALPHA = (2 * DEPTH) ** 0.25
BETA = (8 * DEPTH) ** -0.25
ATTN_SCALE = QK_DIM ** -0.5
NEG_INF = -1e30

kernel_name = 'yoco_pool_diffattn_streaming_step'


def layer_norm(x, g, b):
    xf = x.astype(jnp.float32)
    mu = jnp.mean(xf, axis=-1, keepdims=True)
    xc = xf - mu
    var = jnp.mean(xc * xc, axis=-1, keepdims=True)
    return (xc * lax.rsqrt(var + LN_EPS) * g + b).astype(x.dtype)


def pool_mix(u_ext, pos0, w_pool, scale):
    B, Lx, D = u_ext.shape
    L = Lx - POOL_HIST
    uf = u_ext.astype(jnp.float32)
    cs = jnp.concatenate([jnp.zeros((B, 1, D), jnp.float32), jnp.cumsum(uf, axis=1)], axis=1)
    end = cs[:, POOL_HIST + 1:]
    cur = uf[:, POOL_HIST:]
    pos = pos0 + jnp.arange(L)
    groups = []
    for g, w in enumerate(POOL_WINDOWS):
        sl = slice(g * POOL_GROUP, (g + 1) * POOL_GROUP)
        start = cs[:, POOL_HIST + 1 - w:POOL_HIST + 1 - w + L, sl]
        cnt = jnp.minimum(w, pos + 1).astype(jnp.float32)[None, :, None]
        groups.append((end[..., sl] - start) / cnt - cur[..., sl])
    d = jnp.stack(groups, axis=2)
    out = jnp.einsum('blgc,gcd->blgd', d.astype(w_pool.dtype), w_pool).reshape(B, L, D)
    return (out * scale).astype(u_ext.dtype)


def conv_ffn(x, hist, w_up, conv_w, conv_b, w_down):
    L = x.shape[1]
    h = jnp.einsum('bld,df->blf', x, w_up)
    h_ext = jnp.concatenate([hist.astype(h.dtype), h], axis=1)
    c = conv_b
    for j in range(CONV_W):
        c = c + conv_w[j] * h_ext[:, j:j + L]
    a, v = jnp.split(c, 2, axis=-1)
    y = jnp.einsum('blf,fd->bld', jax.nn.gelu(a) * v, w_down)
    return y.astype(x.dtype), h_ext[:, -(CONV_W - 1):]


def diff_lambda(lam, lam_init):
    lf = lam.astype(jnp.float32)
    return jnp.exp(jnp.sum(lf[0] * lf[1])) - jnp.exp(jnp.sum(lf[2] * lf[3])) + lam_init


def diff_combine(s, v, lam, mask):
    if mask is not None:
        s = jnp.where(mask, s, NEG_INF)
    p = jax.nn.softmax(s, axis=-1)
    w = p[:, :, 0] - lam * p[:, :, 1]
    return jnp.einsum('bhqk,bkhe->bqhe', w.astype(v.dtype), v)


def attn_prompt(q, k, v, lam):
    B, T = q.shape[:2]
    nb = T // Q_BLOCK
    qb = q.reshape(B, nb, Q_BLOCK, N_HEADS, 2, QK_DIM).transpose(1, 0, 2, 3, 4, 5)
    key_chunk = jnp.arange(T) // CHUNK

    def one(args):
        q_blk, i = args
        s = jnp.einsum('bqhid,bkhid->bhiqk', q_blk, k, preferred_element_type=jnp.float32) * ATTN_SCALE
        q_chunk = (i * Q_BLOCK + jnp.arange(Q_BLOCK)) // CHUNK
        mask = key_chunk[None, :] <= q_chunk[:, None]
        return diff_combine(s, v, lam, mask)

    o = lax.map(one, (qb, jnp.arange(nb)))
    return o.transpose(1, 0, 2, 3, 4).reshape(B, T, N_HEADS, V_DIM)


def attn_sample(q, k, v, lam):
    s = jnp.einsum('bqhid,bkhid->bhiqk', q, k, preferred_element_type=jnp.float32) * ATTN_SCALE
    return diff_combine(s, v, lam, None)


def diff_out(o, subln_g, lam_init, w_o):
    of = o.astype(jnp.float32)
    of = of * lax.rsqrt(jnp.mean(of * of, axis=-1, keepdims=True) + LN_EPS) * subln_g * (1.0 - lam_init)
    return jnp.einsum('blhe,hed->bld', of.astype(o.dtype), w_o.reshape(N_HEADS, V_DIM, D_MODEL))


def run_group(x, pool_hist, conv_hist, past_k, past_v, pos0, p):
    B, L, _ = x.shape
    new_pool, new_conv = [], []
    k_new = v_new = k_att = v_att = None
    for l in range(DEPTH):
        if l < N_A_LAYERS:
            u_ext = jnp.concatenate([pool_hist[l].astype(x.dtype), x], axis=1)
            mix = pool_mix(u_ext, pos0, p['w_pool'][l], p['pool_scale'][l])
            new_pool.append(u_ext[:, -POOL_HIST:])
        else:
            j = l - N_A_LAYERS
            lam_init = 0.8 - 0.6 * math.exp(-0.3 * l)
            lam = diff_lambda(p['lam'][j], lam_init)
            q = jnp.einsum('bld,dq->blq', x, p['w_q'][j]).reshape(B, L, N_HEADS, 2, QK_DIM)
            if past_k is None:
                o = attn_prompt(q, k_att, v_att, lam)
            else:
                o = attn_sample(q, k_att, v_att, lam)
            mix = diff_out(o, p['subln_g'][j], lam_init, p['w_o'][j]).astype(x.dtype)
        x = layer_norm(ALPHA * x + mix, p['ln1_g'][l], p['ln1_b'][l])
        f, hc = conv_ffn(x, conv_hist[l], p['w_up'][l], p['conv_w'][l], p['conv_b'][l], p['w_down'][l])
        new_conv.append(hc)
        x = layer_norm(ALPHA * x + f, p['ln2_g'][l], p['ln2_b'][l])
        if l == N_A_LAYERS - 1:
            kv = jnp.einsum('bld,dk->blk', x, p['w_kv'])
            k_new = kv[..., :K_WIDTH].reshape(B, L, N_HEADS, 2 * QK_DIM)
            v_new = kv[..., K_WIDTH:].reshape(B, L, N_HEADS, V_DIM)
            if past_k is None:
                k_all, v_all = k_new, v_new
            else:
                k_all = jnp.concatenate([past_k.astype(k_new.dtype), k_new], axis=1)
                v_all = jnp.concatenate([past_v.astype(v_new.dtype), v_new], axis=1)
            k_att = k_all.reshape(B, k_all.shape[1], N_HEADS, 2, QK_DIM)
            v_att = v_all
    return x, jnp.stack(new_pool), jnp.stack(new_conv), k_new, v_new


def setup_inputs(seed: int = 0) -> dict:
    key = jax.random.key(seed)
    ks = jax.random.split(key, 24)
    f32 = jnp.float32

    def nrm(k, shape, s):
        return jax.random.normal(k, shape, f32) * s

    w_k = nrm(ks[16], (D_MODEL, K_WIDTH), D_MODEL ** -0.5)
    w_v = nrm(ks[17], (D_MODEL, V_WIDTH), D_MODEL ** -0.5 * BETA)
    return {
        'x_prompt': nrm(ks[0], (BATCH, SEQ, D_MODEL), 1.0),
        'x_sample': nrm(ks[1], (DEC_BATCH, DEC_SEQ, D_MODEL), 1.0),
        'state_pool': nrm(ks[2], (N_A_LAYERS, DEC_BATCH, POOL_HIST, D_MODEL), 1.0),
        'state_ffn_conv': nrm(ks[3], (DEPTH, DEC_BATCH, CONV_W - 1, 2 * D_FF), BETA),
        'cache_k': nrm(ks[4], (DEC_BATCH, PAST_LEN, N_HEADS, 2 * QK_DIM), 1.0),
        'cache_v': nrm(ks[5], (DEC_BATCH, PAST_LEN, N_HEADS, V_DIM), BETA),
        'ln1_g': 1.0 + nrm(ks[6], (DEPTH, D_MODEL), 0.02),
        'ln1_b': nrm(ks[7], (DEPTH, D_MODEL), 0.02),
        'ln2_g': 1.0 + nrm(ks[8], (DEPTH, D_MODEL), 0.02),
        'ln2_b': nrm(ks[9], (DEPTH, D_MODEL), 0.02),
        'w_pool': nrm(ks[10], (N_A_LAYERS, N_POOL_GROUPS, POOL_GROUP, POOL_GROUP), POOL_GROUP ** -0.5 * BETA),
        'pool_scale': 1.0 + nrm(ks[11], (N_A_LAYERS, D_MODEL), 0.02),
        'w_up': nrm(ks[12], (DEPTH, D_MODEL, 2 * D_FF), D_MODEL ** -0.5 * BETA),
        'conv_w': nrm(ks[13], (DEPTH, CONV_W, 2 * D_FF), CONV_W ** -0.5),
        'conv_b': nrm(ks[14], (DEPTH, 2 * D_FF), 0.02),
        'w_down': nrm(ks[15], (DEPTH, D_FF, D_MODEL), D_FF ** -0.5 * BETA),
        'w_kv': jnp.concatenate([w_k, w_v], axis=1),
        'w_q': nrm(ks[18], (N_B_LAYERS, D_MODEL, Q_WIDTH), D_MODEL ** -0.5),
        'lam': nrm(ks[19], (N_B_LAYERS, 4, QK_DIM), 0.1),
        'subln_g': 1.0 + nrm(ks[20], (N_B_LAYERS, V_DIM), 0.02),
        'w_o': nrm(ks[21], (N_B_LAYERS, V_WIDTH, D_MODEL), V_WIDTH ** -0.5 * BETA),
    }


def reference(x_prompt, x_sample, state_pool, state_ffn_conv, cache_k, cache_v,
              ln1_g, ln1_b, ln2_g, ln2_b, w_pool, pool_scale, w_up, conv_w, conv_b, w_down,
              w_kv, w_q, lam, subln_g, w_o):
    p = dict(ln1_g=ln1_g, ln1_b=ln1_b, ln2_g=ln2_g, ln2_b=ln2_b, w_pool=w_pool,
             pool_scale=pool_scale, w_up=w_up, conv_w=conv_w, conv_b=conv_b, w_down=w_down,
             w_kv=w_kv, w_q=w_q, lam=lam, subln_g=subln_g, w_o=w_o)
    B = x_prompt.shape[0]
    pool0 = jnp.zeros((N_A_LAYERS, B, POOL_HIST, D_MODEL), x_prompt.dtype)
    conv0 = jnp.zeros((DEPTH, B, CONV_W - 1, 2 * D_FF), x_prompt.dtype)
    y_prompt, pool_prompt, conv_prompt, k_prompt, v_prompt = run_group(
        x_prompt, pool0, conv0, None, None, 0, p)
    y_sample, pool_sample, conv_sample, k_sample, v_sample = run_group(
        x_sample, state_pool, state_ffn_conv, cache_k, cache_v, PAST_LEN, p)
    return (y_prompt, y_sample, pool_prompt, pool_sample, conv_prompt, conv_sample,
            k_prompt, v_prompt, k_sample, v_sample)
```

```python
import functools
import math

import jax
import jax.numpy as jnp
from jax import lax
from jax.experimental import pallas as pl
from jax.experimental.pallas import tpu as pltpu

D_MODEL = 2048
DEPTH = 4
N_A_LAYERS = DEPTH // 2
CHUNK = 64
POOL_WINDOWS = (2, 4, 8, 16)
POOL_GROUP = D_MODEL // len(POOL_WINDOWS)
POOL_HIST = max(POOL_WINDOWS) - 1
N_HEADS = 16
QK_DIM = 64
V_DIM = 2 * QK_DIM
HEAD_W = 2 * QK_DIM
D_FF = 5632
CONV_W = 3
LN_EPS = 1e-5
ALPHA = (2 * DEPTH) ** 0.25
ATTN_SCALE = QK_DIM ** -0.5
NEG_INF = -1e30

V7X_VMEM_BYTES = 64 * 1024 * 1024
VMEM_LIMIT_BYTES = V7X_VMEM_BYTES - 8 * 1024 * 1024
SUBLANES = 8

ROW_TILE = 512
FF_TILE = 512
N_FF_TILES = D_FF // FF_TILE
MM_COL_TILE = 1024
Q_TILE = 256
K_TILE = 512
SAMPLE_HEADS_PER_STEP = 8

_F32 = jnp.float32
_BF16 = jnp.bfloat16


def _params(n_axes):
    return pltpu.CompilerParams(dimension_semantics=("arbitrary",) * n_axes,
                                vmem_limit_bytes=VMEM_LIMIT_BYTES)


def _layer_norm(z, g, b):
    mu = jnp.mean(z, axis=-1, keepdims=True)
    zc = z - mu
    var = jnp.mean(zc * zc, axis=-1, keepdims=True)
    return zc * lax.rsqrt(var + LN_EPS) * g + b


def _gelu_tanh(x):
    cdf = 0.5 * (1.0 + jnp.tanh(math.sqrt(2.0 / math.pi) * (x + 0.044715 * (x * x * x))))
    return x * cdf


def _pool_ln_kernel(x_ref, halo_ref, hist_ref, w_ref, scale_ref, g_ref, b_ref, o_ref,
                    ext_ref, z_ref, *, seqs, rows, pos0):
    i = pl.program_id(0)
    halo_rows = (POOL_HIST + 1) * seqs

    @pl.when(i == 0)
    def _():
        ext_ref[0:halo_rows, :] = hist_ref[...]

    @pl.when(i > 0)
    def _():
        ext_ref[0:halo_rows, :] = halo_ref[...]

    ext_ref[halo_rows:halo_rows + rows, :] = x_ref[...]

    row = lax.broadcasted_iota(jnp.int32, (rows, 1), 0) + i * rows
    step = lax.shift_right_logical(row, int(math.log2(seqs)))
    pos_plus_1 = (step + (pos0 + 1)).astype(_F32)

    for g, w in enumerate(POOL_WINDOWS):
        cols = slice(g * POOL_GROUP, (g + 1) * POOL_GROUP)
        cur = x_ref[:, cols]
        win = cur
        for k in range(1, w):
            start = halo_rows - k * seqs
            win = win + ext_ref[start:start + rows, cols]
        cnt = jnp.minimum(float(w), pos_plus_1)
        d = win / cnt - cur
        mix = jnp.dot(d.astype(_BF16), w_ref[g], preferred_element_type=_F32)
        z_ref[:, cols] = ALPHA * cur + mix * scale_ref[:, cols]

    o_ref[...] = _layer_norm(z_ref[...], g_ref[...], b_ref[...])


def _pool_ln(x, hist, w_pool, scale, g, b, *, seqs, pos0):
    m = x.shape[0]
    rows = min(ROW_TILE, m)
    halo_rows = (POOL_HIST + 1) * seqs
    assert m % rows == 0 and rows % halo_rows == 0 and seqs & (seqs - 1) == 0
    halo_blocks_per_tile = rows // halo_rows
    kern = functools.partial(_pool_ln_kernel, seqs=seqs, rows=rows, pos0=pos0)
    return pl.pallas_call(
        kern,
        grid=(m // rows,),
        in_specs=[
            pl.BlockSpec((rows, D_MODEL), lambda i: (i, 0)),
            pl.BlockSpec((halo_rows, D_MODEL),
                         lambda i: (jnp.maximum(i * halo_blocks_per_tile - 1, 0), 0)),
            pl.BlockSpec((halo_rows, D_MODEL), lambda i: (0, 0)),
            pl.BlockSpec((len(POOL_WINDOWS), POOL_GROUP, POOL_GROUP), lambda i: (0, 0, 0)),
            pl.BlockSpec((1, D_MODEL), lambda i: (0, 0)),
            pl.BlockSpec((1, D_MODEL), lambda i: (0, 0)),
            pl.BlockSpec((1, D_MODEL), lambda i: (0, 0)),
        ],
        out_specs=pl.BlockSpec((rows, D_MODEL), lambda i: (i, 0)),
        out_shape=jax.ShapeDtypeStruct((m, D_MODEL), _F32),
        scratch_shapes=[pltpu.VMEM((halo_rows + rows, D_MODEL), _F32),
                        pltpu.VMEM((rows, D_MODEL), _F32)],
        compiler_params=_params(1),
        name="pool_ln",
    )(x, x, hist, w_pool, scale, g, b)


def _ffn_ln_kernel(x_ref, hist_a_ref, hist_v_ref, wa_ref, wv_ref, cwa_ref, cwv_ref, cba_ref, cbv_ref,
                   wd_ref, g_ref, b_ref, o_ref, new_a_ref, new_v_ref,
                   xb_ref, ext_a_ref, ext_v_ref, *, seqs, rows):
    i = pl.program_id(0)
    j = pl.program_id(1)
    halo = (CONV_W - 1) * seqs
    off = -(-halo // SUBLANES) * SUBLANES

    @pl.when(j == 0)
    def _():
        xb_ref[...] = x_ref[...].astype(_BF16)
        o_ref[...] = jnp.zeros_like(o_ref)

    xb = xb_ref[...]
    ha = jnp.dot(xb, wa_ref[...], preferred_element_type=_F32)
    hv = jnp.dot(xb, wv_ref[...], preferred_element_type=_F32)

    @pl.when(i == 0)
    def _():
        ext_a_ref[off - halo:off, :] = hist_a_ref[...]
        ext_v_ref[off - halo:off, :] = hist_v_ref[...]

    @pl.when(i > 0)
    def _():
        ext_a_ref[off - halo:off, :] = new_a_ref[j]
        ext_v_ref[off - halo:off, :] = new_v_ref[j]

    ext_a_ref[off:off + rows, :] = ha
    ext_v_ref[off:off + rows, :] = hv
    new_a_ref[j] = ha[rows - halo:, :]
    new_v_ref[j] = hv[rows - halo:, :]

    def conv(ext_ref, h, cw_ref, cb_ref):
        cw = cw_ref[...]
        c = cb_ref[...] + cw[0:1, :] * ext_ref[off - 2 * seqs:off - 2 * seqs + rows, :]
        c = c + cw[1:2, :] * ext_ref[off - seqs:off - seqs + rows, :]
        return c + cw[2:3, :] * h

    ca = conv(ext_a_ref, ha, cwa_ref, cba_ref)
    cv = conv(ext_v_ref, hv, cwv_ref, cbv_ref)
    act = (_gelu_tanh(ca) * cv).astype(_BF16)
    o_ref[...] += jnp.dot(act, wd_ref[...], preferred_element_type=_F32)

    @pl.when(j == N_FF_TILES - 1)
    def _():
        o_ref[...] = _layer_norm(ALPHA * x_ref[...] + o_ref[...], g_ref[...], b_ref[...])


def _ffn_ln(x, hist, w_up, conv_w, conv_b, w_down, g, b, *, seqs):
    m = x.shape[0]
    rows = min(ROW_TILE, m)
    halo = (CONV_W - 1) * seqs
    off = -(-halo // SUBLANES) * SUBLANES
    assert m % rows == 0 and rows >= halo
    kern = functools.partial(_ffn_ln_kernel, seqs=seqs, rows=rows)
    a_cols = lambda i, j: (0, j)
    v_cols = lambda i, j: (0, j + N_FF_TILES)
    const = lambda i, j: (0, 0)
    return pl.pallas_call(
        kern,
        grid=(m // rows, N_FF_TILES),
        in_specs=[
            pl.BlockSpec((rows, D_MODEL), lambda i, j: (i, 0)),
            pl.BlockSpec((halo, FF_TILE), a_cols),
            pl.BlockSpec((halo, FF_TILE), v_cols),
            pl.BlockSpec((D_MODEL, FF_TILE), a_cols),
            pl.BlockSpec((D_MODEL, FF_TILE), v_cols),
            pl.BlockSpec((CONV_W, FF_TILE), a_cols),
            pl.BlockSpec((CONV_W, FF_TILE), v_cols),
            pl.BlockSpec((1, FF_TILE), a_cols),
            pl.BlockSpec((1, FF_TILE), v_cols),
            pl.BlockSpec((FF_TILE, D_MODEL), lambda i, j: (j, 0)),
            pl.BlockSpec((1, D_MODEL), const),
            pl.BlockSpec((1, D_MODEL), const),
        ],
        out_specs=[
            pl.BlockSpec((rows, D_MODEL), lambda i, j: (i, 0)),
            pl.BlockSpec((N_FF_TILES, halo, FF_TILE), lambda i, j: (0, 0, 0)),
            pl.BlockSpec((N_FF_TILES, halo, FF_TILE), lambda i, j: (0, 0, 0)),
        ],
        out_shape=[
            jax.ShapeDtypeStruct((m, D_MODEL), _F32),
            jax.ShapeDtypeStruct((N_FF_TILES, halo, FF_TILE), _F32),
            jax.ShapeDtypeStruct((N_FF_TILES, halo, FF_TILE), _F32),
        ],
        scratch_shapes=[
            pltpu.VMEM((rows, D_MODEL), _BF16),
            pltpu.VMEM((off + rows, FF_TILE), _F32),
            pltpu.VMEM((off + rows, FF_TILE), _F32),
        ],
        compiler_params=_params(2),
        name="ffn_ln",
    )(x, hist, hist, w_up, w_up, conv_w, conv_w, conv_b, conv_b, w_down, g, b)


def _proj_kernel(x_ref, w_ref, *refs, scale, n_out):
    out_refs, xb_ref = refs[:n_out], refs[n_out]

    @pl.when(pl.program_id(1) == 0)
    def _():
        xb_ref[...] = x_ref[...].astype(_BF16)

    r = jnp.dot(xb_ref[...], w_ref[...], preferred_element_type=_F32)
    for o_ref in out_refs:
        o_ref[...] = (r if o_ref.dtype == _F32 else r * scale).astype(o_ref.dtype)


def _proj(x, w, out_dtypes, *, scale=1.0):
    m, k = x.shape
    n = w.shape[1]
    rows = min(2 * ROW_TILE, m)
    cols = min(MM_COL_TILE, n)
    assert m % rows == 0 and n % cols == 0
    kern = functools.partial(_proj_kernel, scale=scale, n_out=len(out_dtypes))
    return pl.pallas_call(
        kern,
        grid=(m // rows, n // cols),
        in_specs=[pl.BlockSpec((rows, k), lambda i, j: (i, 0)),
                  pl.BlockSpec((k, cols), lambda i, j: (0, j))],
        out_specs=[pl.BlockSpec((rows, cols), lambda i, j: (i, j)) for _ in out_dtypes],
        out_shape=[jax.ShapeDtypeStruct((m, n), dt) for dt in out_dtypes],
        scratch_shapes=[pltpu.VMEM((rows, k), _BF16)],
        compiler_params=_params(2),
        name="proj",
    )(x, w)


def _diff_lambda(lam_ref, lam_init):
    lam = lam_ref[...]
    e1 = jnp.exp(jnp.sum(lam[0:1, :] * lam[1:2, :], axis=-1, keepdims=True))
    e2 = jnp.exp(jnp.sum(lam[2:3, :] * lam[3:4, :], axis=-1, keepdims=True))
    return e1 - e2 + lam_init


def _split_parts(q):
    first = lax.broadcasted_iota(jnp.int32, (1, HEAD_W), 1) < QK_DIM
    zero = jnp.zeros_like(q)
    return jnp.concatenate([jnp.where(first, q, zero), jnp.where(first, zero, q)], axis=0)


def _sub_norm(o, sg_ref, lam_init):
    return o * lax.rsqrt(jnp.mean(o * o, axis=-1, keepdims=True) + LN_EPS) * sg_ref[...] * (1.0 - lam_init)


def _score(qz, kb):
    return lax.dot_general(qz, kb, (((1,), (1,)), ((), ())), preferred_element_type=_F32)


def _flash_kernel(q_ref, k_ref, v_ref, lam_ref, sg_ref, o_ref, qz_ref, m_ref, l_ref, acc_ref,
                  *, lam_init, seq_len):
    lam = _diff_lambda(lam_ref, lam_init)
    n_rows = 2 * Q_TILE

    def q_block(qi, carry):
        q0 = pl.multiple_of(qi * Q_TILE, Q_TILE)
        qz_ref[...] = _split_parts(q_ref[pl.ds(q0, Q_TILE), :])
        m_ref[...] = jnp.full(m_ref.shape, NEG_INF, _F32)
        l_ref[...] = jnp.zeros_like(l_ref)
        acc_ref[...] = jnp.zeros_like(acc_ref)

        def step(k0, masked):
            kb = k_ref[pl.ds(k0, K_TILE), :]
            vb = v_ref[pl.ds(k0, K_TILE), :]
            s = _score(qz_ref[...], kb)
            if masked:
                r = lax.broadcasted_iota(jnp.int32, (n_rows, 1), 0)
                r = jnp.where(r >= Q_TILE, r - Q_TILE, r) + q0
                c = lax.broadcasted_iota(jnp.int32, (1, K_TILE), 1) + k0
                shift = int(math.log2(CHUNK))
                visible = lax.shift_right_logical(c, shift) <= lax.shift_right_logical(r, shift)
                s = jnp.where(visible, s, NEG_INF)
            m_prev = m_ref[...]
            m_new = jnp.maximum(m_prev, jnp.max(s, axis=1, keepdims=True))
            a = jnp.exp(m_prev - m_new)
            p = jnp.exp(s - m_new)
            l_ref[...] = a * l_ref[...] + jnp.sum(p, axis=1, keepdims=True)
            acc_ref[...] = a * acc_ref[...] + jnp.dot(p.astype(_BF16), vb, preferred_element_type=_F32)
            m_ref[...] = m_new

        n_full = lax.div(qi, K_TILE // Q_TILE)

        def full_step(kj, c):
            step(pl.multiple_of(kj * K_TILE, K_TILE), False)
            return c

        lax.fori_loop(0, n_full, full_step, 0)
        step(pl.multiple_of(n_full * K_TILE, K_TILE), True)

        o2 = acc_ref[...] / l_ref[...]
        o = o2[0:Q_TILE, :] - lam * o2[Q_TILE:, :]
        o_ref[pl.ds(q0, Q_TILE), :] = _sub_norm(o, sg_ref, lam_init).astype(o_ref.dtype)
        return carry

    lax.fori_loop(0, seq_len // Q_TILE, q_block, 0)


def _flash_prompt(q, k, v, lam, sg, *, lam_init):
    t = q.shape[0]
    assert t % K_TILE == 0 and K_TILE % Q_TILE == 0 and Q_TILE % CHUNK == 0
    kern = functools.partial(_flash_kernel, lam_init=lam_init, seq_len=t)
    head = pl.BlockSpec((t, HEAD_W), lambda h: (0, h))
    return pl.pallas_call(
        kern,
        grid=(N_HEADS,),
        in_specs=[head, head, head,
                  pl.BlockSpec((4, QK_DIM), lambda h: (0, 0)),
                  pl.BlockSpec((1, V_DIM), lambda h: (0, 0))],
        out_specs=head,
        out_shape=jax.ShapeDtypeStruct((t, N_HEADS * V_DIM), _BF16),
        scratch_shapes=[pltpu.VMEM((2 * Q_TILE, HEAD_W), _BF16),
                        pltpu.VMEM((2 * Q_TILE, 1), _F32),
                        pltpu.VMEM((2 * Q_TILE, 1), _F32),
                        pltpu.VMEM((2 * Q_TILE, V_DIM), _F32)],
        compiler_params=_params(1),
        name="flash_prompt",
    )(q, k, v, lam, sg)


def _sample_attn_kernel(q_ref, kn_ref, vn_ref, kc_ref, vc_ref, lam_ref, sg_ref, o_ref, *, lam_init, steps):
    lam = _diff_lambda(lam_ref, lam_init)
    for hh in range(SAMPLE_HEADS_PER_STEP):
        cols = slice(hh * HEAD_W, (hh + 1) * HEAD_W)
        qz = _split_parts(q_ref[:, cols])
        kc = kc_ref[0, :, cols].astype(_BF16)
        vc = vc_ref[0, :, cols].astype(_BF16)
        sc = _score(qz, kc)
        sn = _score(qz, kn_ref[:, cols])
        m = jnp.maximum(jnp.max(sc, axis=1, keepdims=True), jnp.max(sn, axis=1, keepdims=True))
        pc = jnp.exp(sc - m)
        pn = jnp.exp(sn - m)
        l = jnp.sum(pc, axis=1, keepdims=True) + jnp.sum(pn, axis=1, keepdims=True)
        o2 = (jnp.dot(pc.astype(_BF16), vc, preferred_element_type=_F32)
              + jnp.dot(pn.astype(_BF16), vn_ref[:, cols], preferred_element_type=_F32)) / l
        o = o2[0:steps, :] - lam * o2[steps:, :]
        o_ref[:, cols] = _sub_norm(o, sg_ref, lam_init).astype(o_ref.dtype)


def _sample_attn(q, k_new, v_new, cache_k, cache_v, lam, sg, *, lam_init, seqs):
    m, width = q.shape
    steps = m // seqs
    past = cache_k.shape[1]
    group_w = SAMPLE_HEADS_PER_STEP * HEAD_W
    groups = width // group_w
    kern = functools.partial(_sample_attn_kernel, lam_init=lam_init, steps=steps)
    new = pl.BlockSpec((steps, group_w), lambda b, g: (0, b * groups + g))
    cache = pl.BlockSpec((1, past, group_w), lambda b, g: (b, 0, g))
    out = pl.pallas_call(
        kern,
        grid=(seqs, groups),
        in_specs=[new, new, new, cache, cache,
                  pl.BlockSpec((4, QK_DIM), lambda b, g: (0, 0)),
                  pl.BlockSpec((1, V_DIM), lambda b, g: (0, 0))],
        out_specs=new,
        out_shape=jax.ShapeDtypeStruct((steps, seqs * width), _BF16),
        compiler_params=_params(2),
        name="sample_attn",
    )(q.reshape(steps, seqs * width), k_new.reshape(steps, seqs * width),
      v_new.reshape(steps, seqs * width), cache_k, cache_v, lam, sg)
    return out.reshape(m, width)


def _oproj_ln_kernel(x_ref, a_ref, w_ref, g_ref, b_ref, o_ref):
    mix = jnp.dot(a_ref[...], w_ref[...], preferred_element_type=_F32)
    o_ref[...] = _layer_norm(ALPHA * x_ref[...] + mix, g_ref[...], b_ref[...])


def _oproj_ln(x, attn, w_o, g, b):
    m = x.shape[0]
    rows = min(ROW_TILE, m)
    assert m % rows == 0
    const = lambda i: (0, 0)
    return pl.pallas_call(
        _oproj_ln_kernel,
        grid=(m // rows,),
        in_specs=[pl.BlockSpec((rows, D_MODEL), lambda i: (i, 0)),
                  pl.BlockSpec((rows, N_HEADS * V_DIM), lambda i: (i, 0)),
                  pl.BlockSpec((N_HEADS * V_DIM, D_MODEL), const),
                  pl.BlockSpec((1, D_MODEL), const),
                  pl.BlockSpec((1, D_MODEL), const)],
        out_specs=pl.BlockSpec((rows, D_MODEL), lambda i: (i, 0)),
        out_shape=jax.ShapeDtypeStruct((m, D_MODEL), _F32),
        compiler_params=_params(1),
        name="oproj_ln",
    )(x, attn, w_o, g, b)


def _run_group(x, pool_hist, conv_hist, caches, pos0, seqs, p):
    layer_inputs, new_conv = [], []
    k_f32 = v_f32 = k_b = v_b = None
    for l in range(DEPTH):
        g1, b1 = p['ln1_g'][l:l + 1], p['ln1_b'][l:l + 1]
        if l < N_A_LAYERS:
            layer_inputs.append(x)
            x = _pool_ln(x, pool_hist[l], p['w_pool'][l], p['pool_scale'][l:l + 1], g1, b1,
                         seqs=seqs, pos0=pos0)
        else:
            j = l - N_A_LAYERS
            lam_init = 0.8 - 0.6 * math.exp(-0.3 * l)
            lam, sg = p['lam'][j], p['subln_g'][j:j + 1]
            (q,) = _proj(x, p['w_q'][j], (_BF16,), scale=ATTN_SCALE)
            if caches is None:
                attn = _flash_prompt(q, k_b, v_b, lam, sg, lam_init=lam_init)
            else:
                attn = _sample_attn(q, k_b, v_b, caches[0], caches[1], lam, sg,
                                    lam_init=lam_init, seqs=seqs)
            x = _oproj_ln(x, attn, p['w_o'][j], g1, b1)
        x, new_a, new_v = _ffn_ln(x, conv_hist[l], p['w_up'][l], p['conv_w'][l], p['conv_b'][l:l + 1],
                                  p['w_down'][l], p['ln2_g'][l:l + 1], p['ln2_b'][l:l + 1], seqs=seqs)
        new_conv.append(jnp.concatenate([_untile_cols(new_a), _untile_cols(new_v)], axis=1))
        if l == N_A_LAYERS - 1:
            k_f32, k_b = _proj(x, p['w_k'], (_F32, _BF16))
            v_f32, v_b = _proj(x, p['w_v'], (_F32, _BF16))
    return x, layer_inputs, new_conv, k_f32, v_f32


def _untile_cols(a):
    tiles, rows, w = a.shape
    return jnp.transpose(a, (1, 0, 2)).reshape(rows, tiles * w)


def _to_time_major(a):
    seqs, steps, w = a.shape
    return jnp.transpose(a, (1, 0, 2)).reshape(steps * seqs, w)


def _to_batch_major(a, seqs):
    m, w = a.shape
    return jnp.transpose(a.reshape(m // seqs, seqs, w), (1, 0, 2))


def kernel(x_prompt, x_sample, state_pool, state_ffn_conv, cache_k, cache_v, ln1_g, ln1_b, ln2_g, ln2_b,
           w_pool, pool_scale, w_up, conv_w, conv_b, w_down, w_kv, w_q, lam, subln_g, w_o):
    k_width = N_HEADS * HEAD_W
    w_kv_b = w_kv.astype(_BF16)
    p = dict(ln1_g=ln1_g, ln1_b=ln1_b, ln2_g=ln2_g, ln2_b=ln2_b, pool_scale=pool_scale,
             conv_w=conv_w, conv_b=conv_b, lam=lam, subln_g=subln_g,
             w_pool=w_pool.astype(_BF16), w_up=w_up.astype(_BF16), w_down=w_down.astype(_BF16),
             w_k=w_kv_b[:, :k_width], w_v=w_kv_b[:, k_width:],
             w_q=w_q.astype(_BF16), w_o=w_o.astype(_BF16))

    def finish(x, layer_inputs, new_conv, k_new, v_new, seqs):
        steps = x.shape[0] // seqs
        y = _to_batch_major(x, seqs)
        pool = jnp.stack([_to_batch_major(u[(steps - POOL_HIST) * seqs:], seqs) for u in layer_inputs])
        conv = jnp.stack([_to_batch_major(c, seqs) for c in new_conv])
        k_new = _to_batch_major(k_new, seqs).reshape(seqs, steps, N_HEADS, HEAD_W)
        v_new = _to_batch_major(v_new, seqs).reshape(seqs, steps, N_HEADS, V_DIM)
        return y, pool, conv, k_new, v_new

    b, t, _ = x_prompt.shape
    assert b == 1 and t >= POOL_HIST
    zeros_pool = jnp.zeros(((POOL_HIST + 1) * b, D_MODEL), _F32)
    zeros_conv = jnp.zeros(((CONV_W - 1) * b, 2 * D_FF), _F32)
    out_p = _run_group(_to_time_major(x_prompt), [zeros_pool] * N_A_LAYERS, [zeros_conv] * DEPTH,
                       None, 0, b, p)
    y_p, pool_p, conv_p, k_p, v_p = finish(*out_p, b)

    sb, steps, _ = x_sample.shape
    past = cache_k.shape[1]
    assert steps >= POOL_HIST
    pad = jnp.zeros((sb, D_MODEL), _F32)
    pool_hist = [jnp.concatenate([pad, _to_time_major(state_pool[l])], axis=0) for l in range(N_A_LAYERS)]
    conv_hist = [_to_time_major(state_ffn_conv[l]) for l in range(DEPTH)]
    caches = (cache_k.reshape(sb, past, k_width), cache_v.reshape(sb, past, N_HEADS * V_DIM))
    out_s = _run_group(_to_time_major(x_sample), pool_hist, conv_hist, caches, past, sb, p)
    y_s, pool_s, conv_s, k_s, v_s = finish(*out_s, sb)

    return (y_p, y_s, pool_p, pool_s, conv_p, conv_s, k_p, v_p, k_s, v_s)
```

```python
import functools
import math

import jax
import jax.numpy as jnp
from jax import lax
from jax.experimental import pallas as pl
from jax.experimental.pallas import tpu as pltpu

D_MODEL = 2048
DEPTH = 4
N_A_LAYERS = DEPTH // 2
CHUNK = 64
POOL_WINDOWS = (2, 4, 8, 16)
POOL_GROUP = D_MODEL // len(POOL_WINDOWS)
POOL_HIST = max(POOL_WINDOWS) - 1
N_HEADS = 16
QK_DIM = 64
V_DIM = 2 * QK_DIM
HEAD_W = 2 * QK_DIM
D_FF = 5632
CONV_W = 3
LN_EPS = 1e-5
ALPHA = (2 * DEPTH) ** 0.25
ATTN_SCALE = QK_DIM ** -0.5
Q_SCALE = ATTN_SCALE * math.log2(math.e)
NEG_INF = -1e30

V7X_VMEM_BYTES = 64 * 1024 * 1024
VMEM_LIMIT_BYTES = V7X_VMEM_BYTES - 8 * 1024 * 1024
SUBLANES = 8

ROW_TILE = 512
FF_TILE = 512
N_FF_TILES = D_FF // FF_TILE
MM_COL_TILE = 1024
Q_TILE = 256
K_TILE = 512
STEP_UNROLL = 4
SAMPLE_HEADS_PER_STEP = 8

_F32 = jnp.float32
_BF16 = jnp.bfloat16


def _params(n_axes):
    return pltpu.CompilerParams(dimension_semantics=("arbitrary",) * n_axes,
                                vmem_limit_bytes=VMEM_LIMIT_BYTES)


def _layer_norm(z, g, b):
    mu = jnp.mean(z, axis=-1, keepdims=True)
    zc = z - mu
    var = jnp.mean(zc * zc, axis=-1, keepdims=True)
    return zc * lax.rsqrt(var + LN_EPS) * g + b


def _gelu_tanh(x):
    cdf = 0.5 * (1.0 + jnp.tanh(math.sqrt(2.0 / math.pi) * (x + 0.044715 * (x * x * x))))
    return x * cdf


def _pool_ln_kernel(x_ref, halo_ref, hist_ref, w_ref, scale_ref, g_ref, b_ref, o_ref,
                    ext_ref, z_ref, *, seqs, rows, pos0):
    i = pl.program_id(0)
    halo_rows = (POOL_HIST + 1) * seqs

    @pl.when(i == 0)
    def _():
        ext_ref[0:halo_rows, :] = hist_ref[...]

    @pl.when(i > 0)
    def _():
        ext_ref[0:halo_rows, :] = halo_ref[...]

    ext_ref[halo_rows:halo_rows + rows, :] = x_ref[...]

    row = lax.broadcasted_iota(jnp.int32, (rows, 1), 0) + i * rows
    step = lax.shift_right_logical(row, int(math.log2(seqs)))
    pos_plus_1 = (step + (pos0 + 1)).astype(_F32)

    for g, w in enumerate(POOL_WINDOWS):
        cols = slice(g * POOL_GROUP, (g + 1) * POOL_GROUP)
        cur = x_ref[:, cols]
        win = cur
        for k in range(1, w):
            start = halo_rows - k * seqs
            win = win + ext_ref[start:start + rows, cols]
        cnt = jnp.minimum(float(w), pos_plus_1)
        d = win / cnt - cur
        mix = jnp.dot(d.astype(_BF16), w_ref[g], preferred_element_type=_F32)
        z_ref[:, cols] = ALPHA * cur + mix * scale_ref[:, cols]

    o_ref[...] = _layer_norm(z_ref[...], g_ref[...], b_ref[...])


def _pool_ln(x, hist, w_pool, scale, g, b, *, seqs, pos0):
    m = x.shape[0]
    rows = min(ROW_TILE, m)
    halo_rows = (POOL_HIST + 1) * seqs
    assert m % rows == 0 and rows % halo_rows == 0 and seqs & (seqs - 1) == 0
    halo_blocks_per_tile = rows // halo_rows
    kern = functools.partial(_pool_ln_kernel, seqs=seqs, rows=rows, pos0=pos0)
    return pl.pallas_call(
        kern,
        grid=(m // rows,),
        in_specs=[
            pl.BlockSpec((rows, D_MODEL), lambda i: (i, 0)),
            pl.BlockSpec((halo_rows, D_MODEL),
                         lambda i: (jnp.maximum(i * halo_blocks_per_tile - 1, 0), 0)),
            pl.BlockSpec((halo_rows, D_MODEL), lambda i: (0, 0)),
            pl.BlockSpec((len(POOL_WINDOWS), POOL_GROUP, POOL_GROUP), lambda i: (0, 0, 0)),
            pl.BlockSpec((1, D_MODEL), lambda i: (0, 0)),
            pl.BlockSpec((1, D_MODEL), lambda i: (0, 0)),
            pl.BlockSpec((1, D_MODEL), lambda i: (0, 0)),
        ],
        out_specs=pl.BlockSpec((rows, D_MODEL), lambda i: (i, 0)),
        out_shape=jax.ShapeDtypeStruct((m, D_MODEL), _F32),
        scratch_shapes=[pltpu.VMEM((halo_rows + rows, D_MODEL), _F32),
                        pltpu.VMEM((rows, D_MODEL), _F32)],
        compiler_params=_params(1),
        name="pool_ln",
    )(x, x, hist, w_pool, scale, g, b)


def _ffn_ln_kernel(x_ref, hist_a_ref, hist_v_ref, wa_ref, wv_ref, cwa_ref, cwv_ref, cba_ref, cbv_ref,
                   wd_ref, g_ref, b_ref, o_ref, new_a_ref, new_v_ref,
                   xb_ref, ext_a_ref, ext_v_ref, *, seqs, rows):
    i = pl.program_id(0)
    j = pl.program_id(1)
    halo = (CONV_W - 1) * seqs
    off = -(-halo // SUBLANES) * SUBLANES

    @pl.when(j == 0)
    def _():
        xb_ref[...] = x_ref[...].astype(_BF16)
        o_ref[...] = jnp.zeros_like(o_ref)

    xb = xb_ref[...]
    ha = jnp.dot(xb, wa_ref[...], preferred_element_type=_F32)
    hv = jnp.dot(xb, wv_ref[...], preferred_element_type=_F32)

    @pl.when(i == 0)
    def _():
        ext_a_ref[off - halo:off, :] = hist_a_ref[...]
        ext_v_ref[off - halo:off, :] = hist_v_ref[...]

    @pl.when(i > 0)
    def _():
        ext_a_ref[off - halo:off, :] = new_a_ref[j]
        ext_v_ref[off - halo:off, :] = new_v_ref[j]

    ext_a_ref[off:off + rows, :] = ha
    ext_v_ref[off:off + rows, :] = hv
    new_a_ref[j] = ha[rows - halo:, :]
    new_v_ref[j] = hv[rows - halo:, :]

    def conv(ext_ref, h, cw_ref, cb_ref):
        cw = cw_ref[...]
        c = cb_ref[...] + cw[0:1, :] * ext_ref[off - 2 * seqs:off - 2 * seqs + rows, :]
        c = c + cw[1:2, :] * ext_ref[off - seqs:off - seqs + rows, :]
        return c + cw[2:3, :] * h

    ca = conv(ext_a_ref, ha, cwa_ref, cba_ref)
    cv = conv(ext_v_ref, hv, cwv_ref, cbv_ref)
    act = (_gelu_tanh(ca) * cv).astype(_BF16)
    o_ref[...] += jnp.dot(act, wd_ref[...], preferred_element_type=_F32)

    @pl.when(j == N_FF_TILES - 1)
    def _():
        o_ref[...] = _layer_norm(ALPHA * x_ref[...] + o_ref[...], g_ref[...], b_ref[...])


def _ffn_ln(x, hist, w_up, conv_w, conv_b, w_down, g, b, *, seqs):
    m = x.shape[0]
    rows = min(ROW_TILE, m)
    halo = (CONV_W - 1) * seqs
    off = -(-halo // SUBLANES) * SUBLANES
    assert m % rows == 0 and rows >= halo
    kern = functools.partial(_ffn_ln_kernel, seqs=seqs, rows=rows)
    a_cols = lambda i, j: (0, j)
    v_cols = lambda i, j: (0, j + N_FF_TILES)
    const = lambda i, j: (0, 0)
    return pl.pallas_call(
        kern,
        grid=(m // rows, N_FF_TILES),
        in_specs=[
            pl.BlockSpec((rows, D_MODEL), lambda i, j: (i, 0)),
            pl.BlockSpec((halo, FF_TILE), a_cols),
            pl.BlockSpec((halo, FF_TILE), v_cols),
            pl.BlockSpec((D_MODEL, FF_TILE), a_cols),
            pl.BlockSpec((D_MODEL, FF_TILE), v_cols),
            pl.BlockSpec((CONV_W, FF_TILE), a_cols),
            pl.BlockSpec((CONV_W, FF_TILE), v_cols),
            pl.BlockSpec((1, FF_TILE), a_cols),
            pl.BlockSpec((1, FF_TILE), v_cols),
            pl.BlockSpec((FF_TILE, D_MODEL), lambda i, j: (j, 0)),
            pl.BlockSpec((1, D_MODEL), const),
            pl.BlockSpec((1, D_MODEL), const),
        ],
        out_specs=[
            pl.BlockSpec((rows, D_MODEL), lambda i, j: (i, 0)),
            pl.BlockSpec((N_FF_TILES, halo, FF_TILE), lambda i, j: (0, 0, 0)),
            pl.BlockSpec((N_FF_TILES, halo, FF_TILE), lambda i, j: (0, 0, 0)),
        ],
        out_shape=[
            jax.ShapeDtypeStruct((m, D_MODEL), _F32),
            jax.ShapeDtypeStruct((N_FF_TILES, halo, FF_TILE), _F32),
            jax.ShapeDtypeStruct((N_FF_TILES, halo, FF_TILE), _F32),
        ],
        scratch_shapes=[
            pltpu.VMEM((rows, D_MODEL), _BF16),
            pltpu.VMEM((off + rows, FF_TILE), _F32),
            pltpu.VMEM((off + rows, FF_TILE), _F32),
        ],
        compiler_params=_params(2),
        name="ffn_ln",
    )(x, hist, hist, w_up, w_up, conv_w, conv_w, conv_b, conv_b, w_down, g, b)


def _proj_kernel(x_ref, w_ref, *refs, scale, n_out):
    out_refs, xb_ref = refs[:n_out], refs[n_out]

    @pl.when(pl.program_id(1) == 0)
    def _():
        xb_ref[...] = x_ref[...].astype(_BF16)

    r = jnp.dot(xb_ref[...], w_ref[...], preferred_element_type=_F32)
    for o_ref in out_refs:
        o_ref[...] = (r if o_ref.dtype == _F32 else r * scale).astype(o_ref.dtype)


def _proj(x, w, out_dtypes, *, scale=1.0):
    m, k = x.shape
    n = w.shape[1]
    rows = min(2 * ROW_TILE, m)
    cols = min(MM_COL_TILE, n)
    assert m % rows == 0 and n % cols == 0
    kern = functools.partial(_proj_kernel, scale=scale, n_out=len(out_dtypes))
    return pl.pallas_call(
        kern,
        grid=(m // rows, n // cols),
        in_specs=[pl.BlockSpec((rows, k), lambda i, j: (i, 0)),
                  pl.BlockSpec((k, cols), lambda i, j: (0, j))],
        out_specs=[pl.BlockSpec((rows, cols), lambda i, j: (i, j)) for _ in out_dtypes],
        out_shape=[jax.ShapeDtypeStruct((m, n), dt) for dt in out_dtypes],
        scratch_shapes=[pltpu.VMEM((rows, k), _BF16)],
        compiler_params=_params(2),
        name="proj",
    )(x, w)


def _diff_lambda(lam_ref, lam_init):
    lam = lam_ref[...]
    e1 = jnp.exp(jnp.sum(lam[0:1, :] * lam[1:2, :], axis=-1, keepdims=True))
    e2 = jnp.exp(jnp.sum(lam[2:3, :] * lam[3:4, :], axis=-1, keepdims=True))
    return e1 - e2 + lam_init


def _split_parts(q):
    first = lax.broadcasted_iota(jnp.int32, (1, HEAD_W), 1) < QK_DIM
    zero = jnp.zeros_like(q)
    return jnp.concatenate([jnp.where(first, q, zero), jnp.where(first, zero, q)], axis=0)


def _sub_norm(o, sg_ref, lam_init):
    return o * lax.rsqrt(jnp.mean(o * o, axis=-1, keepdims=True) + LN_EPS) * sg_ref[...] * (1.0 - lam_init)


def _score(qz, kb):
    return lax.dot_general(qz, kb, (((1,), (1,)), ((), ())), preferred_element_type=_F32)


def _flash_kernel(qt_ref, k_ref, vt_ref, lam_ref, sg_ref, o_ref,
                  qz_ref, s_ref, p_ref, a_ref, m_ref, l_ref, acc_ref, *, lam_init, n_q_blocks):
    lam = _diff_lambda(lam_ref, lam_init)
    n_cols = 2 * Q_TILE
    first_part = lax.broadcasted_iota(jnp.int32, (HEAD_W, 1), 0) < QK_DIM
    chunk_shift = int(math.log2(CHUNK))

    def scores(b, slot):
        k0 = pl.multiple_of(b * K_TILE, K_TILE)
        s_ref[slot] = jnp.dot(k_ref[pl.ds(k0, K_TILE), :], qz_ref[...], preferred_element_type=_F32)

    def softmax(slot, visible):
        s = s_ref[slot]
        if visible is not None:
            s = jnp.where(visible, s, NEG_INF)
        m_prev = m_ref[...]
        m_new = jnp.maximum(m_prev, jnp.max(s, axis=0, keepdims=True))
        a = jnp.exp2(m_prev - m_new)
        p = jnp.exp2(s - m_new)
        l_ref[...] = a * l_ref[...] + jnp.sum(p, axis=0, keepdims=True)
        m_ref[...] = m_new
        a_ref[slot] = a
        p_ref[slot] = p.astype(_BF16)

    def values(b, slot):
        vb = vt_ref[jnp.maximum(b, 0)]
        acc_ref[...] = a_ref[slot] * acc_ref[...] + jnp.dot(vb, p_ref[slot], preferred_element_type=_F32)

    def step(t, slot):
        values(t - 1, 1 - slot)
        softmax(slot, None)
        scores(t + 1, 1 - slot)

    def q_block(qi, carry):
        qt = qt_ref[qi]
        zero = jnp.zeros_like(qt)
        qz_ref[:, 0:Q_TILE] = jnp.where(first_part, qt, zero)
        qz_ref[:, Q_TILE:n_cols] = jnp.where(first_part, zero, qt)
        m_ref[...] = jnp.full(m_ref.shape, NEG_INF, _F32)
        l_ref[...] = jnp.zeros_like(l_ref)
        acc_ref[...] = jnp.zeros_like(acc_ref)
        p_ref[1] = jnp.zeros(p_ref.shape[1:], _BF16)
        a_ref[1] = jnp.ones(a_ref.shape[1:], _F32)

        n_full = lax.div(qi, K_TILE // Q_TILE)
        scores(0, 0)

        def unrolled(u, c):
            for i in range(STEP_UNROLL):
                step(STEP_UNROLL * u + i, i & 1)
            return c

        def single(t, c):
            step(t, t & 1)
            return c

        n_unrolled = lax.div(n_full, STEP_UNROLL)
        lax.fori_loop(0, n_unrolled, unrolled, 0)
        lax.fori_loop(n_unrolled * STEP_UNROLL, n_full, single, 0)

        last = n_full & 1
        values(n_full - 1, 1 - last)
        key = lax.broadcasted_iota(jnp.int32, (K_TILE, 1), 0) + n_full * K_TILE
        col = lax.broadcasted_iota(jnp.int32, (1, n_cols), 1)
        query = jnp.where(col >= Q_TILE, col - Q_TILE, col) + qi * Q_TILE
        visible = (lax.shift_right_logical(key, chunk_shift)
                   <= lax.shift_right_logical(query, chunk_shift))
        softmax(last, visible)
        values(n_full, last)

        o2 = acc_ref[...] / l_ref[...]
        o = o2[:, 0:Q_TILE] - lam * o2[:, Q_TILE:n_cols]
        norm = lax.rsqrt(jnp.mean(o * o, axis=0, keepdims=True) + LN_EPS)
        o_ref[qi] = (o * norm * sg_ref[...] * (1.0 - lam_init)).astype(o_ref.dtype)
        return carry

    lax.fori_loop(0, n_q_blocks, q_block, 0)


def _flash_prompt(q, k, v, lam, sg, *, lam_init):
    t = q.shape[0]
    assert t % K_TILE == 0 and K_TILE % Q_TILE == 0 and Q_TILE % CHUNK == 0 and STEP_UNROLL % 2 == 0
    nq, nk = t // Q_TILE, t // K_TILE
    qt = jnp.transpose(q.reshape(nq, Q_TILE, N_HEADS, HEAD_W), (2, 0, 3, 1))
    vt = jnp.transpose(v.reshape(nk, K_TILE, N_HEADS, V_DIM), (2, 0, 3, 1))
    kern = functools.partial(_flash_kernel, lam_init=lam_init, n_q_blocks=nq)
    out = pl.pallas_call(
        kern,
        grid=(N_HEADS,),
        in_specs=[pl.BlockSpec((None, nq, HEAD_W, Q_TILE), lambda h: (h, 0, 0, 0)),
                  pl.BlockSpec((t, HEAD_W), lambda h: (0, h)),
                  pl.BlockSpec((None, nk, V_DIM, K_TILE), lambda h: (h, 0, 0, 0)),
                  pl.BlockSpec((4, QK_DIM), lambda h: (0, 0)),
                  pl.BlockSpec((V_DIM, 1), lambda h: (0, 0))],
        out_specs=pl.BlockSpec((None, nq, V_DIM, Q_TILE), lambda h: (h, 0, 0, 0)),
        out_shape=jax.ShapeDtypeStruct((N_HEADS, nq, V_DIM, Q_TILE), _BF16),
        scratch_shapes=[pltpu.VMEM((HEAD_W, 2 * Q_TILE), _BF16),
                        pltpu.VMEM((2, K_TILE, 2 * Q_TILE), _F32),
                        pltpu.VMEM((2, K_TILE, 2 * Q_TILE), _BF16),
                        pltpu.VMEM((2, 1, 2 * Q_TILE), _F32),
                        pltpu.VMEM((1, 2 * Q_TILE), _F32),
                        pltpu.VMEM((1, 2 * Q_TILE), _F32),
                        pltpu.VMEM((V_DIM, 2 * Q_TILE), _F32)],
        compiler_params=_params(1),
        name="flash_prompt",
    )(qt, k, vt, lam, sg.reshape(V_DIM, 1))
    return jnp.transpose(out, (1, 3, 0, 2)).reshape(t, N_HEADS * V_DIM)


def _sample_attn_kernel(q_ref, kn_ref, vn_ref, kc_ref, vc_ref, lam_ref, sg_ref, o_ref, *, lam_init, steps):
    lam = _diff_lambda(lam_ref, lam_init)
    for hh in range(SAMPLE_HEADS_PER_STEP):
        cols = slice(hh * HEAD_W, (hh + 1) * HEAD_W)
        qz = _split_parts(q_ref[:, cols])
        kc = kc_ref[0, :, hh, :].astype(_BF16)
        vc = vc_ref[0, :, hh, :].astype(_BF16)
        sc = _score(qz, kc)
        sn = _score(qz, kn_ref[:, cols])
        m = jnp.maximum(jnp.max(sc, axis=1, keepdims=True), jnp.max(sn, axis=1, keepdims=True))
        pc = jnp.exp2(sc - m)
        pn = jnp.exp2(sn - m)
        l = jnp.sum(pc, axis=1, keepdims=True) + jnp.sum(pn, axis=1, keepdims=True)
        o2 = (jnp.dot(pc.astype(_BF16), vc, preferred_element_type=_F32)
              + jnp.dot(pn.astype(_BF16), vn_ref[:, cols], preferred_element_type=_F32)) / l
        o = o2[0:steps, :] - lam * o2[steps:, :]
        o_ref[:, cols] = _sub_norm(o, sg_ref, lam_init).astype(o_ref.dtype)


def _sample_attn(q, k_new, v_new, cache_k, cache_v, lam, sg, *, lam_init, seqs):
    m, width = q.shape
    steps = m // seqs
    past = cache_k.shape[1]
    group_w = SAMPLE_HEADS_PER_STEP * HEAD_W
    groups = width // group_w
    kern = functools.partial(_sample_attn_kernel, lam_init=lam_init, steps=steps)
    new = pl.BlockSpec((steps, group_w), lambda b, g: (0, b * groups + g))
    cache = pl.BlockSpec((1, past, SAMPLE_HEADS_PER_STEP, HEAD_W), lambda b, g: (b, 0, g, 0))
    out = pl.pallas_call(
        kern,
        grid=(seqs, groups),
        in_specs=[new, new, new, cache, cache,
                  pl.BlockSpec((4, QK_DIM), lambda b, g: (0, 0)),
                  pl.BlockSpec((1, V_DIM), lambda b, g: (0, 0))],
        out_specs=new,
        out_shape=jax.ShapeDtypeStruct((steps, seqs * width), _BF16),
        compiler_params=_params(2),
        name="sample_attn",
    )(q.reshape(steps, seqs * width), k_new.reshape(steps, seqs * width),
      v_new.reshape(steps, seqs * width), cache_k, cache_v, lam, sg)
    return out.reshape(m, width)


def _oproj_ln_kernel(x_ref, a_ref, w_ref, g_ref, b_ref, o_ref):
    mix = jnp.dot(a_ref[...], w_ref[...], preferred_element_type=_F32)
    o_ref[...] = _layer_norm(ALPHA * x_ref[...] + mix, g_ref[...], b_ref[...])


def _oproj_ln(x, attn, w_o, g, b):
    m = x.shape[0]
    rows = min(ROW_TILE, m)
    assert m % rows == 0
    const = lambda i: (0, 0)
    return pl.pallas_call(
        _oproj_ln_kernel,
        grid=(m // rows,),
        in_specs=[pl.BlockSpec((rows, D_MODEL), lambda i: (i, 0)),
                  pl.BlockSpec((rows, N_HEADS * V_DIM), lambda i: (i, 0)),
                  pl.BlockSpec((N_HEADS * V_DIM, D_MODEL), const),
                  pl.BlockSpec((1, D_MODEL), const),
                  pl.BlockSpec((1, D_MODEL), const)],
        out_specs=pl.BlockSpec((rows, D_MODEL), lambda i: (i, 0)),
        out_shape=jax.ShapeDtypeStruct((m, D_MODEL), _F32),
        compiler_params=_params(1),
        name="oproj_ln",
    )(x, attn, w_o, g, b)


def _run_group(x, pool_hist, conv_hist, caches, pos0, seqs, p):
    layer_inputs, new_conv = [], []
    k_f32 = v_f32 = k_b = v_b = None
    for l in range(DEPTH):
        g1, b1 = p['ln1_g'][l:l + 1], p['ln1_b'][l:l + 1]
        if l < N_A_LAYERS:
            layer_inputs.append(x)
            x = _pool_ln(x, pool_hist[l], p['w_pool'][l], p['pool_scale'][l:l + 1], g1, b1,
                         seqs=seqs, pos0=pos0)
        else:
            j = l - N_A_LAYERS
            lam_init = 0.8 - 0.6 * math.exp(-0.3 * l)
            lam, sg = p['lam'][j], p['subln_g'][j:j + 1]
            (q,) = _proj(x, p['w_q'][j], (_BF16,), scale=Q_SCALE)
            if caches is None:
                attn = _flash_prompt(q, k_b, v_b, lam, sg, lam_init=lam_init)
            else:
                attn = _sample_attn(q, k_b, v_b, caches[0], caches[1], lam, sg,
                                    lam_init=lam_init, seqs=seqs)
            x = _oproj_ln(x, attn, p['w_o'][j], g1, b1)
        x, new_a, new_v = _ffn_ln(x, conv_hist[l], p['w_up'][l], p['conv_w'][l], p['conv_b'][l:l + 1],
                                  p['w_down'][l], p['ln2_g'][l:l + 1], p['ln2_b'][l:l + 1], seqs=seqs)
        new_conv.append(jnp.concatenate([_untile_cols(new_a), _untile_cols(new_v)], axis=1))
        if l == N_A_LAYERS - 1:
            k_f32, k_b = _proj(x, p['w_k'], (_F32, _BF16))
            v_f32, v_b = _proj(x, p['w_v'], (_F32, _BF16))
    return x, layer_inputs, new_conv, k_f32, v_f32


def _untile_cols(a):
    tiles, rows, w = a.shape
    return jnp.transpose(a, (1, 0, 2)).reshape(rows, tiles * w)


def _to_time_major(a):
    seqs, steps, w = a.shape
    return jnp.transpose(a, (1, 0, 2)).reshape(steps * seqs, w)


def _to_batch_major(a, seqs):
    m, w = a.shape
    return jnp.transpose(a.reshape(m // seqs, seqs, w), (1, 0, 2))


def kernel(x_prompt, x_sample, state_pool, state_ffn_conv, cache_k, cache_v, ln1_g, ln1_b, ln2_g, ln2_b,
           w_pool, pool_scale, w_up, conv_w, conv_b, w_down, w_kv, w_q, lam, subln_g, w_o):
    k_width = N_HEADS * HEAD_W
    w_kv_b = w_kv.astype(_BF16)
    p = dict(ln1_g=ln1_g, ln1_b=ln1_b, ln2_g=ln2_g, ln2_b=ln2_b, pool_scale=pool_scale,
             conv_w=conv_w, conv_b=conv_b, lam=lam, subln_g=subln_g,
             w_pool=w_pool.astype(_BF16), w_up=w_up.astype(_BF16), w_down=w_down.astype(_BF16),
             w_k=w_kv_b[:, :k_width], w_v=w_kv_b[:, k_width:],
             w_q=w_q.astype(_BF16), w_o=w_o.astype(_BF16))

    def finish(x, layer_inputs, new_conv, k_new, v_new, seqs):
        steps = x.shape[0] // seqs
        y = _to_batch_major(x, seqs)
        pool = jnp.stack([_to_batch_major(u[(steps - POOL_HIST) * seqs:], seqs) for u in layer_inputs])
        conv = jnp.stack([_to_batch_major(c, seqs) for c in new_conv])
        k_new = _to_batch_major(k_new, seqs).reshape(seqs, steps, N_HEADS, HEAD_W)
        v_new = _to_batch_major(v_new, seqs).reshape(seqs, steps, N_HEADS, V_DIM)
        return y, pool, conv, k_new, v_new

    b, t, _ = x_prompt.shape
    assert b == 1 and t >= POOL_HIST
    zeros_pool = jnp.zeros(((POOL_HIST + 1) * b, D_MODEL), _F32)
    zeros_conv = jnp.zeros(((CONV_W - 1) * b, 2 * D_FF), _F32)
    out_p = _run_group(_to_time_major(x_prompt), [zeros_pool] * N_A_LAYERS, [zeros_conv] * DEPTH,
                       None, 0, b, p)
    y_p, pool_p, conv_p, k_p, v_p = finish(*out_p, b)

    sb, steps, _ = x_sample.shape
    past = cache_k.shape[1]
    assert steps >= POOL_HIST
    pad = jnp.zeros((sb, D_MODEL), _F32)
    pool_hist = [jnp.concatenate([pad, _to_time_major(state_pool[l])], axis=0) for l in range(N_A_LAYERS)]
    conv_hist = [_to_time_major(state_ffn_conv[l]) for l in range(DEPTH)]
    out_s = _run_group(_to_time_major(x_sample), pool_hist, conv_hist, (cache_k, cache_v), past, sb, p)
    y_s, pool_s, conv_s, k_s, v_s = finish(*out_s, sb)

    return (y_p, y_s, pool_p, pool_s, conv_p, conv_s, k_p, v_p, k_s, v_s)
```

```python
import functools
import math

import jax
import jax.numpy as jnp
from jax import lax
from jax.experimental import pallas as pl
from jax.experimental.pallas import tpu as pltpu

D_MODEL = 2048
DEPTH = 4
N_A_LAYERS = DEPTH // 2
CHUNK = 64
POOL_WINDOWS = (2, 4, 8, 16)
POOL_GROUP = D_MODEL // len(POOL_WINDOWS)
POOL_HIST = max(POOL_WINDOWS) - 1
N_HEADS = 16
QK_DIM = 64
V_DIM = 2 * QK_DIM
HEAD_W = 2 * QK_DIM
D_FF = 5632
CONV_W = 3
LN_EPS = 1e-5
ALPHA = (2 * DEPTH) ** 0.25
ATTN_SCALE = QK_DIM ** -0.5
Q_SCALE = ATTN_SCALE * math.log2(math.e)
NEG_INF = -1e30

V7X_VMEM_BYTES = 64 * 1024 * 1024
VMEM_LIMIT_BYTES = V7X_VMEM_BYTES - 8 * 1024 * 1024
SUBLANES = 8

ROW_TILE = 512
FF_TILE = 512
N_FF_TILES = D_FF // FF_TILE
MM_COL_TILE = 1024
Q_TILE = 256
K_TILE = 512
STEP_UNROLL = 4
SAMPLE_HEADS_PER_STEP = 8

_F32 = jnp.float32
_BF16 = jnp.bfloat16


def _params(n_axes):
    return pltpu.CompilerParams(dimension_semantics=("arbitrary",) * n_axes,
                                vmem_limit_bytes=VMEM_LIMIT_BYTES)


def _layer_norm(z, g, b):
    mu = jnp.mean(z, axis=-1, keepdims=True)
    zc = z - mu
    var = jnp.mean(zc * zc, axis=-1, keepdims=True)
    return zc * lax.rsqrt(var + LN_EPS) * g + b


def _gelu_tanh(x):
    cdf = 0.5 * (1.0 + jnp.tanh(math.sqrt(2.0 / math.pi) * (x + 0.044715 * (x * x * x))))
    return x * cdf


def _pool_ln_kernel(x_ref, halo_ref, hist_ref, w_ref, scale_ref, g_ref, b_ref, o_ref, ob_ref,
                    ext_ref, z_ref, *, seqs, rows, pos0):
    i = pl.program_id(0)
    halo_rows = (POOL_HIST + 1) * seqs

    @pl.when(i == 0)
    def _():
        ext_ref[0:halo_rows, :] = hist_ref[...]

    @pl.when(i > 0)
    def _():
        ext_ref[0:halo_rows, :] = halo_ref[...]

    ext_ref[halo_rows:halo_rows + rows, :] = x_ref[...]

    row = lax.broadcasted_iota(jnp.int32, (rows, 1), 0) + i * rows
    step = lax.shift_right_logical(row, int(math.log2(seqs)))
    pos_plus_1 = (step + (pos0 + 1)).astype(_F32)

    for g, w in enumerate(POOL_WINDOWS):
        cols = slice(g * POOL_GROUP, (g + 1) * POOL_GROUP)
        cur = x_ref[:, cols]
        win = cur
        for k in range(1, w):
            start = halo_rows - k * seqs
            win = win + ext_ref[start:start + rows, cols]
        cnt = jnp.minimum(float(w), pos_plus_1)
        d = win / cnt - cur
        mix = jnp.dot(d.astype(_BF16), w_ref[g], preferred_element_type=_F32)
        z_ref[:, cols] = ALPHA * cur + mix * scale_ref[:, cols]

    out = _layer_norm(z_ref[...], g_ref[...], b_ref[...])
    o_ref[...] = out
    ob_ref[...] = out.astype(_BF16)


def _pool_ln(x, hist, w_pool, scale, g, b, *, seqs, pos0):
    m = x.shape[0]
    rows = min(ROW_TILE, m)
    halo_rows = (POOL_HIST + 1) * seqs
    assert m % rows == 0 and rows % halo_rows == 0 and seqs & (seqs - 1) == 0
    halo_blocks_per_tile = rows // halo_rows
    kern = functools.partial(_pool_ln_kernel, seqs=seqs, rows=rows, pos0=pos0)
    return pl.pallas_call(
        kern,
        grid=(m // rows,),
        in_specs=[
            pl.BlockSpec((rows, D_MODEL), lambda i: (i, 0)),
            pl.BlockSpec((halo_rows, D_MODEL),
                         lambda i: (jnp.maximum(i * halo_blocks_per_tile - 1, 0), 0)),
            pl.BlockSpec((halo_rows, D_MODEL), lambda i: (0, 0)),
            pl.BlockSpec((len(POOL_WINDOWS), POOL_GROUP, POOL_GROUP), lambda i: (0, 0, 0)),
            pl.BlockSpec((1, D_MODEL), lambda i: (0, 0)),
            pl.BlockSpec((1, D_MODEL), lambda i: (0, 0)),
            pl.BlockSpec((1, D_MODEL), lambda i: (0, 0)),
        ],
        out_specs=[pl.BlockSpec((rows, D_MODEL), lambda i: (i, 0))] * 2,
        out_shape=[jax.ShapeDtypeStruct((m, D_MODEL), _F32), jax.ShapeDtypeStruct((m, D_MODEL), _BF16)],
        scratch_shapes=[pltpu.VMEM((halo_rows + rows, D_MODEL), _F32),
                        pltpu.VMEM((rows, D_MODEL), _F32)],
        compiler_params=_params(1),
        name="pool_ln",
    )(x, x, hist, w_pool, scale, g, b)


def _ffn_ln_kernel(xb_ref, x_ref, hist_a_ref, hist_v_ref, wa_ref, wv_ref, cwa_ref, cwv_ref, cba_ref, cbv_ref,
                   wd_ref, g_ref, b_ref, *refs, seqs, rows, n_items, bf16_copy):
    if bf16_copy:
        o_ref, ob_ref, new_a_ref, new_v_ref, ext_a_ref, ext_v_ref = refs
    else:
        (o_ref, new_a_ref, new_v_ref, ext_a_ref, ext_v_ref), ob_ref = refs, None
    s = pl.program_id(0)
    halo = (CONV_W - 1) * seqs
    off = -(-halo // SUBLANES) * SUBLANES
    a_item = jnp.minimum(s, n_items - 1)
    i_a = lax.div(a_item, N_FF_TILES)
    j_a = a_item - i_a * N_FF_TILES
    j_b = lax.rem(jnp.maximum(s - 1, 0), N_FF_TILES)
    parity = s & 1

    @pl.when(s == 0)
    def _():
        ext_a_ref[1] = jnp.zeros(ext_a_ref.shape[1:], _F32)
        ext_v_ref[1] = jnp.zeros(ext_v_ref.shape[1:], _F32)

    @pl.when(i_a == 0)
    def _():
        ext_a_ref[parity, off - halo:off, :] = hist_a_ref[...]
        ext_v_ref[parity, off - halo:off, :] = hist_v_ref[...]

    @pl.when(i_a > 0)
    def _():
        ext_a_ref[parity, off - halo:off, :] = new_a_ref[j_a]
        ext_v_ref[parity, off - halo:off, :] = new_v_ref[j_a]

    @pl.when(j_b == 0)
    def _():
        o_ref[...] = jnp.zeros_like(o_ref)

    def both_halves(slot):
        xb = xb_ref[...]
        ha = jnp.dot(xb, wa_ref[...], preferred_element_type=_F32)
        hv = jnp.dot(xb, wv_ref[...], preferred_element_type=_F32)
        ext_a_ref[slot, off:off + rows, :] = ha
        ext_v_ref[slot, off:off + rows, :] = hv
        new_a_ref[j_a] = ha[rows - halo:, :]
        new_v_ref[j_a] = hv[rows - halo:, :]

        def conv(ext_ref, cw_ref, cb_ref):
            cw = cw_ref[...]
            c = cb_ref[...]
            for tap in range(CONV_W):
                start = off - (CONV_W - 1 - tap) * seqs
                c = c + cw[tap:tap + 1, :] * ext_ref[1 - slot, start:start + rows, :]
            return c

        ca = conv(ext_a_ref, cwa_ref, cba_ref)
        cv = conv(ext_v_ref, cwv_ref, cbv_ref)
        act = (_gelu_tanh(ca) * cv).astype(_BF16)
        o_ref[...] += jnp.dot(act, wd_ref[...], preferred_element_type=_F32)

    for slot in (0, 1):
        pl.when(parity == slot)(functools.partial(both_halves, slot))

    @pl.when(jnp.logical_and(s > 0, j_b == N_FF_TILES - 1))
    def _():
        out = _layer_norm(ALPHA * x_ref[...] + o_ref[...], g_ref[...], b_ref[...])
        o_ref[...] = out
        if ob_ref is not None:
            ob_ref[...] = out.astype(_BF16)


def _ffn_ln(xb, x, hist, w_up, conv_w, conv_b, w_down, g, b, *, seqs, bf16_copy):
    m = x.shape[0]
    rows = min(ROW_TILE, m)
    halo = (CONV_W - 1) * seqs
    off = -(-halo // SUBLANES) * SUBLANES
    assert m % rows == 0 and rows >= halo
    n_items = (m // rows) * N_FF_TILES
    kern = functools.partial(_ffn_ln_kernel, seqs=seqs, rows=rows, n_items=n_items, bf16_copy=bf16_copy)

    def up_item(s):
        item = jnp.minimum(s, n_items - 1)
        return lax.div(item, N_FF_TILES), lax.rem(item, N_FF_TILES)

    def down_item(s):
        item = jnp.maximum(s - 1, 0)
        return lax.div(item, N_FF_TILES), lax.rem(item, N_FF_TILES)

    up_a = lambda s: (0, up_item(s)[1])
    up_v = lambda s: (0, up_item(s)[1] + N_FF_TILES)
    down_a = lambda s: (0, down_item(s)[1])
    down_v = lambda s: (0, down_item(s)[1] + N_FF_TILES)
    const = lambda s: (0, 0)
    down_rows = pl.BlockSpec((rows, D_MODEL), lambda s: (down_item(s)[0], 0))
    new_hist = pl.BlockSpec((N_FF_TILES, halo, FF_TILE), lambda s: (0, 0, 0))
    row_out = [jax.ShapeDtypeStruct((m, D_MODEL), _F32)]
    if bf16_copy:
        row_out.append(jax.ShapeDtypeStruct((m, D_MODEL), _BF16))
    return pl.pallas_call(
        kern,
        grid=(n_items + 1,),
        in_specs=[
            pl.BlockSpec((rows, D_MODEL), lambda s: (up_item(s)[0], 0)),
            down_rows,
            pl.BlockSpec((halo, FF_TILE), up_a),
            pl.BlockSpec((halo, FF_TILE), up_v),
            pl.BlockSpec((D_MODEL, FF_TILE), up_a),
            pl.BlockSpec((D_MODEL, FF_TILE), up_v),
            pl.BlockSpec((CONV_W, FF_TILE), down_a),
            pl.BlockSpec((CONV_W, FF_TILE), down_v),
            pl.BlockSpec((1, FF_TILE), down_a),
            pl.BlockSpec((1, FF_TILE), down_v),
            pl.BlockSpec((FF_TILE, D_MODEL), lambda s: (down_item(s)[1], 0)),
            pl.BlockSpec((1, D_MODEL), const),
            pl.BlockSpec((1, D_MODEL), const),
        ],
        out_specs=[down_rows] * len(row_out) + [new_hist, new_hist],
        out_shape=row_out + [jax.ShapeDtypeStruct((N_FF_TILES, halo, FF_TILE), _F32)] * 2,
        scratch_shapes=[
            pltpu.VMEM((2, off + rows, FF_TILE), _F32),
            pltpu.VMEM((2, off + rows, FF_TILE), _F32),
        ],
        compiler_params=_params(1),
        name="ffn_ln",
    )(xb, x, hist, hist, w_up, w_up, conv_w, conv_w, conv_b, conv_b, w_down, g, b)


def _proj_kernel(x_ref, w_ref, *out_refs, scale):
    r = jnp.dot(x_ref[...], w_ref[...], preferred_element_type=_F32)
    for o_ref in out_refs:
        o_ref[...] = (r if o_ref.dtype == _F32 else r * scale).astype(o_ref.dtype)


def _proj(x, w, out_dtypes, *, scale=1.0):
    m, k = x.shape
    n = w.shape[1]
    rows = min(2 * ROW_TILE, m)
    cols = min(MM_COL_TILE, n)
    assert m % rows == 0 and n % cols == 0
    return pl.pallas_call(
        functools.partial(_proj_kernel, scale=scale),
        grid=(m // rows, n // cols),
        in_specs=[pl.BlockSpec((rows, k), lambda i, j: (i, 0)),
                  pl.BlockSpec((k, cols), lambda i, j: (0, j))],
        out_specs=[pl.BlockSpec((rows, cols), lambda i, j: (i, j)) for _ in out_dtypes],
        out_shape=[jax.ShapeDtypeStruct((m, n), dt) for dt in out_dtypes],
        compiler_params=_params(2),
        name="proj",
    )(x, w)


def _diff_lambda(lam_ref, lam_init):
    lam = lam_ref[...]
    e1 = jnp.exp(jnp.sum(lam[0:1, :] * lam[1:2, :], axis=-1, keepdims=True))
    e2 = jnp.exp(jnp.sum(lam[2:3, :] * lam[3:4, :], axis=-1, keepdims=True))
    return e1 - e2 + lam_init


def _split_parts(q):
    first = lax.broadcasted_iota(jnp.int32, (1, HEAD_W), 1) < QK_DIM
    zero = jnp.zeros_like(q)
    return jnp.concatenate([jnp.where(first, q, zero), jnp.where(first, zero, q)], axis=0)


def _sub_norm(o, sg_ref, lam_init):
    return o * lax.rsqrt(jnp.mean(o * o, axis=-1, keepdims=True) + LN_EPS) * sg_ref[...] * (1.0 - lam_init)


def _score(qz, kb):
    return lax.dot_general(qz, kb, (((1,), (1,)), ((), ())), preferred_element_type=_F32)


def _flash_kernel(qt_ref, k_ref, vt_ref, lam_ref, sg_ref, o_ref,
                  qz_ref, s_ref, p_ref, a_ref, m_ref, l_ref, acc_ref, *, lam_init, n_q_blocks):
    lam = _diff_lambda(lam_ref, lam_init)
    n_cols = 2 * Q_TILE
    first_part = lax.broadcasted_iota(jnp.int32, (HEAD_W, 1), 0) < QK_DIM
    chunk_shift = int(math.log2(CHUNK))

    def scores(b, slot):
        k0 = pl.multiple_of(b * K_TILE, K_TILE)
        s_ref[slot] = jnp.dot(k_ref[pl.ds(k0, K_TILE), :], qz_ref[...], preferred_element_type=_F32)

    def softmax(slot, visible):
        s = s_ref[slot]
        if visible is not None:
            s = jnp.where(visible, s, NEG_INF)
        m_prev = m_ref[...]
        m_new = jnp.maximum(m_prev, jnp.max(s, axis=0, keepdims=True))
        a = jnp.exp2(m_prev - m_new)
        p = jnp.exp2(s - m_new)
        l_ref[...] = a * l_ref[...] + jnp.sum(p, axis=0, keepdims=True)
        m_ref[...] = m_new
        a_ref[slot] = a
        p_ref[slot] = p.astype(_BF16)

    def values(b, slot):
        vb = vt_ref[jnp.maximum(b, 0)]
        acc_ref[...] = a_ref[slot] * acc_ref[...] + jnp.dot(vb, p_ref[slot], preferred_element_type=_F32)

    def step(t, slot):
        values(t - 1, 1 - slot)
        softmax(slot, None)
        scores(t + 1, 1 - slot)

    def q_block(qi, carry):
        qt = qt_ref[qi]
        zero = jnp.zeros_like(qt)
        qz_ref[:, 0:Q_TILE] = jnp.where(first_part, qt, zero)
        qz_ref[:, Q_TILE:n_cols] = jnp.where(first_part, zero, qt)
        m_ref[...] = jnp.full(m_ref.shape, NEG_INF, _F32)
        l_ref[...] = jnp.zeros_like(l_ref)
        acc_ref[...] = jnp.zeros_like(acc_ref)
        p_ref[1] = jnp.zeros(p_ref.shape[1:], _BF16)
        a_ref[1] = jnp.ones(a_ref.shape[1:], _F32)

        n_full = lax.div(qi, K_TILE // Q_TILE)
        scores(0, 0)

        def unrolled(u, c):
            for i in range(STEP_UNROLL):
                step(STEP_UNROLL * u + i, i & 1)
            return c

        def single(t, c):
            step(t, t & 1)
            return c

        n_unrolled = lax.div(n_full, STEP_UNROLL)
        lax.fori_loop(0, n_unrolled, unrolled, 0)
        lax.fori_loop(n_unrolled * STEP_UNROLL, n_full, single, 0)

        last = n_full & 1
        values(n_full - 1, 1 - last)
        key = lax.broadcasted_iota(jnp.int32, (K_TILE, 1), 0) + n_full * K_TILE
        col = lax.broadcasted_iota(jnp.int32, (1, n_cols), 1)
        query = jnp.where(col >= Q_TILE, col - Q_TILE, col) + qi * Q_TILE
        visible = (lax.shift_right_logical(key, chunk_shift)
                   <= lax.shift_right_logical(query, chunk_shift))
        softmax(last, visible)
        values(n_full, last)

        o2 = acc_ref[...] / l_ref[...]
        o = o2[:, 0:Q_TILE] - lam * o2[:, Q_TILE:n_cols]
        norm = lax.rsqrt(jnp.mean(o * o, axis=0, keepdims=True) + LN_EPS)
        o_ref[qi] = (o * norm * sg_ref[...] * (1.0 - lam_init)).astype(o_ref.dtype)
        return carry

    lax.fori_loop(0, n_q_blocks, q_block, 0)


def _flash_prompt(q, k, v, lam, sg, *, lam_init):
    t = q.shape[0]
    assert t % K_TILE == 0 and K_TILE % Q_TILE == 0 and Q_TILE % CHUNK == 0 and STEP_UNROLL % 2 == 0
    nq, nk = t // Q_TILE, t // K_TILE
    qt = jnp.transpose(q.reshape(nq, Q_TILE, N_HEADS, HEAD_W), (2, 0, 3, 1))
    vt = jnp.transpose(v.reshape(nk, K_TILE, N_HEADS, V_DIM), (2, 0, 3, 1))
    kern = functools.partial(_flash_kernel, lam_init=lam_init, n_q_blocks=nq)
    out = pl.pallas_call(
        kern,
        grid=(N_HEADS,),
        in_specs=[pl.BlockSpec((None, nq, HEAD_W, Q_TILE), lambda h: (h, 0, 0, 0)),
                  pl.BlockSpec((t, HEAD_W), lambda h: (0, h)),
                  pl.BlockSpec((None, nk, V_DIM, K_TILE), lambda h: (h, 0, 0, 0)),
                  pl.BlockSpec((4, QK_DIM), lambda h: (0, 0)),
                  pl.BlockSpec((V_DIM, 1), lambda h: (0, 0))],
        out_specs=pl.BlockSpec((None, nq, V_DIM, Q_TILE), lambda h: (h, 0, 0, 0)),
        out_shape=jax.ShapeDtypeStruct((N_HEADS, nq, V_DIM, Q_TILE), _BF16),
        scratch_shapes=[pltpu.VMEM((HEAD_W, 2 * Q_TILE), _BF16),
                        pltpu.VMEM((2, K_TILE, 2 * Q_TILE), _F32),
                        pltpu.VMEM((2, K_TILE, 2 * Q_TILE), _BF16),
                        pltpu.VMEM((2, 1, 2 * Q_TILE), _F32),
                        pltpu.VMEM((1, 2 * Q_TILE), _F32),
                        pltpu.VMEM((1, 2 * Q_TILE), _F32),
                        pltpu.VMEM((V_DIM, 2 * Q_TILE), _F32)],
        compiler_params=_params(1),
        name="flash_prompt",
    )(qt, k, vt, lam, sg.reshape(V_DIM, 1))
    return jnp.transpose(out, (1, 3, 0, 2)).reshape(t, N_HEADS * V_DIM)


def _sample_attn_kernel(q_ref, kn_ref, vn_ref, kc_ref, vc_ref, lam_ref, sg_ref, o_ref, *, lam_init, steps):
    heads = SAMPLE_HEADS_PER_STEP
    per_head = 2 * steps
    past = kc_ref.shape[1]
    lam = _diff_lambda(lam_ref, lam_init)
    head_cols = [slice(h * HEAD_W, (h + 1) * HEAD_W) for h in range(heads)]
    head_rows = [slice(h * per_head, (h + 1) * per_head) for h in range(heads)]

    qz = jnp.concatenate([_split_parts(q_ref[:, c]) for c in head_cols], axis=0)
    kc = kc_ref[0].reshape(past * heads, HEAD_W).astype(_BF16)
    vc = vc_ref[0].reshape(past * heads, V_DIM).astype(_BF16)
    sc = _score(qz, kc)
    row_head = lax.shift_right_logical(lax.broadcasted_iota(jnp.int32, (heads * per_head, 1), 0),
                                       int(math.log2(per_head)))
    col_head = lax.broadcasted_iota(jnp.int32, (1, past * heads), 1) & (heads - 1)
    sc = jnp.where(row_head == col_head, sc, NEG_INF)
    sn = jnp.concatenate([_score(qz[r, :], kn_ref[:, c]) for r, c in zip(head_rows, head_cols)], axis=0)
    m = jnp.maximum(jnp.max(sc, axis=1, keepdims=True), jnp.max(sn, axis=1, keepdims=True))
    pc = jnp.exp2(sc - m)
    pn = jnp.exp2(sn - m)
    l = jnp.sum(pc, axis=1, keepdims=True) + jnp.sum(pn, axis=1, keepdims=True)
    pn = pn.astype(_BF16)
    on = jnp.concatenate([jnp.dot(pn[r, :], vn_ref[:, c], preferred_element_type=_F32)
                          for r, c in zip(head_rows, head_cols)], axis=0)
    o2 = (jnp.dot(pc.astype(_BF16), vc, preferred_element_type=_F32) + on) / l
    for h in range(heads):
        first = h * per_head
        o = o2[first:first + steps, :] - lam * o2[first + steps:first + per_head, :]
        o_ref[:, head_cols[h]] = _sub_norm(o, sg_ref, lam_init).astype(o_ref.dtype)


def _sample_attn(q, k_new, v_new, cache_k, cache_v, lam, sg, *, lam_init, seqs):
    m, width = q.shape
    steps = m // seqs
    past = cache_k.shape[1]
    group_w = SAMPLE_HEADS_PER_STEP * HEAD_W
    groups = width // group_w
    kern = functools.partial(_sample_attn_kernel, lam_init=lam_init, steps=steps)
    new = pl.BlockSpec((steps, group_w), lambda b, g: (0, b * groups + g))
    cache = pl.BlockSpec((1, past, SAMPLE_HEADS_PER_STEP, HEAD_W), lambda b, g: (b, 0, g, 0))
    out = pl.pallas_call(
        kern,
        grid=(seqs, groups),
        in_specs=[new, new, new, cache, cache,
                  pl.BlockSpec((4, QK_DIM), lambda b, g: (0, 0)),
                  pl.BlockSpec((1, V_DIM), lambda b, g: (0, 0))],
        out_specs=new,
        out_shape=jax.ShapeDtypeStruct((steps, seqs * width), _BF16),
        compiler_params=_params(2),
        name="sample_attn",
    )(q.reshape(steps, seqs * width), k_new.reshape(steps, seqs * width),
      v_new.reshape(steps, seqs * width), cache_k, cache_v, lam, sg)
    return out.reshape(m, width)


def _oproj_ln_kernel(x_ref, a_ref, w_ref, g_ref, b_ref, o_ref, ob_ref):
    mix = jnp.dot(a_ref[...], w_ref[...], preferred_element_type=_F32)
    out = _layer_norm(ALPHA * x_ref[...] + mix, g_ref[...], b_ref[...])
    o_ref[...] = out
    ob_ref[...] = out.astype(_BF16)


def _oproj_ln(x, attn, w_o, g, b):
    m = x.shape[0]
    rows = min(ROW_TILE, m)
    assert m % rows == 0
    const = lambda i: (0, 0)
    return pl.pallas_call(
        _oproj_ln_kernel,
        grid=(m // rows,),
        in_specs=[pl.BlockSpec((rows, D_MODEL), lambda i: (i, 0)),
                  pl.BlockSpec((rows, N_HEADS * V_DIM), lambda i: (i, 0)),
                  pl.BlockSpec((N_HEADS * V_DIM, D_MODEL), const),
                  pl.BlockSpec((1, D_MODEL), const),
                  pl.BlockSpec((1, D_MODEL), const)],
        out_specs=[pl.BlockSpec((rows, D_MODEL), lambda i: (i, 0))] * 2,
        out_shape=[jax.ShapeDtypeStruct((m, D_MODEL), _F32), jax.ShapeDtypeStruct((m, D_MODEL), _BF16)],
        compiler_params=_params(1),
        name="oproj_ln",
    )(x, attn, w_o, g, b)


def _run_group(x, pool_hist, conv_hist, caches, pos0, seqs, p):
    layer_inputs, new_conv = [], []
    k_f32 = v_f32 = k_b = v_b = xb = None
    for l in range(DEPTH):
        g1, b1 = p['ln1_g'][l:l + 1], p['ln1_b'][l:l + 1]
        if l < N_A_LAYERS:
            layer_inputs.append(x)
            x, xb = _pool_ln(x, pool_hist[l], p['w_pool'][l], p['pool_scale'][l:l + 1], g1, b1,
                             seqs=seqs, pos0=pos0)
        else:
            j = l - N_A_LAYERS
            lam_init = 0.8 - 0.6 * math.exp(-0.3 * l)
            lam, sg = p['lam'][j], p['subln_g'][j:j + 1]
            (q,) = _proj(xb, p['w_q'][j], (_BF16,), scale=Q_SCALE)
            if caches is None:
                attn = _flash_prompt(q, k_b, v_b, lam, sg, lam_init=lam_init)
            else:
                attn = _sample_attn(q, k_b, v_b, caches[0], caches[1], lam, sg,
                                    lam_init=lam_init, seqs=seqs)
            x, xb = _oproj_ln(x, attn, p['w_o'][j], g1, b1)
        bf16_copy = N_A_LAYERS - 1 <= l < DEPTH - 1
        x, *xb, new_a, new_v = _ffn_ln(xb, x, conv_hist[l], p['w_up'][l], p['conv_w'][l], p['conv_b'][l:l + 1],
                                       p['w_down'][l], p['ln2_g'][l:l + 1], p['ln2_b'][l:l + 1],
                                       seqs=seqs, bf16_copy=bf16_copy)
        xb = xb[0] if xb else None
        new_conv.append(jnp.concatenate([_untile_cols(new_a), _untile_cols(new_v)], axis=1))
        if l == N_A_LAYERS - 1:
            k_f32, k_b = _proj(xb, p['w_k'], (_F32, _BF16))
            v_f32, v_b = _proj(xb, p['w_v'], (_F32, _BF16))
    return x, layer_inputs, new_conv, k_f32, v_f32


def _untile_cols(a):
    tiles, rows, w = a.shape
    return jnp.transpose(a, (1, 0, 2)).reshape(rows, tiles * w)


def _to_time_major(a):
    seqs, steps, w = a.shape
    return jnp.transpose(a, (1, 0, 2)).reshape(steps * seqs, w)


def _to_batch_major(a, seqs):
    m, w = a.shape
    return jnp.transpose(a.reshape(m // seqs, seqs, w), (1, 0, 2))


def kernel(x_prompt, x_sample, state_pool, state_ffn_conv, cache_k, cache_v, ln1_g, ln1_b, ln2_g, ln2_b,
           w_pool, pool_scale, w_up, conv_w, conv_b, w_down, w_kv, w_q, lam, subln_g, w_o):
    k_width = N_HEADS * HEAD_W
    w_kv_b = w_kv.astype(_BF16)
    p = dict(ln1_g=ln1_g, ln1_b=ln1_b, ln2_g=ln2_g, ln2_b=ln2_b, pool_scale=pool_scale,
             conv_w=conv_w, conv_b=conv_b, lam=lam, subln_g=subln_g,
             w_pool=w_pool.astype(_BF16), w_up=w_up.astype(_BF16), w_down=w_down.astype(_BF16),
             w_k=w_kv_b[:, :k_width], w_v=w_kv_b[:, k_width:],
             w_q=w_q.astype(_BF16), w_o=w_o.astype(_BF16))

    def finish(x, layer_inputs, new_conv, k_new, v_new, seqs):
        steps = x.shape[0] // seqs
        y = _to_batch_major(x, seqs)
        pool = jnp.stack([_to_batch_major(u[(steps - POOL_HIST) * seqs:], seqs) for u in layer_inputs])
        conv = jnp.stack([_to_batch_major(c, seqs) for c in new_conv])
        k_new = _to_batch_major(k_new, seqs).reshape(seqs, steps, N_HEADS, HEAD_W)
        v_new = _to_batch_major(v_new, seqs).reshape(seqs, steps, N_HEADS, V_DIM)
        return y, pool, conv, k_new, v_new

    b, t, _ = x_prompt.shape
    assert b == 1 and t >= POOL_HIST
    zeros_pool = jnp.zeros(((POOL_HIST + 1) * b, D_MODEL), _F32)
    zeros_conv = jnp.zeros(((CONV_W - 1) * b, 2 * D_FF), _F32)
    out_p = _run_group(_to_time_major(x_prompt), [zeros_pool] * N_A_LAYERS, [zeros_conv] * DEPTH,
                       None, 0, b, p)
    y_p, pool_p, conv_p, k_p, v_p = finish(*out_p, b)

    sb, steps, _ = x_sample.shape
    past = cache_k.shape[1]
    assert steps >= POOL_HIST
    pad = jnp.zeros((sb, D_MODEL), _F32)
    pool_hist = [jnp.concatenate([pad, _to_time_major(state_pool[l])], axis=0) for l in range(N_A_LAYERS)]
    conv_hist = [_to_time_major(state_ffn_conv[l]) for l in range(DEPTH)]
    out_s = _run_group(_to_time_major(x_sample), pool_hist, conv_hist, (cache_k, cache_v), past, sb, p)
    y_s, pool_s, conv_s, k_s, v_s = finish(*out_s, sb)

    return (y_p, y_s, pool_p, pool_s, conv_p, conv_s, k_p, v_p, k_s, v_s)
```

```python
import functools
import math

import jax
import jax.numpy as jnp
from jax import lax
from jax.experimental import pallas as pl
from jax.experimental.pallas import tpu as pltpu

D_MODEL = 2048
DEPTH = 4
N_A_LAYERS = DEPTH // 2
CHUNK = 64
POOL_WINDOWS = (2, 4, 8, 16)
POOL_GROUP = D_MODEL // len(POOL_WINDOWS)
POOL_HIST = max(POOL_WINDOWS) - 1
N_HEADS = 16
QK_DIM = 64
V_DIM = 2 * QK_DIM
HEAD_W = 2 * QK_DIM
D_FF = 5632
CONV_W = 3
LN_EPS = 1e-5
ALPHA = (2 * DEPTH) ** 0.25
ATTN_SCALE = QK_DIM ** -0.5
Q_SCALE = ATTN_SCALE * math.log2(math.e)
NEG_INF = -1e30

V7X_VMEM_BYTES = 64 * 1024 * 1024
VMEM_LIMIT_BYTES = V7X_VMEM_BYTES - 8 * 1024 * 1024
SUBLANES = 8

ROW_TILE = 512
FF_TILE = 512
N_FF_TILES = D_FF // FF_TILE
MM_COL_TILE = 1024
Q_TILE = 256
K_TILE = 512
STEP_UNROLL = 2
SAMPLE_HEADS_PER_STEP = 8

_F32 = jnp.float32
_BF16 = jnp.bfloat16


def _params(n_axes):
    return pltpu.CompilerParams(dimension_semantics=("arbitrary",) * n_axes,
                                vmem_limit_bytes=VMEM_LIMIT_BYTES)


def _layer_norm(z, g, b):
    mu = jnp.mean(z, axis=-1, keepdims=True)
    zc = z - mu
    var = jnp.mean(zc * zc, axis=-1, keepdims=True)
    return zc * lax.rsqrt(var + LN_EPS) * g + b


def _gelu_tanh(x):
    cdf = 0.5 * (1.0 + jnp.tanh(math.sqrt(2.0 / math.pi) * (x + 0.044715 * (x * x * x))))
    return x * cdf


def _pool_ln_kernel(x_ref, halo_ref, hist_ref, w_ref, scale_ref, g_ref, b_ref, o_ref, ob_ref,
                    ext_ref, z_ref, *, seqs, rows, pos0):
    i = pl.program_id(0)
    halo_rows = (POOL_HIST + 1) * seqs

    @pl.when(i == 0)
    def _():
        ext_ref[0:halo_rows, :] = hist_ref[...]

    @pl.when(i > 0)
    def _():
        ext_ref[0:halo_rows, :] = halo_ref[...]

    ext_ref[halo_rows:halo_rows + rows, :] = x_ref[...]

    row = lax.broadcasted_iota(jnp.int32, (rows, 1), 0) + i * rows
    step = lax.shift_right_logical(row, int(math.log2(seqs)))
    pos_plus_1 = (step + (pos0 + 1)).astype(_F32)

    for g, w in enumerate(POOL_WINDOWS):
        cols = slice(g * POOL_GROUP, (g + 1) * POOL_GROUP)
        cur = x_ref[:, cols]
        win = cur
        for k in range(1, w):
            start = halo_rows - k * seqs
            win = win + ext_ref[start:start + rows, cols]
        cnt = jnp.minimum(float(w), pos_plus_1)
        d = win / cnt - cur
        mix = jnp.dot(d.astype(_BF16), w_ref[g], preferred_element_type=_F32)
        z_ref[:, cols] = ALPHA * cur + mix * scale_ref[:, cols]

    out = _layer_norm(z_ref[...], g_ref[...], b_ref[...])
    o_ref[...] = out
    ob_ref[...] = out.astype(_BF16)


def _pool_ln(x, hist, w_pool, scale, g, b, *, seqs, pos0):
    m = x.shape[0]
    rows = min(ROW_TILE, m)
    halo_rows = (POOL_HIST + 1) * seqs
    assert m % rows == 0 and rows % halo_rows == 0 and seqs & (seqs - 1) == 0
    halo_blocks_per_tile = rows // halo_rows
    kern = functools.partial(_pool_ln_kernel, seqs=seqs, rows=rows, pos0=pos0)
    return pl.pallas_call(
        kern,
        grid=(m // rows,),
        in_specs=[
            pl.BlockSpec((rows, D_MODEL), lambda i: (i, 0)),
            pl.BlockSpec((halo_rows, D_MODEL),
                         lambda i: (jnp.maximum(i * halo_blocks_per_tile - 1, 0), 0)),
            pl.BlockSpec((halo_rows, D_MODEL), lambda i: (0, 0)),
            pl.BlockSpec((len(POOL_WINDOWS), POOL_GROUP, POOL_GROUP), lambda i: (0, 0, 0)),
            pl.BlockSpec((1, D_MODEL), lambda i: (0, 0)),
            pl.BlockSpec((1, D_MODEL), lambda i: (0, 0)),
            pl.BlockSpec((1, D_MODEL), lambda i: (0, 0)),
        ],
        out_specs=[pl.BlockSpec((rows, D_MODEL), lambda i: (i, 0))] * 2,
        out_shape=[jax.ShapeDtypeStruct((m, D_MODEL), _F32), jax.ShapeDtypeStruct((m, D_MODEL), _BF16)],
        scratch_shapes=[pltpu.VMEM((halo_rows + rows, D_MODEL), _F32),
                        pltpu.VMEM((rows, D_MODEL), _F32)],
        compiler_params=_params(1),
        name="pool_ln",
    )(x, x, hist, w_pool, scale, g, b)


def _ffn_ln_kernel(xb_ref, x_ref, hist_a_ref, hist_v_ref, wa_ref, wv_ref, cwa_ref, cwv_ref, cba_ref, cbv_ref,
                   wd_ref, g_ref, b_ref, *refs, seqs, rows, n_items, bf16_copy):
    if bf16_copy:
        o_ref, ob_ref, new_a_ref, new_v_ref, ext_a_ref, ext_v_ref = refs
    else:
        (o_ref, new_a_ref, new_v_ref, ext_a_ref, ext_v_ref), ob_ref = refs, None
    s = pl.program_id(0)
    halo = (CONV_W - 1) * seqs
    off = -(-halo // SUBLANES) * SUBLANES
    a_item = jnp.minimum(s, n_items - 1)
    i_a = lax.div(a_item, N_FF_TILES)
    j_a = a_item - i_a * N_FF_TILES
    j_b = lax.rem(jnp.maximum(s - 1, 0), N_FF_TILES)
    parity = s & 1

    @pl.when(s == 0)
    def _():
        ext_a_ref[1] = jnp.zeros(ext_a_ref.shape[1:], _F32)
        ext_v_ref[1] = jnp.zeros(ext_v_ref.shape[1:], _F32)

    @pl.when(i_a == 0)
    def _():
        ext_a_ref[parity, off - halo:off, :] = hist_a_ref[...]
        ext_v_ref[parity, off - halo:off, :] = hist_v_ref[...]

    @pl.when(i_a > 0)
    def _():
        ext_a_ref[parity, off - halo:off, :] = new_a_ref[j_a]
        ext_v_ref[parity, off - halo:off, :] = new_v_ref[j_a]

    @pl.when(j_b == 0)
    def _():
        o_ref[...] = jnp.zeros_like(o_ref)

    def both_halves(slot):
        xb = xb_ref[...]
        ha = jnp.dot(xb, wa_ref[...], preferred_element_type=_F32)
        hv = jnp.dot(xb, wv_ref[...], preferred_element_type=_F32)
        ext_a_ref[slot, off:off + rows, :] = ha
        ext_v_ref[slot, off:off + rows, :] = hv
        new_a_ref[j_a] = ha[rows - halo:, :]
        new_v_ref[j_a] = hv[rows - halo:, :]

        def conv(ext_ref, cw_ref, cb_ref):
            cw = cw_ref[...]
            c = cb_ref[...]
            for tap in range(CONV_W):
                start = off - (CONV_W - 1 - tap) * seqs
                c = c + cw[tap:tap + 1, :] * ext_ref[1 - slot, start:start + rows, :]
            return c

        ca = conv(ext_a_ref, cwa_ref, cba_ref)
        cv = conv(ext_v_ref, cwv_ref, cbv_ref)
        act = (_gelu_tanh(ca) * cv).astype(_BF16)
        o_ref[...] += jnp.dot(act, wd_ref[...], preferred_element_type=_F32)

    for slot in (0, 1):
        pl.when(parity == slot)(functools.partial(both_halves, slot))

    @pl.when(jnp.logical_and(s > 0, j_b == N_FF_TILES - 1))
    def _():
        out = _layer_norm(ALPHA * x_ref[...] + o_ref[...], g_ref[...], b_ref[...])
        o_ref[...] = out
        if ob_ref is not None:
            ob_ref[...] = out.astype(_BF16)


def _ffn_ln(xb, x, hist, w_up, conv_w, conv_b, w_down, g, b, *, seqs, bf16_copy):
    m = x.shape[0]
    rows = min(ROW_TILE, m)
    halo = (CONV_W - 1) * seqs
    off = -(-halo // SUBLANES) * SUBLANES
    assert m % rows == 0 and rows >= halo
    n_items = (m // rows) * N_FF_TILES
    kern = functools.partial(_ffn_ln_kernel, seqs=seqs, rows=rows, n_items=n_items, bf16_copy=bf16_copy)

    def up_item(s):
        item = jnp.minimum(s, n_items - 1)
        return lax.div(item, N_FF_TILES), lax.rem(item, N_FF_TILES)

    def down_item(s):
        item = jnp.maximum(s - 1, 0)
        return lax.div(item, N_FF_TILES), lax.rem(item, N_FF_TILES)

    up_a = lambda s: (0, up_item(s)[1])
    up_v = lambda s: (0, up_item(s)[1] + N_FF_TILES)
    down_a = lambda s: (0, down_item(s)[1])
    down_v = lambda s: (0, down_item(s)[1] + N_FF_TILES)
    const = lambda s: (0, 0)
    down_rows = pl.BlockSpec((rows, D_MODEL), lambda s: (down_item(s)[0], 0))
    new_hist = pl.BlockSpec((N_FF_TILES, halo, FF_TILE), lambda s: (0, 0, 0))
    row_out = [jax.ShapeDtypeStruct((m, D_MODEL), _F32)]
    if bf16_copy:
        row_out.append(jax.ShapeDtypeStruct((m, D_MODEL), _BF16))
    return pl.pallas_call(
        kern,
        grid=(n_items + 1,),
        in_specs=[
            pl.BlockSpec((rows, D_MODEL), lambda s: (up_item(s)[0], 0)),
            down_rows,
            pl.BlockSpec((halo, FF_TILE), up_a),
            pl.BlockSpec((halo, FF_TILE), up_v),
            pl.BlockSpec((D_MODEL, FF_TILE), up_a),
            pl.BlockSpec((D_MODEL, FF_TILE), up_v),
            pl.BlockSpec((CONV_W, FF_TILE), down_a),
            pl.BlockSpec((CONV_W, FF_TILE), down_v),
            pl.BlockSpec((1, FF_TILE), down_a),
            pl.BlockSpec((1, FF_TILE), down_v),
            pl.BlockSpec((FF_TILE, D_MODEL), lambda s: (down_item(s)[1], 0)),
            pl.BlockSpec((1, D_MODEL), const),
            pl.BlockSpec((1, D_MODEL), const),
        ],
        out_specs=[down_rows] * len(row_out) + [new_hist, new_hist],
        out_shape=row_out + [jax.ShapeDtypeStruct((N_FF_TILES, halo, FF_TILE), _F32)] * 2,
        scratch_shapes=[
            pltpu.VMEM((2, off + rows, FF_TILE), _F32),
            pltpu.VMEM((2, off + rows, FF_TILE), _F32),
        ],
        compiler_params=_params(1),
        name="ffn_ln",
    )(xb, x, hist, hist, w_up, w_up, conv_w, conv_w, conv_b, conv_b, w_down, g, b)


def _proj_kernel(x_ref, w_ref, *out_refs, scale):
    r = jnp.dot(x_ref[...], w_ref[...], preferred_element_type=_F32)
    for o_ref in out_refs:
        if o_ref.dtype == _F32:
            for h in range(o_ref.shape[1]):
                o_ref[:, h, :] = r[:, h * HEAD_W:(h + 1) * HEAD_W]
        else:
            o_ref[...] = (r * scale).astype(o_ref.dtype)


def _proj(x, w, out_dtypes, *, scale=1.0):
    m, k = x.shape
    n = w.shape[1]
    rows = min(2 * ROW_TILE, m)
    cols = min(MM_COL_TILE, n)
    assert m % rows == 0 and n % cols == 0 and cols % HEAD_W == 0
    flat = pl.BlockSpec((rows, cols), lambda i, j: (i, j)), (m, n)
    heads = pl.BlockSpec((rows, cols // HEAD_W, HEAD_W), lambda i, j: (i, j, 0)), (m, n // HEAD_W, HEAD_W)
    outs = [heads if dt == _F32 else flat for dt in out_dtypes]
    return pl.pallas_call(
        functools.partial(_proj_kernel, scale=scale),
        grid=(m // rows, n // cols),
        in_specs=[pl.BlockSpec((rows, k), lambda i, j: (i, 0)),
                  pl.BlockSpec((k, cols), lambda i, j: (0, j))],
        out_specs=[spec for spec, _ in outs],
        out_shape=[jax.ShapeDtypeStruct(shape, dt) for (_, shape), dt in zip(outs, out_dtypes)],
        compiler_params=_params(2),
        name="proj",
    )(x, w)


def _diff_lambda(lam_ref, lam_init):
    lam = lam_ref[...]
    e1 = jnp.exp(jnp.sum(lam[0:1, :] * lam[1:2, :], axis=-1, keepdims=True))
    e2 = jnp.exp(jnp.sum(lam[2:3, :] * lam[3:4, :], axis=-1, keepdims=True))
    return e1 - e2 + lam_init


def _split_parts(q):
    first = lax.broadcasted_iota(jnp.int32, (1, HEAD_W), 1) < QK_DIM
    zero = jnp.zeros_like(q)
    return jnp.concatenate([jnp.where(first, q, zero), jnp.where(first, zero, q)], axis=0)


def _sub_norm(o, sg_ref, lam_init):
    return o * lax.rsqrt(jnp.mean(o * o, axis=-1, keepdims=True) + LN_EPS) * sg_ref[...] * (1.0 - lam_init)


def _score(qz, kb):
    return lax.dot_general(qz, kb, (((1,), (1,)), ((), ())), preferred_element_type=_F32)


def _flash_kernel(qt_ref, k_ref, vt_ref, lam_ref, sg_ref, o_ref,
                  qz_ref, s_ref, p_ref, a_ref, m_ref, l_ref, acc_ref, *, lam_init, n_q_blocks):
    lam = _diff_lambda(lam_ref, lam_init)
    n_cols = 2 * Q_TILE
    first_part = lax.broadcasted_iota(jnp.int32, (HEAD_W, 1), 0) < QK_DIM
    chunk_shift = int(math.log2(CHUNK))

    def scores(b, slot):
        k0 = pl.multiple_of(b * K_TILE, K_TILE)
        s_ref[slot] = jnp.dot(k_ref[pl.ds(k0, K_TILE), :], qz_ref[...], preferred_element_type=_F32)

    def softmax(slot, visible):
        s = s_ref[slot]
        if visible is not None:
            s = jnp.where(visible, s, NEG_INF)
        m_prev = m_ref[...]
        m_new = jnp.maximum(m_prev, jnp.max(s, axis=0, keepdims=True))
        a = jnp.exp2(m_prev - m_new)
        p = jnp.exp2(s - m_new)
        l_ref[...] = a * l_ref[...] + jnp.sum(p, axis=0, keepdims=True)
        m_ref[...] = m_new
        a_ref[slot] = a
        p_ref[slot] = p.astype(_BF16)

    def values(b, slot):
        vb = vt_ref[jnp.maximum(b, 0)]
        acc_ref[...] = a_ref[slot] * acc_ref[...] + jnp.dot(vb, p_ref[slot], preferred_element_type=_F32)

    def step(t, slot):
        values(t - 1, 1 - slot)
        softmax(slot, None)
        scores(t + 1, 1 - slot)

    def q_block(qi, carry):
        qt = qt_ref[qi]
        zero = jnp.zeros_like(qt)
        qz_ref[:, 0:Q_TILE] = jnp.where(first_part, qt, zero)
        qz_ref[:, Q_TILE:n_cols] = jnp.where(first_part, zero, qt)
        m_ref[...] = jnp.full(m_ref.shape, NEG_INF, _F32)
        l_ref[...] = jnp.zeros_like(l_ref)
        acc_ref[...] = jnp.zeros_like(acc_ref)
        p_ref[1] = jnp.zeros(p_ref.shape[1:], _BF16)
        a_ref[1] = jnp.ones(a_ref.shape[1:], _F32)

        n_full = lax.div(qi, K_TILE // Q_TILE)
        scores(0, 0)

        def unrolled(u, c):
            for i in range(STEP_UNROLL):
                step(STEP_UNROLL * u + i, i & 1)
            return c

        def single(t, c):
            step(t, t & 1)
            return c

        n_unrolled = lax.div(n_full, STEP_UNROLL)
        lax.fori_loop(0, n_unrolled, unrolled, 0)
        lax.fori_loop(n_unrolled * STEP_UNROLL, n_full, single, 0)

        last = n_full & 1
        values(n_full - 1, 1 - last)
        key = lax.broadcasted_iota(jnp.int32, (K_TILE, 1), 0) + n_full * K_TILE
        col = lax.broadcasted_iota(jnp.int32, (1, n_cols), 1)
        query = jnp.where(col >= Q_TILE, col - Q_TILE, col) + qi * Q_TILE
        visible = (lax.shift_right_logical(key, chunk_shift)
                   <= lax.shift_right_logical(query, chunk_shift))
        softmax(last, visible)
        values(n_full, last)

        o2 = acc_ref[...] / l_ref[...]
        o = o2[:, 0:Q_TILE] - lam * o2[:, Q_TILE:n_cols]
        norm = lax.rsqrt(jnp.mean(o * o, axis=0, keepdims=True) + LN_EPS)
        o_ref[qi] = (o * norm * sg_ref[...] * (1.0 - lam_init)).astype(o_ref.dtype)
        return carry

    lax.fori_loop(0, n_q_blocks, q_block, 0)


def _flash_prompt(q, k, v, lam, sg, *, lam_init):
    t = q.shape[0]
    assert t % K_TILE == 0 and K_TILE % Q_TILE == 0 and Q_TILE % CHUNK == 0 and STEP_UNROLL % 2 == 0
    nq, nk = t // Q_TILE, t // K_TILE
    qt = jnp.transpose(q.reshape(nq, Q_TILE, N_HEADS, HEAD_W), (2, 0, 3, 1))
    vt = jnp.transpose(v.reshape(nk, K_TILE, N_HEADS, V_DIM), (2, 0, 3, 1))
    kern = functools.partial(_flash_kernel, lam_init=lam_init, n_q_blocks=nq)
    out = pl.pallas_call(
        kern,
        grid=(N_HEADS,),
        in_specs=[pl.BlockSpec((None, nq, HEAD_W, Q_TILE), lambda h: (h, 0, 0, 0)),
                  pl.BlockSpec((t, HEAD_W), lambda h: (0, h)),
                  pl.BlockSpec((None, nk, V_DIM, K_TILE), lambda h: (h, 0, 0, 0)),
                  pl.BlockSpec((4, QK_DIM), lambda h: (0, 0)),
                  pl.BlockSpec((V_DIM, 1), lambda h: (0, 0))],
        out_specs=pl.BlockSpec((None, nq, V_DIM, Q_TILE), lambda h: (h, 0, 0, 0)),
        out_shape=jax.ShapeDtypeStruct((N_HEADS, nq, V_DIM, Q_TILE), _BF16),
        scratch_shapes=[pltpu.VMEM((HEAD_W, 2 * Q_TILE), _BF16),
                        pltpu.VMEM((2, K_TILE, 2 * Q_TILE), _F32),
                        pltpu.VMEM((2, K_TILE, 2 * Q_TILE), _BF16),
                        pltpu.VMEM((2, 1, 2 * Q_TILE), _F32),
                        pltpu.VMEM((1, 2 * Q_TILE), _F32),
                        pltpu.VMEM((1, 2 * Q_TILE), _F32),
                        pltpu.VMEM((V_DIM, 2 * Q_TILE), _F32)],
        compiler_params=_params(1),
        name="flash_prompt",
    )(qt, k, vt, lam, sg.reshape(V_DIM, 1))
    return jnp.transpose(out, (1, 3, 0, 2)).reshape(t, N_HEADS * V_DIM)


def _sample_attn_kernel(q_ref, kn_ref, vn_ref, kc_ref, vc_ref, lam_ref, sg_ref, o_ref, *, lam_init, steps):
    heads = SAMPLE_HEADS_PER_STEP
    per_head = 2 * steps
    past = kc_ref.shape[1]
    lam = _diff_lambda(lam_ref, lam_init)
    head_cols = [slice(h * HEAD_W, (h + 1) * HEAD_W) for h in range(heads)]
    head_rows = [slice(h * per_head, (h + 1) * per_head) for h in range(heads)]

    qz = jnp.concatenate([_split_parts(q_ref[:, c]) for c in head_cols], axis=0)
    kc = kc_ref[0].reshape(past * heads, HEAD_W).astype(_BF16)
    vc = vc_ref[0].reshape(past * heads, V_DIM).astype(_BF16)
    sc = _score(qz, kc)
    row_head = lax.shift_right_logical(lax.broadcasted_iota(jnp.int32, (heads * per_head, 1), 0),
                                       int(math.log2(per_head)))
    col_head = lax.broadcasted_iota(jnp.int32, (1, past * heads), 1) & (heads - 1)
    sc = jnp.where(row_head == col_head, sc, NEG_INF)
    sn = jnp.concatenate([_score(qz[r, :], kn_ref[:, c]) for r, c in zip(head_rows, head_cols)], axis=0)
    m = jnp.maximum(jnp.max(sc, axis=1, keepdims=True), jnp.max(sn, axis=1, keepdims=True))
    pc = jnp.exp2(sc - m)
    pn = jnp.exp2(sn - m)
    l = jnp.sum(pc, axis=1, keepdims=True) + jnp.sum(pn, axis=1, keepdims=True)
    pn = pn.astype(_BF16)
    on = jnp.concatenate([jnp.dot(pn[r, :], vn_ref[:, c], preferred_element_type=_F32)
                          for r, c in zip(head_rows, head_cols)], axis=0)
    o2 = (jnp.dot(pc.astype(_BF16), vc, preferred_element_type=_F32) + on) / l
    for h in range(heads):
        first = h * per_head
        o = o2[first:first + steps, :] - lam * o2[first + steps:first + per_head, :]
        o_ref[:, head_cols[h]] = _sub_norm(o, sg_ref, lam_init).astype(o_ref.dtype)


def _sample_attn(q, k_new, v_new, cache_k, cache_v, lam, sg, *, lam_init, seqs):
    m, width = q.shape
    steps = m // seqs
    past = cache_k.shape[1]
    group_w = SAMPLE_HEADS_PER_STEP * HEAD_W
    groups = width // group_w
    kern = functools.partial(_sample_attn_kernel, lam_init=lam_init, steps=steps)
    new = pl.BlockSpec((steps, group_w), lambda b, g: (0, b * groups + g))
    cache = pl.BlockSpec((1, past, SAMPLE_HEADS_PER_STEP, HEAD_W), lambda b, g: (b, 0, g, 0))
    out = pl.pallas_call(
        kern,
        grid=(seqs, groups),
        in_specs=[new, new, new, cache, cache,
                  pl.BlockSpec((4, QK_DIM), lambda b, g: (0, 0)),
                  pl.BlockSpec((1, V_DIM), lambda b, g: (0, 0))],
        out_specs=new,
        out_shape=jax.ShapeDtypeStruct((steps, seqs * width), _BF16),
        compiler_params=_params(2),
        name="sample_attn",
    )(q.reshape(steps, seqs * width), k_new.reshape(steps, seqs * width),
      v_new.reshape(steps, seqs * width), cache_k, cache_v, lam, sg)
    return out.reshape(m, width)


def _oproj_ln_kernel(x_ref, a_ref, w_ref, g_ref, b_ref, o_ref, ob_ref):
    mix = jnp.dot(a_ref[...], w_ref[...], preferred_element_type=_F32)
    out = _layer_norm(ALPHA * x_ref[...] + mix, g_ref[...], b_ref[...])
    o_ref[...] = out
    ob_ref[...] = out.astype(_BF16)


def _oproj_ln(x, attn, w_o, g, b):
    m = x.shape[0]
    rows = min(ROW_TILE, m)
    assert m % rows == 0
    const = lambda i: (0, 0)
    return pl.pallas_call(
        _oproj_ln_kernel,
        grid=(m // rows,),
        in_specs=[pl.BlockSpec((rows, D_MODEL), lambda i: (i, 0)),
                  pl.BlockSpec((rows, N_HEADS * V_DIM), lambda i: (i, 0)),
                  pl.BlockSpec((N_HEADS * V_DIM, D_MODEL), const),
                  pl.BlockSpec((1, D_MODEL), const),
                  pl.BlockSpec((1, D_MODEL), const)],
        out_specs=[pl.BlockSpec((rows, D_MODEL), lambda i: (i, 0))] * 2,
        out_shape=[jax.ShapeDtypeStruct((m, D_MODEL), _F32), jax.ShapeDtypeStruct((m, D_MODEL), _BF16)],
        compiler_params=_params(1),
        name="oproj_ln",
    )(x, attn, w_o, g, b)


def _run_group(x, pool_hist, conv_hist, caches, pos0, seqs, p):
    layer_inputs, new_conv = [], []
    k_f32 = v_f32 = k_b = v_b = xb = None
    for l in range(DEPTH):
        g1, b1 = p['ln1_g'][l:l + 1], p['ln1_b'][l:l + 1]
        if l < N_A_LAYERS:
            layer_inputs.append(x)
            x, xb = _pool_ln(x, pool_hist[l], p['w_pool'][l], p['pool_scale'][l:l + 1], g1, b1,
                             seqs=seqs, pos0=pos0)
        else:
            j = l - N_A_LAYERS
            lam_init = 0.8 - 0.6 * math.exp(-0.3 * l)
            lam, sg = p['lam'][j], p['subln_g'][j:j + 1]
            (q,) = _proj(xb, p['w_q'][j], (_BF16,), scale=Q_SCALE)
            if caches is None:
                attn = _flash_prompt(q, k_b, v_b, lam, sg, lam_init=lam_init)
            else:
                attn = _sample_attn(q, k_b, v_b, caches[0], caches[1], lam, sg,
                                    lam_init=lam_init, seqs=seqs)
            x, xb = _oproj_ln(x, attn, p['w_o'][j], g1, b1)
        bf16_copy = N_A_LAYERS - 1 <= l < DEPTH - 1
        x, *xb, new_a, new_v = _ffn_ln(xb, x, conv_hist[l], p['w_up'][l], p['conv_w'][l], p['conv_b'][l:l + 1],
                                       p['w_down'][l], p['ln2_g'][l:l + 1], p['ln2_b'][l:l + 1],
                                       seqs=seqs, bf16_copy=bf16_copy)
        xb = xb[0] if xb else None
        new_conv.append(jnp.concatenate([_untile_cols(new_a), _untile_cols(new_v)], axis=1))
        if l == N_A_LAYERS - 1:
            k_f32, k_b = _proj(xb, p['w_k'], (_F32, _BF16))
            v_f32, v_b = _proj(xb, p['w_v'], (_F32, _BF16))
    return x, layer_inputs, new_conv, k_f32, v_f32


def _untile_cols(a):
    tiles, rows, w = a.shape
    return jnp.transpose(a, (1, 0, 2)).reshape(rows, tiles * w)


def _to_time_major(a):
    seqs, steps, w = a.shape
    return jnp.transpose(a, (1, 0, 2)).reshape(steps * seqs, w)


def _to_batch_major(a, seqs):
    return jnp.swapaxes(a.reshape(a.shape[0] // seqs, seqs, *a.shape[1:]), 0, 1)


def kernel(x_prompt, x_sample, state_pool, state_ffn_conv, cache_k, cache_v, ln1_g, ln1_b, ln2_g, ln2_b,
           w_pool, pool_scale, w_up, conv_w, conv_b, w_down, w_kv, w_q, lam, subln_g, w_o):
    k_width = N_HEADS * HEAD_W
    w_kv_b = w_kv.astype(_BF16)
    p = dict(ln1_g=ln1_g, ln1_b=ln1_b, ln2_g=ln2_g, ln2_b=ln2_b, pool_scale=pool_scale,
             conv_w=conv_w, conv_b=conv_b, lam=lam, subln_g=subln_g,
             w_pool=w_pool.astype(_BF16), w_up=w_up.astype(_BF16), w_down=w_down.astype(_BF16),
             w_k=w_kv_b[:, :k_width], w_v=w_kv_b[:, k_width:],
             w_q=w_q.astype(_BF16), w_o=w_o.astype(_BF16))

    def finish(x, layer_inputs, new_conv, k_new, v_new, seqs):
        steps = x.shape[0] // seqs
        y = _to_batch_major(x, seqs)
        pool = jnp.stack([_to_batch_major(u[(steps - POOL_HIST) * seqs:], seqs) for u in layer_inputs])
        conv = jnp.stack([_to_batch_major(c, seqs) for c in new_conv])
        return y, pool, conv, _to_batch_major(k_new, seqs), _to_batch_major(v_new, seqs)

    b, t, _ = x_prompt.shape
    assert b == 1 and t >= POOL_HIST
    zeros_pool = jnp.zeros(((POOL_HIST + 1) * b, D_MODEL), _F32)
    zeros_conv = jnp.zeros(((CONV_W - 1) * b, 2 * D_FF), _F32)
    out_p = _run_group(_to_time_major(x_prompt), [zeros_pool] * N_A_LAYERS, [zeros_conv] * DEPTH,
                       None, 0, b, p)
    y_p, pool_p, conv_p, k_p, v_p = finish(*out_p, b)

    sb, steps, _ = x_sample.shape
    past = cache_k.shape[1]
    assert steps >= POOL_HIST
    pad = jnp.zeros((sb, D_MODEL), _F32)
    pool_hist = [jnp.concatenate([pad, _to_time_major(state_pool[l])], axis=0) for l in range(N_A_LAYERS)]
    conv_hist = [_to_time_major(state_ffn_conv[l]) for l in range(DEPTH)]
    out_s = _run_group(_to_time_major(x_sample), pool_hist, conv_hist, (cache_k, cache_v), past, sb, p)
    y_s, pool_s, conv_s, k_s, v_s = finish(*out_s, sb)

    return (y_p, y_s, pool_p, pool_s, conv_p, conv_s, k_p, v_p, k_s, v_s)
```

```python
import functools
import math

import jax
import jax.numpy as jnp
from jax import lax
from jax.experimental import pallas as pl
from jax.experimental.pallas import tpu as pltpu

D_MODEL = 2048
DEPTH = 4
N_A_LAYERS = DEPTH // 2
CHUNK = 64
POOL_WINDOWS = (2, 4, 8, 16)
POOL_GROUP = D_MODEL // len(POOL_WINDOWS)
POOL_HIST = max(POOL_WINDOWS) - 1
N_HEADS = 16
QK_DIM = 64
V_DIM = 2 * QK_DIM
HEAD_W = 2 * QK_DIM
D_FF = 5632
CONV_W = 3
LN_EPS = 1e-5
ALPHA = (2 * DEPTH) ** 0.25
ATTN_SCALE = QK_DIM ** -0.5
Q_SCALE = ATTN_SCALE * math.log2(math.e)
NEG_INF = -1e30

V7X_VMEM_BYTES = 64 * 1024 * 1024
VMEM_LIMIT_BYTES = V7X_VMEM_BYTES - 8 * 1024 * 1024
SUBLANES = 8

ROW_TILE = 512
FF_TILE = 512
N_FF_TILES = D_FF // FF_TILE
MM_COL_TILE = 1024
Q_TILE = 256
K_TILE = 512
STEP_UNROLL = 4
SAMPLE_HEADS_PER_STEP = 8

_F32 = jnp.float32
_BF16 = jnp.bfloat16


def _params(n_axes):
    return pltpu.CompilerParams(dimension_semantics=("arbitrary",) * n_axes,
                                vmem_limit_bytes=VMEM_LIMIT_BYTES)


def _layer_norm(z, g, b):
    mu = jnp.mean(z, axis=-1, keepdims=True)
    zc = z - mu
    var = jnp.mean(zc * zc, axis=-1, keepdims=True)
    return zc * lax.rsqrt(var + LN_EPS) * g + b


def _gelu_tanh(x):
    cdf = 0.5 * (1.0 + jnp.tanh(math.sqrt(2.0 / math.pi) * (x + 0.044715 * (x * x * x))))
    return x * cdf


def _pool_ln_kernel(x_ref, halo_ref, hist_ref, w_ref, scale_ref, g_ref, b_ref, o_ref, ob_ref,
                    ext_ref, z_ref, *, seqs, rows, pos0):
    i = pl.program_id(0)
    halo_rows = (POOL_HIST + 1) * seqs

    @pl.when(i == 0)
    def _():
        ext_ref[0:halo_rows, :] = hist_ref[...]

    @pl.when(i > 0)
    def _():
        ext_ref[0:halo_rows, :] = halo_ref[...]

    ext_ref[halo_rows:halo_rows + rows, :] = x_ref[...]

    row = lax.broadcasted_iota(jnp.int32, (rows, 1), 0) + i * rows
    step = lax.shift_right_logical(row, int(math.log2(seqs)))
    pos_plus_1 = (step + (pos0 + 1)).astype(_F32)

    for g, w in enumerate(POOL_WINDOWS):
        cols = slice(g * POOL_GROUP, (g + 1) * POOL_GROUP)
        cur = x_ref[:, cols]
        win = cur
        for k in range(1, w):
            start = halo_rows - k * seqs
            win = win + ext_ref[start:start + rows, cols]
        cnt = jnp.minimum(float(w), pos_plus_1)
        d = win / cnt - cur
        mix = jnp.dot(d.astype(_BF16), w_ref[g], preferred_element_type=_F32)
        z_ref[:, cols] = ALPHA * cur + mix * scale_ref[:, cols]

    out = _layer_norm(z_ref[...], g_ref[...], b_ref[...])
    o_ref[...] = out
    ob_ref[...] = out.astype(_BF16)


def _pool_ln(x, hist, w_pool, scale, g, b, *, layer, seqs, pos0):
    m = x.shape[0]
    rows = min(ROW_TILE, m)
    halo_rows = (POOL_HIST + 1) * seqs
    assert m % rows == 0 and rows % halo_rows == 0 and seqs & (seqs - 1) == 0
    halo_blocks_per_tile = rows // halo_rows
    kern = functools.partial(_pool_ln_kernel, seqs=seqs, rows=rows, pos0=pos0)
    return pl.pallas_call(
        kern,
        grid=(m // rows,),
        in_specs=[
            pl.BlockSpec((rows, D_MODEL), lambda i: (i, 0)),
            pl.BlockSpec((halo_rows, D_MODEL),
                         lambda i: (jnp.maximum(i * halo_blocks_per_tile - 1, 0), 0)),
            pl.BlockSpec((halo_rows, D_MODEL), lambda i: (0, 0)),
            pl.BlockSpec((None, len(POOL_WINDOWS), POOL_GROUP, POOL_GROUP), lambda i: (layer, 0, 0, 0)),
            pl.BlockSpec((1, D_MODEL), lambda i: (0, 0)),
            pl.BlockSpec((1, D_MODEL), lambda i: (0, 0)),
            pl.BlockSpec((1, D_MODEL), lambda i: (0, 0)),
        ],
        out_specs=[pl.BlockSpec((rows, D_MODEL), lambda i: (i, 0))] * 2,
        out_shape=[jax.ShapeDtypeStruct((m, D_MODEL), _F32), jax.ShapeDtypeStruct((m, D_MODEL), _BF16)],
        scratch_shapes=[pltpu.VMEM((halo_rows + rows, D_MODEL), _F32),
                        pltpu.VMEM((rows, D_MODEL), _F32)],
        compiler_params=_params(1),
        name="pool_ln",
    )(x, x, hist, w_pool, scale, g, b)


def _ffn_ln_kernel(xb_ref, x_ref, hist_a_ref, hist_v_ref, wa_ref, wv_ref, cwa_ref, cwv_ref, cba_ref, cbv_ref,
                   wd_ref, g_ref, b_ref, *refs, seqs, rows, n_items, bf16_copy):
    if bf16_copy:
        o_ref, ob_ref, new_a_ref, new_v_ref, ext_a_ref, ext_v_ref = refs
    else:
        (o_ref, new_a_ref, new_v_ref, ext_a_ref, ext_v_ref), ob_ref = refs, None
    s = pl.program_id(0)
    halo = (CONV_W - 1) * seqs
    off = -(-halo // SUBLANES) * SUBLANES
    a_item = jnp.minimum(s, n_items - 1)
    i_a = lax.div(a_item, N_FF_TILES)
    j_a = a_item - i_a * N_FF_TILES
    j_b = lax.rem(jnp.maximum(s - 1, 0), N_FF_TILES)
    parity = s & 1

    @pl.when(s == 0)
    def _():
        ext_a_ref[1] = jnp.zeros(ext_a_ref.shape[1:], _F32)
        ext_v_ref[1] = jnp.zeros(ext_v_ref.shape[1:], _F32)

    @pl.when(i_a == 0)
    def _():
        ext_a_ref[parity, off - halo:off, :] = hist_a_ref[...]
        ext_v_ref[parity, off - halo:off, :] = hist_v_ref[...]

    @pl.when(i_a > 0)
    def _():
        ext_a_ref[parity, off - halo:off, :] = new_a_ref[j_a]
        ext_v_ref[parity, off - halo:off, :] = new_v_ref[j_a]

    @pl.when(j_b == 0)
    def _():
        o_ref[...] = jnp.zeros_like(o_ref)

    def both_halves(slot):
        xb = xb_ref[...]
        ha = jnp.dot(xb, wa_ref[...], preferred_element_type=_F32)
        hv = jnp.dot(xb, wv_ref[...], preferred_element_type=_F32)
        ext_a_ref[slot, off:off + rows, :] = ha
        ext_v_ref[slot, off:off + rows, :] = hv
        new_a_ref[j_a] = ha[rows - halo:, :]
        new_v_ref[j_a] = hv[rows - halo:, :]

        def conv(ext_ref, cw_ref, cb_ref):
            cw = cw_ref[...]
            c = cb_ref[...]
            for tap in range(CONV_W):
                start = off - (CONV_W - 1 - tap) * seqs
                c = c + cw[tap:tap + 1, :] * ext_ref[1 - slot, start:start + rows, :]
            return c

        ca = conv(ext_a_ref, cwa_ref, cba_ref)
        cv = conv(ext_v_ref, cwv_ref, cbv_ref)
        act = (_gelu_tanh(ca) * cv).astype(_BF16)
        o_ref[...] += jnp.dot(act, wd_ref[...], preferred_element_type=_F32)

    for slot in (0, 1):
        pl.when(parity == slot)(functools.partial(both_halves, slot))

    @pl.when(jnp.logical_and(s > 0, j_b == N_FF_TILES - 1))
    def _():
        out = _layer_norm(ALPHA * x_ref[...] + o_ref[...], g_ref[...], b_ref[...])
        o_ref[...] = out
        if ob_ref is not None:
            ob_ref[...] = out.astype(_BF16)


def _ffn_ln(xb, x, hist, w_up, conv_w, conv_b, w_down, g, b, *, layer, seqs, bf16_copy):
    m = x.shape[0]
    rows = min(ROW_TILE, m)
    halo = (CONV_W - 1) * seqs
    off = -(-halo // SUBLANES) * SUBLANES
    assert m % rows == 0 and rows >= halo
    n_items = (m // rows) * N_FF_TILES
    kern = functools.partial(_ffn_ln_kernel, seqs=seqs, rows=rows, n_items=n_items, bf16_copy=bf16_copy)

    def up_item(s):
        item = jnp.minimum(s, n_items - 1)
        return lax.div(item, N_FF_TILES), lax.rem(item, N_FF_TILES)

    def down_item(s):
        item = jnp.maximum(s - 1, 0)
        return lax.div(item, N_FF_TILES), lax.rem(item, N_FF_TILES)

    up_a = lambda s: (0, up_item(s)[1])
    up_v = lambda s: (0, up_item(s)[1] + N_FF_TILES)
    w_up_a = lambda s: (layer, 0, up_item(s)[1])
    w_up_v = lambda s: (layer, 0, up_item(s)[1] + N_FF_TILES)
    down_a = lambda s: (0, down_item(s)[1])
    down_v = lambda s: (0, down_item(s)[1] + N_FF_TILES)
    const = lambda s: (0, 0)
    down_rows = pl.BlockSpec((rows, D_MODEL), lambda s: (down_item(s)[0], 0))
    new_hist = pl.BlockSpec((N_FF_TILES, halo, FF_TILE), lambda s: (0, 0, 0))
    row_out = [jax.ShapeDtypeStruct((m, D_MODEL), _F32)]
    if bf16_copy:
        row_out.append(jax.ShapeDtypeStruct((m, D_MODEL), _BF16))
    return pl.pallas_call(
        kern,
        grid=(n_items + 1,),
        in_specs=[
            pl.BlockSpec((rows, D_MODEL), lambda s: (up_item(s)[0], 0)),
            down_rows,
            pl.BlockSpec((halo, FF_TILE), up_a),
            pl.BlockSpec((halo, FF_TILE), up_v),
            pl.BlockSpec((None, D_MODEL, FF_TILE), w_up_a),
            pl.BlockSpec((None, D_MODEL, FF_TILE), w_up_v),
            pl.BlockSpec((CONV_W, FF_TILE), down_a),
            pl.BlockSpec((CONV_W, FF_TILE), down_v),
            pl.BlockSpec((1, FF_TILE), down_a),
            pl.BlockSpec((1, FF_TILE), down_v),
            pl.BlockSpec((None, FF_TILE, D_MODEL), lambda s: (layer, down_item(s)[1], 0)),
            pl.BlockSpec((1, D_MODEL), const),
            pl.BlockSpec((1, D_MODEL), const),
        ],
        out_specs=[down_rows] * len(row_out) + [new_hist, new_hist],
        out_shape=row_out + [jax.ShapeDtypeStruct((N_FF_TILES, halo, FF_TILE), _F32)] * 2,
        scratch_shapes=[
            pltpu.VMEM((2, off + rows, FF_TILE), _F32),
            pltpu.VMEM((2, off + rows, FF_TILE), _F32),
        ],
        compiler_params=_params(1),
        name="ffn_ln",
    )(xb, x, hist, hist, w_up, w_up, conv_w, conv_w, conv_b, conv_b, w_down, g, b)


def _proj_kernel(x_ref, w_ref, *out_refs, scale):
    r = jnp.dot(x_ref[...], w_ref[...], preferred_element_type=_F32)
    for o_ref in out_refs:
        if o_ref.dtype == _F32:
            for h in range(o_ref.shape[1]):
                o_ref[:, h, :] = r[:, h * HEAD_W:(h + 1) * HEAD_W]
        else:
            o_ref[...] = (r * scale).astype(o_ref.dtype)


def _proj(x, w, out_dtypes, *, n, col0=0, layer=None, scale=1.0):
    m, k = x.shape
    rows = min(2 * ROW_TILE, m)
    cols = min(MM_COL_TILE, n)
    assert m % rows == 0 and n % cols == 0 and col0 % cols == 0 and cols % HEAD_W == 0
    if layer is None:
        w_spec = pl.BlockSpec((k, cols), lambda i, j: (0, j + col0 // cols))
    else:
        w_spec = pl.BlockSpec((None, k, cols), lambda i, j: (layer, 0, j + col0 // cols))
    flat = pl.BlockSpec((rows, cols), lambda i, j: (i, j)), (m, n)
    heads = pl.BlockSpec((rows, cols // HEAD_W, HEAD_W), lambda i, j: (i, j, 0)), (m, n // HEAD_W, HEAD_W)
    outs = [heads if dt == _F32 else flat for dt in out_dtypes]
    return pl.pallas_call(
        functools.partial(_proj_kernel, scale=scale),
        grid=(m // rows, n // cols),
        in_specs=[pl.BlockSpec((rows, k), lambda i, j: (i, 0)), w_spec],
        out_specs=[spec for spec, _ in outs],
        out_shape=[jax.ShapeDtypeStruct(shape, dt) for (_, shape), dt in zip(outs, out_dtypes)],
        compiler_params=_params(2),
        name="proj",
    )(x, w)


def _diff_lambda(lam_ref, lam_init):
    lam = lam_ref[...]
    e1 = jnp.exp(jnp.sum(lam[0:1, :] * lam[1:2, :], axis=-1, keepdims=True))
    e2 = jnp.exp(jnp.sum(lam[2:3, :] * lam[3:4, :], axis=-1, keepdims=True))
    return e1 - e2 + lam_init


def _split_parts(q):
    first = lax.broadcasted_iota(jnp.int32, (1, HEAD_W), 1) < QK_DIM
    zero = jnp.zeros_like(q)
    return jnp.concatenate([jnp.where(first, q, zero), jnp.where(first, zero, q)], axis=0)


def _sub_norm(o, sg_ref, lam_init):
    return o * lax.rsqrt(jnp.mean(o * o, axis=-1, keepdims=True) + LN_EPS) * sg_ref[...] * (1.0 - lam_init)


def _score(qz, kb):
    return lax.dot_general(qz, kb, (((1,), (1,)), ((), ())), preferred_element_type=_F32)


def _flash_kernel(qt_ref, k_ref, vt_ref, lam_ref, sg_ref, o_ref,
                  qz_ref, s_ref, p_ref, a_ref, m_ref, l_ref, acc_ref, *, lam_init, n_q_blocks):
    lam = _diff_lambda(lam_ref, lam_init)
    n_cols = 2 * Q_TILE
    first_part = lax.broadcasted_iota(jnp.int32, (HEAD_W, 1), 0) < QK_DIM
    chunk_shift = int(math.log2(CHUNK))

    def scores(b, slot):
        k0 = pl.multiple_of(b * K_TILE, K_TILE)
        s_ref[slot] = jnp.dot(k_ref[pl.ds(k0, K_TILE), :], qz_ref[...], preferred_element_type=_F32)

    def softmax(slot, visible):
        s = s_ref[slot]
        if visible is not None:
            s = jnp.where(visible, s, NEG_INF)
        m_prev = m_ref[...]
        m_new = jnp.maximum(m_prev, jnp.max(s, axis=0, keepdims=True))
        a = jnp.exp2(m_prev - m_new)
        p = jnp.exp2(s - m_new)
        l_ref[...] = a * l_ref[...] + jnp.sum(p, axis=0, keepdims=True)
        m_ref[...] = m_new
        a_ref[slot] = a
        p_ref[slot] = p.astype(_BF16)

    def values(b, slot):
        vb = vt_ref[jnp.maximum(b, 0)]
        acc_ref[...] = a_ref[slot] * acc_ref[...] + jnp.dot(vb, p_ref[slot], preferred_element_type=_F32)

    def step(t, slot):
        values(t - 1, 1 - slot)
        softmax(slot, None)
        scores(t + 1, 1 - slot)

    def q_block(qi, carry):
        qt = qt_ref[qi]
        zero = jnp.zeros_like(qt)
        qz_ref[:, 0:Q_TILE] = jnp.where(first_part, qt, zero)
        qz_ref[:, Q_TILE:n_cols] = jnp.where(first_part, zero, qt)
        m_ref[...] = jnp.full(m_ref.shape, NEG_INF, _F32)
        l_ref[...] = jnp.zeros_like(l_ref)
        acc_ref[...] = jnp.zeros_like(acc_ref)
        p_ref[1] = jnp.zeros(p_ref.shape[1:], _BF16)
        a_ref[1] = jnp.ones(a_ref.shape[1:], _F32)

        n_full = lax.div(qi, K_TILE // Q_TILE)
        scores(0, 0)

        def unrolled(u, c):
            for i in range(STEP_UNROLL):
                step(STEP_UNROLL * u + i, i & 1)
            return c

        def single(t, c):
            step(t, t & 1)
            return c

        n_unrolled = lax.div(n_full, STEP_UNROLL)
        lax.fori_loop(0, n_unrolled, unrolled, 0)
        lax.fori_loop(n_unrolled * STEP_UNROLL, n_full, single, 0)

        last = n_full & 1
        values(n_full - 1, 1 - last)
        key = lax.broadcasted_iota(jnp.int32, (K_TILE, 1), 0) + n_full * K_TILE
        col = lax.broadcasted_iota(jnp.int32, (1, n_cols), 1)
        query = jnp.where(col >= Q_TILE, col - Q_TILE, col) + qi * Q_TILE
        visible = (lax.shift_right_logical(key, chunk_shift)
                   <= lax.shift_right_logical(query, chunk_shift))
        softmax(last, visible)
        values(n_full, last)

        o2 = acc_ref[...] / l_ref[...]
        o = o2[:, 0:Q_TILE] - lam * o2[:, Q_TILE:n_cols]
        norm = lax.rsqrt(jnp.mean(o * o, axis=0, keepdims=True) + LN_EPS)
        o_ref[qi] = (o * norm * sg_ref[...] * (1.0 - lam_init)).astype(o_ref.dtype)
        return carry

    lax.fori_loop(0, n_q_blocks, q_block, 0)


def _flash_prompt(q, k, v, lam, sg, *, lam_init):
    t = q.shape[0]
    assert t % K_TILE == 0 and K_TILE % Q_TILE == 0 and Q_TILE % CHUNK == 0 and STEP_UNROLL % 2 == 0
    nq, nk = t // Q_TILE, t // K_TILE
    qt = jnp.transpose(q.reshape(nq, Q_TILE, N_HEADS, HEAD_W), (2, 0, 3, 1))
    vt = jnp.transpose(v.reshape(nk, K_TILE, N_HEADS, V_DIM), (2, 0, 3, 1))
    kern = functools.partial(_flash_kernel, lam_init=lam_init, n_q_blocks=nq)
    out = pl.pallas_call(
        kern,
        grid=(N_HEADS,),
        in_specs=[pl.BlockSpec((None, nq, HEAD_W, Q_TILE), lambda h: (h, 0, 0, 0)),
                  pl.BlockSpec((t, HEAD_W), lambda h: (0, h)),
                  pl.BlockSpec((None, nk, V_DIM, K_TILE), lambda h: (h, 0, 0, 0)),
                  pl.BlockSpec((4, QK_DIM), lambda h: (0, 0)),
                  pl.BlockSpec((V_DIM, 1), lambda h: (0, 0))],
        out_specs=pl.BlockSpec((None, nq, V_DIM, Q_TILE), lambda h: (h, 0, 0, 0)),
        out_shape=jax.ShapeDtypeStruct((N_HEADS, nq, V_DIM, Q_TILE), _BF16),
        scratch_shapes=[pltpu.VMEM((HEAD_W, 2 * Q_TILE), _BF16),
                        pltpu.VMEM((2, K_TILE, 2 * Q_TILE), _F32),
                        pltpu.VMEM((2, K_TILE, 2 * Q_TILE), _BF16),
                        pltpu.VMEM((2, 1, 2 * Q_TILE), _F32),
                        pltpu.VMEM((1, 2 * Q_TILE), _F32),
                        pltpu.VMEM((1, 2 * Q_TILE), _F32),
                        pltpu.VMEM((V_DIM, 2 * Q_TILE), _F32)],
        compiler_params=_params(1),
        name="flash_prompt",
    )(qt, k, vt, lam, sg.reshape(V_DIM, 1))
    return jnp.transpose(out, (1, 3, 0, 2)).reshape(t, N_HEADS * V_DIM)


def _sample_attn_kernel(q_ref, kn_ref, vn_ref, kc_ref, vc_ref, lam_ref, sg_ref, o_ref, *, lam_init, steps):
    heads = SAMPLE_HEADS_PER_STEP
    per_head = 2 * steps
    past = kc_ref.shape[1]
    lam = _diff_lambda(lam_ref, lam_init)
    head_cols = [slice(h * HEAD_W, (h + 1) * HEAD_W) for h in range(heads)]
    head_rows = [slice(h * per_head, (h + 1) * per_head) for h in range(heads)]

    qz = jnp.concatenate([_split_parts(q_ref[:, c]) for c in head_cols], axis=0)
    kc = kc_ref[0].reshape(past * heads, HEAD_W).astype(_BF16)
    vc = vc_ref[0].reshape(past * heads, V_DIM).astype(_BF16)
    sc = _score(qz, kc)
    row_head = lax.shift_right_logical(lax.broadcasted_iota(jnp.int32, (heads * per_head, 1), 0),
                                       int(math.log2(per_head)))
    col_head = lax.broadcasted_iota(jnp.int32, (1, past * heads), 1) & (heads - 1)
    sc = jnp.where(row_head == col_head, sc, NEG_INF)
    sn = jnp.concatenate([_score(qz[r, :], kn_ref[:, c]) for r, c in zip(head_rows, head_cols)], axis=0)
    m = jnp.maximum(jnp.max(sc, axis=1, keepdims=True), jnp.max(sn, axis=1, keepdims=True))
    pc = jnp.exp2(sc - m)
    pn = jnp.exp2(sn - m)
    l = jnp.sum(pc, axis=1, keepdims=True) + jnp.sum(pn, axis=1, keepdims=True)
    pn = pn.astype(_BF16)
    on = jnp.concatenate([jnp.dot(pn[r, :], vn_ref[:, c], preferred_element_type=_F32)
                          for r, c in zip(head_rows, head_cols)], axis=0)
    o2 = (jnp.dot(pc.astype(_BF16), vc, preferred_element_type=_F32) + on) / l
    for h in range(heads):
        first = h * per_head
        o = o2[first:first + steps, :] - lam * o2[first + steps:first + per_head, :]
        o_ref[:, head_cols[h]] = _sub_norm(o, sg_ref, lam_init).astype(o_ref.dtype)


def _sample_attn(q, k_new, v_new, cache_k, cache_v, lam, sg, *, lam_init, seqs):
    m, width = q.shape
    steps = m // seqs
    past = cache_k.shape[1]
    group_w = SAMPLE_HEADS_PER_STEP * HEAD_W
    groups = width // group_w
    kern = functools.partial(_sample_attn_kernel, lam_init=lam_init, steps=steps)
    new = pl.BlockSpec((steps, group_w), lambda b, g: (0, b * groups + g))
    cache = pl.BlockSpec((1, past, SAMPLE_HEADS_PER_STEP, HEAD_W), lambda b, g: (b, 0, g, 0))
    out = pl.pallas_call(
        kern,
        grid=(seqs, groups),
        in_specs=[new, new, new, cache, cache,
                  pl.BlockSpec((4, QK_DIM), lambda b, g: (0, 0)),
                  pl.BlockSpec((1, V_DIM), lambda b, g: (0, 0))],
        out_specs=new,
        out_shape=jax.ShapeDtypeStruct((steps, seqs * width), _BF16),
        compiler_params=_params(2),
        name="sample_attn",
    )(q.reshape(steps, seqs * width), k_new.reshape(steps, seqs * width),
      v_new.reshape(steps, seqs * width), cache_k, cache_v, lam, sg)
    return out.reshape(m, width)


def _oproj_ln_kernel(x_ref, a_ref, w_ref, g_ref, b_ref, o_ref, ob_ref):
    mix = jnp.dot(a_ref[...], w_ref[...], preferred_element_type=_F32)
    out = _layer_norm(ALPHA * x_ref[...] + mix, g_ref[...], b_ref[...])
    o_ref[...] = out
    ob_ref[...] = out.astype(_BF16)


def _oproj_ln(x, attn, w_o, g, b, *, layer):
    m = x.shape[0]
    rows = min(ROW_TILE, m)
    assert m % rows == 0
    const = lambda i: (0, 0)
    return pl.pallas_call(
        _oproj_ln_kernel,
        grid=(m // rows,),
        in_specs=[pl.BlockSpec((rows, D_MODEL), lambda i: (i, 0)),
                  pl.BlockSpec((rows, N_HEADS * V_DIM), lambda i: (i, 0)),
                  pl.BlockSpec((None, N_HEADS * V_DIM, D_MODEL), lambda i: (layer, 0, 0)),
                  pl.BlockSpec((1, D_MODEL), const),
                  pl.BlockSpec((1, D_MODEL), const)],
        out_specs=[pl.BlockSpec((rows, D_MODEL), lambda i: (i, 0))] * 2,
        out_shape=[jax.ShapeDtypeStruct((m, D_MODEL), _F32), jax.ShapeDtypeStruct((m, D_MODEL), _BF16)],
        compiler_params=_params(1),
        name="oproj_ln",
    )(x, attn, w_o, g, b)


def _run_group(x, pool_hist, conv_hist, caches, pos0, seqs, p):
    layer_inputs, new_conv = [], []
    k_f32 = v_f32 = k_b = v_b = xb = None
    for l in range(DEPTH):
        g1, b1 = p['ln1_g'][l:l + 1], p['ln1_b'][l:l + 1]
        if l < N_A_LAYERS:
            layer_inputs.append(x)
            x, xb = _pool_ln(x, pool_hist[l], p['w_pool'], p['pool_scale'][l:l + 1], g1, b1,
                             layer=l, seqs=seqs, pos0=pos0)
        else:
            j = l - N_A_LAYERS
            lam_init = 0.8 - 0.6 * math.exp(-0.3 * l)
            lam, sg = p['lam'][j], p['subln_g'][j:j + 1]
            (q,) = _proj(xb, p['w_q'], (_BF16,), n=N_HEADS * HEAD_W, layer=j, scale=Q_SCALE)
            if caches is None:
                attn = _flash_prompt(q, k_b, v_b, lam, sg, lam_init=lam_init)
            else:
                attn = _sample_attn(q, k_b, v_b, caches[0], caches[1], lam, sg,
                                    lam_init=lam_init, seqs=seqs)
            x, xb = _oproj_ln(x, attn, p['w_o'], g1, b1, layer=j)
        bf16_copy = N_A_LAYERS - 1 <= l < DEPTH - 1
        x, *xb, new_a, new_v = _ffn_ln(xb, x, conv_hist[l], p['w_up'], p['conv_w'][l], p['conv_b'][l:l + 1],
                                       p['w_down'], p['ln2_g'][l:l + 1], p['ln2_b'][l:l + 1],
                                       layer=l, seqs=seqs, bf16_copy=bf16_copy)
        xb = xb[0] if xb else None
        new_conv.append(jnp.concatenate([_untile_cols(new_a), _untile_cols(new_v)], axis=1))
        if l == N_A_LAYERS - 1:
            k_width = N_HEADS * HEAD_W
            k_f32, k_b = _proj(xb, p['w_kv'], (_F32, _BF16), n=k_width)
            v_f32, v_b = _proj(xb, p['w_kv'], (_F32, _BF16), n=N_HEADS * V_DIM, col0=k_width)
    return x, layer_inputs, new_conv, k_f32, v_f32


def _untile_cols(a):
    tiles, rows, w = a.shape
    return jnp.transpose(a, (1, 0, 2)).reshape(rows, tiles * w)


def _to_time_major(a):
    seqs, steps, w = a.shape
    return jnp.transpose(a, (1, 0, 2)).reshape(steps * seqs, w)


def _to_batch_major(a, seqs):
    return jnp.swapaxes(a.reshape(a.shape[0] // seqs, seqs, *a.shape[1:]), 0, 1)


def kernel(x_prompt, x_sample, state_pool, state_ffn_conv, cache_k, cache_v, ln1_g, ln1_b, ln2_g, ln2_b,
           w_pool, pool_scale, w_up, conv_w, conv_b, w_down, w_kv, w_q, lam, subln_g, w_o):
    p = dict(ln1_g=ln1_g, ln1_b=ln1_b, ln2_g=ln2_g, ln2_b=ln2_b, pool_scale=pool_scale,
             conv_w=conv_w, conv_b=conv_b, lam=lam, subln_g=subln_g,
             w_pool=w_pool.astype(_BF16), w_up=w_up.astype(_BF16), w_down=w_down.astype(_BF16),
             w_kv=w_kv.astype(_BF16), w_q=w_q.astype(_BF16), w_o=w_o.astype(_BF16))

    def finish(x, layer_inputs, new_conv, k_new, v_new, seqs):
        steps = x.shape[0] // seqs
        y = _to_batch_major(x, seqs)
        pool = jnp.stack([_to_batch_major(u[(steps - POOL_HIST) * seqs:], seqs) for u in layer_inputs])
        conv = jnp.stack([_to_batch_major(c, seqs) for c in new_conv])
        return y, pool, conv, _to_batch_major(k_new, seqs), _to_batch_major(v_new, seqs)

    b, t, _ = x_prompt.shape
    assert b == 1 and t >= POOL_HIST
    zeros_pool = jnp.zeros(((POOL_HIST + 1) * b, D_MODEL), _F32)
    zeros_conv = jnp.zeros(((CONV_W - 1) * b, 2 * D_FF), _F32)
    out_p = _run_group(_to_time_major(x_prompt), [zeros_pool] * N_A_LAYERS, [zeros_conv] * DEPTH,
                       None, 0, b, p)
    y_p, pool_p, conv_p, k_p, v_p = finish(*out_p, b)

    sb, steps, _ = x_sample.shape
    past = cache_k.shape[1]
    assert steps >= POOL_HIST
    pad = jnp.zeros((sb, D_MODEL), _F32)
    pool_hist = [jnp.concatenate([pad, _to_time_major(state_pool[l])], axis=0) for l in range(N_A_LAYERS)]
    conv_hist = [_to_time_major(state_ffn_conv[l]) for l in range(DEPTH)]
    out_s = _run_group(_to_time_major(x_sample), pool_hist, conv_hist, (cache_k, cache_v), past, sb, p)
    y_s, pool_s, conv_s, k_s, v_s = finish(*out_s, sb)

    return (y_p, y_s, pool_p, pool_s, conv_p, conv_s, k_p, v_p, k_s, v_s)
```

```python
import functools
import math

import jax
import jax.numpy as jnp
from jax import lax
from jax.experimental import pallas as pl
from jax.experimental.pallas import tpu as pltpu

D_MODEL = 2048
DEPTH = 4
N_A_LAYERS = DEPTH // 2
CHUNK = 64
POOL_WINDOWS = (2, 4, 8, 16)
POOL_GROUP = D_MODEL // len(POOL_WINDOWS)
POOL_HIST = max(POOL_WINDOWS) - 1
N_HEADS = 16
QK_DIM = 64
V_DIM = 2 * QK_DIM
HEAD_W = 2 * QK_DIM
D_FF = 5632
CONV_W = 3
LN_EPS = 1e-5
ALPHA = (2 * DEPTH) ** 0.25
ATTN_SCALE = QK_DIM ** -0.5
Q_SCALE = ATTN_SCALE * math.log2(math.e)
NEG_INF = -1e30

V7X_VMEM_BYTES = 64 * 1024 * 1024
VMEM_LIMIT_BYTES = V7X_VMEM_BYTES - 8 * 1024 * 1024
SUBLANES = 8

ROW_TILE = 512
FF_TILE = 512
N_FF_TILES = D_FF // FF_TILE
MM_COL_TILE = 1024
Q_TILE = 256
K_TILE = 512
STEP_UNROLL = 4
SAMPLE_HEADS_PER_STEP = 8

_F32 = jnp.float32
_BF16 = jnp.bfloat16


def _params(n_axes):
    return pltpu.CompilerParams(dimension_semantics=("arbitrary",) * n_axes,
                                vmem_limit_bytes=VMEM_LIMIT_BYTES)


def _layer_norm(z, g, b):
    mu = jnp.mean(z, axis=-1, keepdims=True)
    zc = z - mu
    var = jnp.mean(zc * zc, axis=-1, keepdims=True)
    return zc * lax.rsqrt(var + LN_EPS) * g + b


def _gelu_tanh(x):
    cdf = 0.5 * (1.0 + jnp.tanh(math.sqrt(2.0 / math.pi) * (x + 0.044715 * (x * x * x))))
    return x * cdf


def _pool_ln_kernel(x_ref, halo_ref, hist_ref, w_ref, scale_ref, g_ref, b_ref, o_ref, ob_ref,
                    ext_ref, z_ref, *, seqs, rows, pos0):
    i = pl.program_id(0)
    halo_rows = (POOL_HIST + 1) * seqs

    @pl.when(i == 0)
    def _():
        ext_ref[0:halo_rows, :] = hist_ref[...]

    @pl.when(i > 0)
    def _():
        ext_ref[0:halo_rows, :] = halo_ref[...]

    ext_ref[halo_rows:halo_rows + rows, :] = x_ref[...]

    row = lax.broadcasted_iota(jnp.int32, (rows, 1), 0) + i * rows
    step = lax.shift_right_logical(row, int(math.log2(seqs)))
    pos_plus_1 = (step + (pos0 + 1)).astype(_F32)

    for g, w in enumerate(POOL_WINDOWS):
        cols = slice(g * POOL_GROUP, (g + 1) * POOL_GROUP)
        cur = x_ref[:, cols]
        win = cur
        for k in range(1, w):
            start = halo_rows - k * seqs
            win = win + ext_ref[start:start + rows, cols]
        cnt = jnp.minimum(float(w), pos_plus_1)
        d = win / cnt - cur
        mix = jnp.dot(d.astype(_BF16), w_ref[g], preferred_element_type=_F32)
        z_ref[:, cols] = ALPHA * cur + mix * scale_ref[:, cols]

    out = _layer_norm(z_ref[...], g_ref[...], b_ref[...])
    o_ref[...] = out
    ob_ref[...] = out.astype(_BF16)


def _pool_ln(x, hist, w_pool, scale, g, b, *, layer, seqs, pos0):
    m = x.shape[0]
    rows = min(ROW_TILE, m)
    halo_rows = (POOL_HIST + 1) * seqs
    assert m % rows == 0 and rows % halo_rows == 0 and seqs & (seqs - 1) == 0
    halo_blocks_per_tile = rows // halo_rows
    kern = functools.partial(_pool_ln_kernel, seqs=seqs, rows=rows, pos0=pos0)
    return pl.pallas_call(
        kern,
        grid=(m // rows,),
        in_specs=[
            pl.BlockSpec((rows, D_MODEL), lambda i: (i, 0)),
            pl.BlockSpec((halo_rows, D_MODEL),
                         lambda i: (jnp.maximum(i * halo_blocks_per_tile - 1, 0), 0)),
            pl.BlockSpec((halo_rows, D_MODEL), lambda i: (0, 0)),
            pl.BlockSpec((None, len(POOL_WINDOWS), POOL_GROUP, POOL_GROUP), lambda i: (layer, 0, 0, 0)),
            pl.BlockSpec((None, 1, D_MODEL), lambda i: (layer, 0, 0)),
            pl.BlockSpec((None, 1, D_MODEL), lambda i: (layer, 0, 0)),
            pl.BlockSpec((None, 1, D_MODEL), lambda i: (layer, 0, 0)),
        ],
        out_specs=[pl.BlockSpec((rows, D_MODEL), lambda i: (i, 0))] * 2,
        out_shape=[jax.ShapeDtypeStruct((m, D_MODEL), _F32), jax.ShapeDtypeStruct((m, D_MODEL), _BF16)],
        scratch_shapes=[pltpu.VMEM((halo_rows + rows, D_MODEL), _F32),
                        pltpu.VMEM((rows, D_MODEL), _F32)],
        compiler_params=_params(1),
        name="pool_ln",
    )(x, x, hist, w_pool, scale, g, b)


def _ffn_ln_kernel(xb_ref, x_ref, hist_a_ref, hist_v_ref, wa_ref, wv_ref, cwa_ref, cwv_ref, cba_ref, cbv_ref,
                   wd_ref, g_ref, b_ref, *refs, seqs, rows, n_items, bf16_copy):
    if bf16_copy:
        o_ref, ob_ref, new_a_ref, new_v_ref, ext_a_ref, ext_v_ref = refs
    else:
        (o_ref, new_a_ref, new_v_ref, ext_a_ref, ext_v_ref), ob_ref = refs, None
    s = pl.program_id(0)
    halo = (CONV_W - 1) * seqs
    a_item = jnp.minimum(s, n_items - 1)
    i_a = lax.div(a_item, N_FF_TILES)
    j_a = a_item - i_a * N_FF_TILES
    j_b = lax.rem(jnp.maximum(s - 1, 0), N_FF_TILES)
    parity = s & 1

    @pl.when(s == 0)
    def _():
        ext_a_ref[1] = jnp.zeros(ext_a_ref.shape[1:], _F32)
        ext_v_ref[1] = jnp.zeros(ext_v_ref.shape[1:], _F32)

    n_copies = ext_a_ref.shape[1]
    delays = range(CONV_W)
    if n_copies == 1:
        copy_of, write_at, read_at = [0] * CONV_W, [halo] * CONV_W, [halo - d * seqs for d in delays]
    else:
        copy_of, write_at, read_at = list(delays), [d * seqs for d in delays], [0] * CONV_W

    def fill_history(src_a, src_v):
        for c in range(n_copies):
            n = write_at[copy_of.index(c)]
            if n:
                ext_a_ref[parity, c, 0:n, :] = src_a[halo - n:halo, :]
                ext_v_ref[parity, c, 0:n, :] = src_v[halo - n:halo, :]

    @pl.when(i_a == 0)
    def _():
        fill_history(hist_a_ref[...], hist_v_ref[...])

    @pl.when(i_a > 0)
    def _():
        fill_history(new_a_ref[j_a], new_v_ref[j_a])

    @pl.when(j_b == 0)
    def _():
        o_ref[...] = jnp.zeros_like(o_ref)

    def both_halves(slot):
        xb = xb_ref[...]
        ha = jnp.dot(xb, wa_ref[...], preferred_element_type=_F32)
        hv = jnp.dot(xb, wv_ref[...], preferred_element_type=_F32)
        for c in range(n_copies):
            at = write_at[copy_of.index(c)]
            ext_a_ref[slot, c, at:at + rows, :] = ha
            ext_v_ref[slot, c, at:at + rows, :] = hv
        new_a_ref[j_a] = ha[rows - halo:, :]
        new_v_ref[j_a] = hv[rows - halo:, :]

        def conv(ext_ref, cw_ref, cb_ref):
            cw = cw_ref[...]
            c = cb_ref[...]
            for tap in range(CONV_W):
                d = CONV_W - 1 - tap
                c = c + cw[tap:tap + 1, :] * ext_ref[1 - slot, copy_of[d], read_at[d]:read_at[d] + rows, :]
            return c

        ca = conv(ext_a_ref, cwa_ref, cba_ref)
        cv = conv(ext_v_ref, cwv_ref, cbv_ref)
        act = (_gelu_tanh(ca) * cv).astype(_BF16)
        o_ref[...] += jnp.dot(act, wd_ref[...], preferred_element_type=_F32)

    for slot in (0, 1):
        pl.when(parity == slot)(functools.partial(both_halves, slot))

    @pl.when(jnp.logical_and(s > 0, j_b == N_FF_TILES - 1))
    def _():
        out = _layer_norm(ALPHA * x_ref[...] + o_ref[...], g_ref[...], b_ref[...])
        o_ref[...] = out
        if ob_ref is not None:
            ob_ref[...] = out.astype(_BF16)


def _ffn_ln(xb, x, hist, w_up, conv_w, conv_b, w_down, g, b, *, layer, seqs, bf16_copy):
    m = x.shape[0]
    rows = min(ROW_TILE, m)
    halo = (CONV_W - 1) * seqs
    ext_rows = -(-(rows + halo) // SUBLANES) * SUBLANES
    n_copies = 1 if seqs % SUBLANES == 0 else CONV_W
    assert m % rows == 0 and rows >= halo
    n_items = (m // rows) * N_FF_TILES
    kern = functools.partial(_ffn_ln_kernel, seqs=seqs, rows=rows, n_items=n_items, bf16_copy=bf16_copy)

    def up_item(s):
        item = jnp.minimum(s, n_items - 1)
        return lax.div(item, N_FF_TILES), lax.rem(item, N_FF_TILES)

    def down_item(s):
        item = jnp.maximum(s - 1, 0)
        return lax.div(item, N_FF_TILES), lax.rem(item, N_FF_TILES)

    up_a = lambda s: (0, up_item(s)[1])
    up_v = lambda s: (0, up_item(s)[1] + N_FF_TILES)
    w_up_a = lambda s: (layer, 0, up_item(s)[1])
    w_up_v = lambda s: (layer, 0, up_item(s)[1] + N_FF_TILES)
    conv_w_a = lambda s: (layer, 0, down_item(s)[1])
    conv_w_v = lambda s: (layer, 0, down_item(s)[1] + N_FF_TILES)
    ln_row = lambda s: (layer, 0, 0)
    down_rows = pl.BlockSpec((rows, D_MODEL), lambda s: (down_item(s)[0], 0))
    new_hist = pl.BlockSpec((N_FF_TILES, halo, FF_TILE), lambda s: (0, 0, 0))
    row_out = [jax.ShapeDtypeStruct((m, D_MODEL), _F32)]
    if bf16_copy:
        row_out.append(jax.ShapeDtypeStruct((m, D_MODEL), _BF16))
    return pl.pallas_call(
        kern,
        grid=(n_items + 1,),
        in_specs=[
            pl.BlockSpec((rows, D_MODEL), lambda s: (up_item(s)[0], 0)),
            down_rows,
            pl.BlockSpec((halo, FF_TILE), up_a),
            pl.BlockSpec((halo, FF_TILE), up_v),
            pl.BlockSpec((None, D_MODEL, FF_TILE), w_up_a),
            pl.BlockSpec((None, D_MODEL, FF_TILE), w_up_v),
            pl.BlockSpec((None, CONV_W, FF_TILE), conv_w_a),
            pl.BlockSpec((None, CONV_W, FF_TILE), conv_w_v),
            pl.BlockSpec((None, 1, FF_TILE), conv_w_a),
            pl.BlockSpec((None, 1, FF_TILE), conv_w_v),
            pl.BlockSpec((None, FF_TILE, D_MODEL), lambda s: (layer, down_item(s)[1], 0)),
            pl.BlockSpec((None, 1, D_MODEL), ln_row),
            pl.BlockSpec((None, 1, D_MODEL), ln_row),
        ],
        out_specs=[down_rows] * len(row_out) + [new_hist, new_hist],
        out_shape=row_out + [jax.ShapeDtypeStruct((N_FF_TILES, halo, FF_TILE), _F32)] * 2,
        scratch_shapes=[
            pltpu.VMEM((2, n_copies, ext_rows, FF_TILE), _F32),
            pltpu.VMEM((2, n_copies, ext_rows, FF_TILE), _F32),
        ],
        compiler_params=_params(1),
        name="ffn_ln",
    )(xb, x, hist, hist, w_up, w_up, conv_w, conv_w, conv_b, conv_b, w_down, g, b)


def _proj_kernel(x_ref, w_ref, *out_refs, scale):
    r = jnp.dot(x_ref[...], w_ref[...], preferred_element_type=_F32)
    for o_ref in out_refs:
        if o_ref.dtype == _F32:
            for h in range(o_ref.shape[1]):
                o_ref[:, h, :] = r[:, h * HEAD_W:(h + 1) * HEAD_W]
        else:
            o_ref[...] = (r * scale).astype(o_ref.dtype)


def _proj(x, w, out_dtypes, *, n, col0=0, layer=None, scale=1.0):
    m, k = x.shape
    rows = min(2 * ROW_TILE, m)
    cols = min(MM_COL_TILE, n)
    assert m % rows == 0 and n % cols == 0 and col0 % cols == 0 and cols % HEAD_W == 0
    if layer is None:
        w_spec = pl.BlockSpec((k, cols), lambda i, j: (0, j + col0 // cols))
    else:
        w_spec = pl.BlockSpec((None, k, cols), lambda i, j: (layer, 0, j + col0 // cols))
    flat = pl.BlockSpec((rows, cols), lambda i, j: (i, j)), (m, n)
    heads = pl.BlockSpec((rows, cols // HEAD_W, HEAD_W), lambda i, j: (i, j, 0)), (m, n // HEAD_W, HEAD_W)
    outs = [heads if dt == _F32 else flat for dt in out_dtypes]
    return pl.pallas_call(
        functools.partial(_proj_kernel, scale=scale),
        grid=(m // rows, n // cols),
        in_specs=[pl.BlockSpec((rows, k), lambda i, j: (i, 0)), w_spec],
        out_specs=[spec for spec, _ in outs],
        out_shape=[jax.ShapeDtypeStruct(shape, dt) for (_, shape), dt in zip(outs, out_dtypes)],
        compiler_params=_params(2),
        name="proj",
    )(x, w)


def _diff_lambda(lam_ref, lam_init):
    lam = lam_ref[...]
    e1 = jnp.exp(jnp.sum(lam[0:1, :] * lam[1:2, :], axis=-1, keepdims=True))
    e2 = jnp.exp(jnp.sum(lam[2:3, :] * lam[3:4, :], axis=-1, keepdims=True))
    return e1 - e2 + lam_init


def _split_parts(q):
    first = lax.broadcasted_iota(jnp.int32, (1, HEAD_W), 1) < QK_DIM
    zero = jnp.zeros_like(q)
    return jnp.concatenate([jnp.where(first, q, zero), jnp.where(first, zero, q)], axis=0)


def _sub_norm(o, sg_ref, lam_init):
    return o * lax.rsqrt(jnp.mean(o * o, axis=-1, keepdims=True) + LN_EPS) * sg_ref[...] * (1.0 - lam_init)


def _score(qz, kb):
    return lax.dot_general(qz, kb, (((1,), (1,)), ((), ())), preferred_element_type=_F32)


def _flash_kernel(qt_ref, k_ref, vt_ref, bias_ref, lam_ref, sg_ref, o_ref,
                  qz_ref, s_ref, p_ref, a_ref, m_ref, l_ref, acc_ref, *, lam_init, n_q_blocks):
    lam = _diff_lambda(lam_ref, lam_init)
    n_cols = 2 * Q_TILE
    first_part = lax.broadcasted_iota(jnp.int32, (HEAD_W, 1), 0) < QK_DIM

    l_ref[...] = jnp.zeros_like(l_ref)
    acc_ref[...] = jnp.zeros_like(acc_ref)
    p_ref[1] = jnp.zeros(p_ref.shape[1:], _BF16)
    a_ref[1] = jnp.ones(a_ref.shape[1:], _F32)

    def load_queries(qi):
        qt = qt_ref[qi]
        zero = jnp.zeros_like(qt)
        qz_ref[qi & 1, :, 0:Q_TILE] = jnp.where(first_part, qt, zero)
        qz_ref[qi & 1, :, Q_TILE:n_cols] = jnp.where(first_part, zero, qt)

    def scores(qi, b, slot):
        k0 = pl.multiple_of(b * K_TILE, K_TILE)
        s_ref[slot] = jnp.dot(k_ref[pl.ds(k0, K_TILE), :], qz_ref[qi & 1], preferred_element_type=_F32)

    def softmax(slot, bias):
        s = s_ref[slot]
        if bias is not None:
            s = s + bias
        m_prev = m_ref[...]
        m_new = jnp.maximum(m_prev, jnp.max(s, axis=0, keepdims=True))
        a = jnp.exp2(m_prev - m_new)
        p = jnp.exp2(s - m_new)
        l_ref[...] = a * l_ref[...] + jnp.sum(p, axis=0, keepdims=True)
        m_ref[...] = m_new
        a_ref[slot] = a
        p_ref[slot] = p.astype(_BF16)

    def values(b, slot):
        vb = vt_ref[jnp.maximum(b, 0)]
        acc_ref[...] = a_ref[slot] * acc_ref[...] + jnp.dot(vb, p_ref[slot], preferred_element_type=_F32)

    def step(qi, t, slot):
        values(t - 1, 1 - slot)
        softmax(slot, None)
        scores(qi, t + 1, 1 - slot)

    def q_block(qi, carry):
        m_ref[...] = jnp.full(m_ref.shape, NEG_INF, _F32)
        n_full = lax.div(qi, K_TILE // Q_TILE)

        def unrolled(u, c):
            for i in range(STEP_UNROLL):
                step(qi, STEP_UNROLL * u + i, i & 1)
            return c

        def single(t, c):
            step(qi, t, t & 1)
            return c

        n_unrolled = lax.div(n_full, STEP_UNROLL)
        lax.fori_loop(0, n_unrolled, unrolled, 0)
        lax.fori_loop(n_unrolled * STEP_UNROLL, n_full, single, 0)

        last = n_full & 1
        values(n_full - 1, 1 - last)
        softmax(last, bias_ref[qi - n_full * (K_TILE // Q_TILE)])
        nxt = jnp.minimum(qi + 1, n_q_blocks - 1)
        load_queries(nxt)
        scores(nxt, 0, 0)
        values(n_full, last)

        o2 = acc_ref[...] / l_ref[...]
        o = o2[:, 0:Q_TILE] - lam * o2[:, Q_TILE:n_cols]
        norm = lax.rsqrt(jnp.mean(o * o, axis=0, keepdims=True) + LN_EPS)
        o_ref[qi] = (o * norm * sg_ref[...] * (1.0 - lam_init)).astype(o_ref.dtype)
        return carry

    load_queries(0)
    scores(0, 0, 0)
    lax.fori_loop(0, n_q_blocks, q_block, 0)


def _flash_prompt(q, k, v, lam, sg, *, layer, lam_init):
    t = q.shape[0]
    assert t % K_TILE == 0 and K_TILE % Q_TILE == 0 and Q_TILE % CHUNK == 0 and STEP_UNROLL % 2 == 0
    nq, nk = t // Q_TILE, t // K_TILE
    qt = jnp.transpose(q.reshape(nq, Q_TILE, N_HEADS, HEAD_W), (2, 0, 3, 1))
    vt = jnp.transpose(v.reshape(nk, K_TILE, N_HEADS, V_DIM), (2, 0, 3, 1))
    sub = K_TILE // Q_TILE
    key_chunk = (jnp.arange(K_TILE) // CHUNK)[None, :, None]
    query = jnp.arange(sub)[:, None, None] * Q_TILE + (jnp.arange(2 * Q_TILE) % Q_TILE)[None, None, :]
    tail_bias = jnp.where(key_chunk <= query // CHUNK, 0.0, NEG_INF).astype(_F32)
    kern = functools.partial(_flash_kernel, lam_init=lam_init, n_q_blocks=nq)
    out = pl.pallas_call(
        kern,
        grid=(N_HEADS,),
        in_specs=[pl.BlockSpec((None, nq, HEAD_W, Q_TILE), lambda h: (h, 0, 0, 0)),
                  pl.BlockSpec((t, HEAD_W), lambda h: (0, h)),
                  pl.BlockSpec((None, nk, V_DIM, K_TILE), lambda h: (h, 0, 0, 0)),
                  pl.BlockSpec((sub, K_TILE, 2 * Q_TILE), lambda h: (0, 0, 0)),
                  pl.BlockSpec((None, 4, QK_DIM), lambda h: (layer, 0, 0)),
                  pl.BlockSpec((None, V_DIM, 1), lambda h: (layer, 0, 0))],
        out_specs=pl.BlockSpec((None, nq, V_DIM, Q_TILE), lambda h: (h, 0, 0, 0)),
        out_shape=jax.ShapeDtypeStruct((N_HEADS, nq, V_DIM, Q_TILE), _BF16),
        scratch_shapes=[pltpu.VMEM((2, HEAD_W, 2 * Q_TILE), _BF16),
                        pltpu.VMEM((2, K_TILE, 2 * Q_TILE), _F32),
                        pltpu.VMEM((2, K_TILE, 2 * Q_TILE), _BF16),
                        pltpu.VMEM((2, 1, 2 * Q_TILE), _F32),
                        pltpu.VMEM((1, 2 * Q_TILE), _F32),
                        pltpu.VMEM((1, 2 * Q_TILE), _F32),
                        pltpu.VMEM((V_DIM, 2 * Q_TILE), _F32)],
        compiler_params=_params(1),
        name="flash_prompt",
    )(qt, k, vt, tail_bias, lam, sg.reshape(sg.shape[0], V_DIM, 1))
    return jnp.transpose(out, (1, 3, 0, 2)).reshape(t, N_HEADS * V_DIM)


def _sample_attn_kernel(q_ref, kn_ref, vn_ref, kc_ref, vc_ref, lam_ref, sg_ref, o_ref, *, lam_init, steps):
    heads = SAMPLE_HEADS_PER_STEP
    per_head = 2 * steps
    past = kc_ref.shape[1]
    lam = _diff_lambda(lam_ref, lam_init)
    head_cols = [slice(h * HEAD_W, (h + 1) * HEAD_W) for h in range(heads)]
    head_rows = [slice(h * per_head, (h + 1) * per_head) for h in range(heads)]

    qz = jnp.concatenate([_split_parts(q_ref[:, c]) for c in head_cols], axis=0)
    kc = kc_ref[0].reshape(past * heads, HEAD_W).astype(_BF16)
    vc = vc_ref[0].reshape(past * heads, V_DIM).astype(_BF16)
    sc = _score(qz, kc)
    row_head = lax.shift_right_logical(lax.broadcasted_iota(jnp.int32, (heads * per_head, 1), 0),
                                       int(math.log2(per_head)))
    col_head = lax.broadcasted_iota(jnp.int32, (1, past * heads), 1) & (heads - 1)
    sc = jnp.where(row_head == col_head, sc, NEG_INF)
    sn = jnp.concatenate([_score(qz[r, :], kn_ref[:, c]) for r, c in zip(head_rows, head_cols)], axis=0)
    m = jnp.maximum(jnp.max(sc, axis=1, keepdims=True), jnp.max(sn, axis=1, keepdims=True))
    pc = jnp.exp2(sc - m)
    pn = jnp.exp2(sn - m)
    l = jnp.sum(pc, axis=1, keepdims=True) + jnp.sum(pn, axis=1, keepdims=True)
    pn = pn.astype(_BF16)
    on = jnp.concatenate([jnp.dot(pn[r, :], vn_ref[:, c], preferred_element_type=_F32)
                          for r, c in zip(head_rows, head_cols)], axis=0)
    o2 = (jnp.dot(pc.astype(_BF16), vc, preferred_element_type=_F32) + on) / l
    for h in range(heads):
        first = h * per_head
        o = o2[first:first + steps, :] - lam * o2[first + steps:first + per_head, :]
        o_ref[:, head_cols[h]] = _sub_norm(o, sg_ref, lam_init).astype(o_ref.dtype)


def _sample_attn(q, k_new, v_new, cache_k, cache_v, lam, sg, *, layer, lam_init, seqs):
    m, width = q.shape
    steps = m // seqs
    past = cache_k.shape[1]
    group_w = SAMPLE_HEADS_PER_STEP * HEAD_W
    groups = width // group_w
    kern = functools.partial(_sample_attn_kernel, lam_init=lam_init, steps=steps)
    new = pl.BlockSpec((steps, group_w), lambda b, g: (0, b * groups + g))
    cache = pl.BlockSpec((1, past, SAMPLE_HEADS_PER_STEP, HEAD_W), lambda b, g: (b, 0, g, 0))
    out = pl.pallas_call(
        kern,
        grid=(seqs, groups),
        in_specs=[new, new, new, cache, cache,
                  pl.BlockSpec((None, 4, QK_DIM), lambda b, g: (layer, 0, 0)),
                  pl.BlockSpec((None, 1, V_DIM), lambda b, g: (layer, 0, 0))],
        out_specs=new,
        out_shape=jax.ShapeDtypeStruct((steps, seqs * width), _BF16),
        compiler_params=_params(2),
        name="sample_attn",
    )(q.reshape(steps, seqs * width), k_new.reshape(steps, seqs * width),
      v_new.reshape(steps, seqs * width), cache_k, cache_v, lam, sg)
    return out.reshape(m, width)


def _oproj_ln_kernel(x_ref, a_ref, w_ref, g_ref, b_ref, o_ref, ob_ref):
    mix = jnp.dot(a_ref[...], w_ref[...], preferred_element_type=_F32)
    out = _layer_norm(ALPHA * x_ref[...] + mix, g_ref[...], b_ref[...])
    o_ref[...] = out
    ob_ref[...] = out.astype(_BF16)


def _oproj_ln(x, attn, w_o, g, b, *, layer, ln_layer):
    m = x.shape[0]
    rows = min(ROW_TILE, m)
    assert m % rows == 0
    const = lambda i: (ln_layer, 0, 0)
    return pl.pallas_call(
        _oproj_ln_kernel,
        grid=(m // rows,),
        in_specs=[pl.BlockSpec((rows, D_MODEL), lambda i: (i, 0)),
                  pl.BlockSpec((rows, N_HEADS * V_DIM), lambda i: (i, 0)),
                  pl.BlockSpec((None, N_HEADS * V_DIM, D_MODEL), lambda i: (layer, 0, 0)),
                  pl.BlockSpec((None, 1, D_MODEL), const),
                  pl.BlockSpec((None, 1, D_MODEL), const)],
        out_specs=[pl.BlockSpec((rows, D_MODEL), lambda i: (i, 0))] * 2,
        out_shape=[jax.ShapeDtypeStruct((m, D_MODEL), _F32), jax.ShapeDtypeStruct((m, D_MODEL), _BF16)],
        compiler_params=_params(1),
        name="oproj_ln",
    )(x, attn, w_o, g, b)


def _run_group(x, pool_hist, conv_hist, caches, pos0, seqs, p):
    layer_inputs, new_conv = [], []
    k_f32 = v_f32 = k_b = v_b = xb = None
    for l in range(DEPTH):
        if l < N_A_LAYERS:
            layer_inputs.append(x)
            x, xb = _pool_ln(x, pool_hist[l], p['w_pool'], p['pool_scale'], p['ln1_g'], p['ln1_b'],
                             layer=l, seqs=seqs, pos0=pos0)
        else:
            j = l - N_A_LAYERS
            lam_init = 0.8 - 0.6 * math.exp(-0.3 * l)
            (q,) = _proj(xb, p['w_q'], (_BF16,), n=N_HEADS * HEAD_W, layer=j, scale=Q_SCALE)
            if caches is None:
                attn = _flash_prompt(q, k_b, v_b, p['lam'], p['subln_g'], layer=j, lam_init=lam_init)
            else:
                attn = _sample_attn(q, k_b, v_b, caches[0], caches[1], p['lam'], p['subln_g'],
                                    layer=j, lam_init=lam_init, seqs=seqs)
            x, xb = _oproj_ln(x, attn, p['w_o'], p['ln1_g'], p['ln1_b'], layer=j, ln_layer=l)
        bf16_copy = N_A_LAYERS - 1 <= l < DEPTH - 1
        x, *xb, new_a, new_v = _ffn_ln(xb, x, conv_hist[l], p['w_up'], p['conv_w'], p['conv_b'],
                                       p['w_down'], p['ln2_g'], p['ln2_b'],
                                       layer=l, seqs=seqs, bf16_copy=bf16_copy)
        xb = xb[0] if xb else None
        new_conv.append(jnp.concatenate([_untile_cols(new_a), _untile_cols(new_v)], axis=1))
        if l == N_A_LAYERS - 1:
            k_width = N_HEADS * HEAD_W
            k_f32, k_b = _proj(xb, p['w_kv'], (_F32, _BF16), n=k_width)
            v_f32, v_b = _proj(xb, p['w_kv'], (_F32, _BF16), n=N_HEADS * V_DIM, col0=k_width)
    return x, layer_inputs, new_conv, k_f32, v_f32


def _untile_cols(a):
    tiles, rows, w = a.shape
    return jnp.transpose(a, (1, 0, 2)).reshape(rows, tiles * w)


def _to_time_major(a):
    seqs, steps, w = a.shape
    return jnp.transpose(a, (1, 0, 2)).reshape(steps * seqs, w)


def _to_batch_major(a, seqs):
    return jnp.swapaxes(a.reshape(a.shape[0] // seqs, seqs, *a.shape[1:]), 0, 1)


def kernel(x_prompt, x_sample, state_pool, state_ffn_conv, cache_k, cache_v, ln1_g, ln1_b, ln2_g, ln2_b,
           w_pool, pool_scale, w_up, conv_w, conv_b, w_down, w_kv, w_q, lam, subln_g, w_o):
    rows = lambda a: a[:, None, :]
    p = dict(ln1_g=rows(ln1_g), ln1_b=rows(ln1_b), ln2_g=rows(ln2_g), ln2_b=rows(ln2_b),
             pool_scale=rows(pool_scale), conv_w=conv_w, conv_b=rows(conv_b), lam=lam, subln_g=rows(subln_g),
             w_pool=w_pool.astype(_BF16), w_up=w_up.astype(_BF16), w_down=w_down.astype(_BF16),
             w_kv=w_kv.astype(_BF16), w_q=w_q.astype(_BF16), w_o=w_o.astype(_BF16))

    def finish(x, layer_inputs, new_conv, k_new, v_new, seqs):
        steps = x.shape[0] // seqs
        y = _to_batch_major(x, seqs)
        pool = jnp.stack([_to_batch_major(u[(steps - POOL_HIST) * seqs:], seqs) for u in layer_inputs])
        conv = jnp.stack([_to_batch_major(c, seqs) for c in new_conv])
        return y, pool, conv, _to_batch_major(k_new, seqs), _to_batch_major(v_new, seqs)

    b, t, _ = x_prompt.shape
    assert b == 1 and t >= POOL_HIST
    zeros_pool = jnp.zeros(((POOL_HIST + 1) * b, D_MODEL), _F32)
    zeros_conv = jnp.zeros(((CONV_W - 1) * b, 2 * D_FF), _F32)
    out_p = _run_group(_to_time_major(x_prompt), [zeros_pool] * N_A_LAYERS, [zeros_conv] * DEPTH,
                       None, 0, b, p)
    y_p, pool_p, conv_p, k_p, v_p = finish(*out_p, b)

    sb, steps, _ = x_sample.shape
    past = cache_k.shape[1]
    assert steps >= POOL_HIST
    pad = jnp.zeros((sb, D_MODEL), _F32)
    pool_hist = [jnp.concatenate([pad, _to_time_major(state_pool[l])], axis=0) for l in range(N_A_LAYERS)]
    conv_hist = [_to_time_major(state_ffn_conv[l]) for l in range(DEPTH)]
    out_s = _run_group(_to_time_major(x_sample), pool_hist, conv_hist, (cache_k, cache_v), past, sb, p)
    y_s, pool_s, conv_s, k_s, v_s = finish(*out_s, sb)

    return (y_p, y_s, pool_p, pool_s, conv_p, conv_s, k_p, v_p, k_s, v_s)
```

```python
import functools
import math

import jax
import jax.numpy as jnp
from jax import lax
from jax.experimental import pallas as pl
from jax.experimental.pallas import tpu as pltpu

D_MODEL = 2048
DEPTH = 4
N_A_LAYERS = DEPTH // 2
CHUNK = 64
POOL_WINDOWS = (2, 4, 8, 16)
POOL_GROUP = D_MODEL // len(POOL_WINDOWS)
POOL_HIST = max(POOL_WINDOWS) - 1
N_HEADS = 16
QK_DIM = 64
V_DIM = 2 * QK_DIM
HEAD_W = 2 * QK_DIM
D_FF = 5632
CONV_W = 3
LN_EPS = 1e-5
ALPHA = (2 * DEPTH) ** 0.25
ATTN_SCALE = QK_DIM ** -0.5
Q_SCALE = ATTN_SCALE * math.log2(math.e)
NEG_INF = -1e30

V7X_VMEM_BYTES = 64 * 1024 * 1024
VMEM_LIMIT_BYTES = V7X_VMEM_BYTES - 8 * 1024 * 1024
SUBLANES = 8

ROW_TILE = 512
FF_TILE = 512
N_FF_TILES = D_FF // FF_TILE
MM_COL_TILE = 1024
Q_TILE = 256
K_TILE = 512
STEP_UNROLL = 4
SAMPLE_HEADS_PER_STEP = 8

_F32 = jnp.float32
_BF16 = jnp.bfloat16


def _params(n_axes):
    return pltpu.CompilerParams(dimension_semantics=("arbitrary",) * n_axes,
                                vmem_limit_bytes=VMEM_LIMIT_BYTES)


def _layer_norm(z, g, b):
    mu = jnp.mean(z, axis=-1, keepdims=True)
    zc = z - mu
    var = jnp.mean(zc * zc, axis=-1, keepdims=True)
    return zc * lax.rsqrt(var + LN_EPS) * g + b


def _gelu_tanh(x):
    cdf = 0.5 * (1.0 + jnp.tanh(math.sqrt(2.0 / math.pi) * (x + 0.044715 * (x * x * x))))
    return x * cdf


def _pool_ln_kernel(x_ref, halo_ref, hist_ref, w_ref, scale_ref, g_ref, b_ref, o_ref, ob_ref,
                    ext_ref, z_ref, *, seqs, rows, pos0):
    i = pl.program_id(0)
    halo_rows = (POOL_HIST + 1) * seqs

    @pl.when(i == 0)
    def _():
        ext_ref[0:halo_rows, :] = hist_ref[...]

    @pl.when(i > 0)
    def _():
        ext_ref[0:halo_rows, :] = halo_ref[...]

    ext_ref[halo_rows:halo_rows + rows, :] = x_ref[...]

    row = lax.broadcasted_iota(jnp.int32, (rows, 1), 0) + i * rows
    step = lax.shift_right_logical(row, int(math.log2(seqs)))
    pos_plus_1 = (step + (pos0 + 1)).astype(_F32)

    for g, w in enumerate(POOL_WINDOWS):
        cols = slice(g * POOL_GROUP, (g + 1) * POOL_GROUP)
        cur = x_ref[:, cols]
        win = cur
        for k in range(1, w):
            start = halo_rows - k * seqs
            win = win + ext_ref[start:start + rows, cols]
        cnt = jnp.minimum(float(w), pos_plus_1)
        d = win / cnt - cur
        mix = jnp.dot(d.astype(_BF16), w_ref[g], preferred_element_type=_F32)
        z_ref[:, cols] = ALPHA * cur + mix * scale_ref[:, cols]

    out = _layer_norm(z_ref[...], g_ref[...], b_ref[...])
    o_ref[...] = out
    ob_ref[...] = out.astype(_BF16)


def _pool_ln(x, hist, w_pool, scale, g, b, *, layer, seqs, pos0):
    m = x.shape[0]
    rows = min(ROW_TILE, m)
    halo_rows = (POOL_HIST + 1) * seqs
    assert m % rows == 0 and rows % halo_rows == 0 and seqs & (seqs - 1) == 0
    halo_blocks_per_tile = rows // halo_rows
    kern = functools.partial(_pool_ln_kernel, seqs=seqs, rows=rows, pos0=pos0)
    return pl.pallas_call(
        kern,
        grid=(m // rows,),
        in_specs=[
            pl.BlockSpec((rows, D_MODEL), lambda i: (i, 0)),
            pl.BlockSpec((halo_rows, D_MODEL),
                         lambda i: (jnp.maximum(i * halo_blocks_per_tile - 1, 0), 0)),
            pl.BlockSpec((halo_rows, D_MODEL), lambda i: (0, 0)),
            pl.BlockSpec((None, len(POOL_WINDOWS), POOL_GROUP, POOL_GROUP), lambda i: (layer, 0, 0, 0)),
            pl.BlockSpec((None, 1, D_MODEL), lambda i: (layer, 0, 0)),
            pl.BlockSpec((None, 1, D_MODEL), lambda i: (layer, 0, 0)),
            pl.BlockSpec((None, 1, D_MODEL), lambda i: (layer, 0, 0)),
        ],
        out_specs=[pl.BlockSpec((rows, D_MODEL), lambda i: (i, 0))] * 2,
        out_shape=[jax.ShapeDtypeStruct((m, D_MODEL), _F32), jax.ShapeDtypeStruct((m, D_MODEL), _BF16)],
        scratch_shapes=[pltpu.VMEM((halo_rows + rows, D_MODEL), _F32),
                        pltpu.VMEM((rows, D_MODEL), _F32)],
        compiler_params=_params(1),
        name="pool_ln",
    )(x, x, hist, w_pool, scale, g, b)


def _ffn_ln_kernel(xb_ref, x_ref, hist_a_ref, hist_v_ref, wa_ref, wv_ref, cwa_ref, cwv_ref, cba_ref, cbv_ref,
                   wd_ref, g_ref, b_ref, *refs, seqs, rows, n_items, bf16_copy, cast_next):
    refs = list(refs)
    next_up_ref, next_down_ref = (refs.pop(0), refs.pop(0)) if cast_next else (None, None)
    o_ref = refs.pop(0)
    ob_ref = refs.pop(0) if bf16_copy else None
    new_a_ref, new_v_ref = refs.pop(0), refs.pop(0)
    next_up_b_ref, next_down_b_ref = (refs.pop(0), refs.pop(0)) if cast_next else (None, None)
    ext_a_ref, ext_v_ref = refs
    if cast_next:
        next_up_b_ref[...] = next_up_ref[...].astype(_BF16)
        next_down_b_ref[...] = next_down_ref[...].astype(_BF16)
    s = pl.program_id(0)
    halo = (CONV_W - 1) * seqs
    a_item = jnp.minimum(s, n_items - 1)
    i_a = lax.div(a_item, N_FF_TILES)
    j_a = a_item - i_a * N_FF_TILES
    j_b = lax.rem(jnp.maximum(s - 1, 0), N_FF_TILES)
    parity = s & 1

    @pl.when(s == 0)
    def _():
        ext_a_ref[1] = jnp.zeros(ext_a_ref.shape[1:], _F32)
        ext_v_ref[1] = jnp.zeros(ext_v_ref.shape[1:], _F32)

    n_copies = ext_a_ref.shape[1]
    delays = range(CONV_W)
    if n_copies == 1:
        copy_of, write_at, read_at = [0] * CONV_W, [halo] * CONV_W, [halo - d * seqs for d in delays]
    else:
        copy_of, write_at, read_at = list(delays), [d * seqs for d in delays], [0] * CONV_W

    def fill_history(src_a, src_v):
        for c in range(n_copies):
            n = write_at[copy_of.index(c)]
            if n:
                ext_a_ref[parity, c, 0:n, :] = src_a[halo - n:halo, :]
                ext_v_ref[parity, c, 0:n, :] = src_v[halo - n:halo, :]

    @pl.when(i_a == 0)
    def _():
        fill_history(hist_a_ref[...], hist_v_ref[...])

    @pl.when(i_a > 0)
    def _():
        fill_history(new_a_ref[j_a], new_v_ref[j_a])

    @pl.when(j_b == 0)
    def _():
        o_ref[...] = jnp.zeros_like(o_ref)

    def both_halves(slot):
        xb = xb_ref[...]
        ha = jnp.dot(xb, wa_ref[...], preferred_element_type=_F32)
        hv = jnp.dot(xb, wv_ref[...], preferred_element_type=_F32)
        for c in range(n_copies):
            at = write_at[copy_of.index(c)]
            ext_a_ref[slot, c, at:at + rows, :] = ha
            ext_v_ref[slot, c, at:at + rows, :] = hv
        new_a_ref[j_a] = ha[rows - halo:, :]
        new_v_ref[j_a] = hv[rows - halo:, :]

        def conv(ext_ref, cw_ref, cb_ref):
            cw = cw_ref[...]
            c = cb_ref[...]
            for tap in range(CONV_W):
                d = CONV_W - 1 - tap
                c = c + cw[tap:tap + 1, :] * ext_ref[1 - slot, copy_of[d], read_at[d]:read_at[d] + rows, :]
            return c

        ca = conv(ext_a_ref, cwa_ref, cba_ref)
        cv = conv(ext_v_ref, cwv_ref, cbv_ref)
        act = (_gelu_tanh(ca) * cv).astype(_BF16)
        o_ref[...] += jnp.dot(act, wd_ref[...], preferred_element_type=_F32)

    for slot in (0, 1):
        pl.when(parity == slot)(functools.partial(both_halves, slot))

    @pl.when(jnp.logical_and(s > 0, j_b == N_FF_TILES - 1))
    def _():
        out = _layer_norm(ALPHA * x_ref[...] + o_ref[...], g_ref[...], b_ref[...])
        o_ref[...] = out
        if ob_ref is not None:
            ob_ref[...] = out.astype(_BF16)


def _ffn_ln(xb, x, hist, w_up, conv_w, conv_b, w_down, g, b, *, layer, seqs, bf16_copy, next_weights=None):
    m = x.shape[0]
    rows = min(ROW_TILE, m)
    halo = (CONV_W - 1) * seqs
    ext_rows = -(-(rows + halo) // SUBLANES) * SUBLANES
    n_copies = 1 if seqs % SUBLANES == 0 else CONV_W
    assert m % rows == 0 and rows >= halo
    n_items = (m // rows) * N_FF_TILES
    kern = functools.partial(_ffn_ln_kernel, seqs=seqs, rows=rows, n_items=n_items, bf16_copy=bf16_copy,
                             cast_next=next_weights is not None)

    def up_item(s):
        item = jnp.minimum(s, n_items - 1)
        return lax.div(item, N_FF_TILES), lax.rem(item, N_FF_TILES)

    def down_item(s):
        item = jnp.maximum(s - 1, 0)
        return lax.div(item, N_FF_TILES), lax.rem(item, N_FF_TILES)

    up_a = lambda s: (0, up_item(s)[1])
    up_v = lambda s: (0, up_item(s)[1] + N_FF_TILES)
    conv_w_a = lambda s: (layer, 0, down_item(s)[1])
    conv_w_v = lambda s: (layer, 0, down_item(s)[1] + N_FF_TILES)
    ln_row = lambda s: (layer, 0, 0)
    down_rows = pl.BlockSpec((rows, D_MODEL), lambda s: (down_item(s)[0], 0))
    new_hist = pl.BlockSpec((N_FF_TILES, halo, FF_TILE), lambda s: (0, 0, 0))
    row_out = [jax.ShapeDtypeStruct((m, D_MODEL), _F32)]
    if bf16_copy:
        row_out.append(jax.ShapeDtypeStruct((m, D_MODEL), _BF16))
    cast_in, cast_specs, cast_out = [], [], []
    if next_weights is not None:
        n_row_tiles = m // rows
        up_blk = (D_MODEL // n_row_tiles, 2 * D_FF // N_FF_TILES)
        down_blk = (D_FF // n_items, D_MODEL)
        assert up_blk[0] * n_row_tiles == D_MODEL and down_blk[0] * n_items == D_FF
        assert up_blk[0] % (2 * SUBLANES) == 0 and down_blk[0] % (2 * SUBLANES) == 0
        cast_in = list(next_weights)
        cast_specs = [pl.BlockSpec((None,) + up_blk, lambda s: (layer + 1,) + up_item(s)),
                      pl.BlockSpec((None,) + down_blk, lambda s: (layer + 1, jnp.minimum(s, n_items - 1), 0))]
        cast_out = [(pl.BlockSpec(up_blk, up_item), jax.ShapeDtypeStruct((D_MODEL, 2 * D_FF), _BF16)),
                    (pl.BlockSpec(down_blk, lambda s: (jnp.minimum(s, n_items - 1), 0)),
                     jax.ShapeDtypeStruct((D_FF, D_MODEL), _BF16))]
    return pl.pallas_call(
        kern,
        grid=(n_items + 1,),
        in_specs=[
            pl.BlockSpec((rows, D_MODEL), lambda s: (up_item(s)[0], 0)),
            down_rows,
            pl.BlockSpec((halo, FF_TILE), up_a),
            pl.BlockSpec((halo, FF_TILE), up_v),
            pl.BlockSpec((D_MODEL, FF_TILE), up_a),
            pl.BlockSpec((D_MODEL, FF_TILE), up_v),
            pl.BlockSpec((None, CONV_W, FF_TILE), conv_w_a),
            pl.BlockSpec((None, CONV_W, FF_TILE), conv_w_v),
            pl.BlockSpec((None, 1, FF_TILE), conv_w_a),
            pl.BlockSpec((None, 1, FF_TILE), conv_w_v),
            pl.BlockSpec((FF_TILE, D_MODEL), lambda s: (down_item(s)[1], 0)),
            pl.BlockSpec((None, 1, D_MODEL), ln_row),
            pl.BlockSpec((None, 1, D_MODEL), ln_row),
        ] + cast_specs,
        out_specs=[down_rows] * len(row_out) + [new_hist, new_hist] + [spec for spec, _ in cast_out],
        out_shape=(row_out + [jax.ShapeDtypeStruct((N_FF_TILES, halo, FF_TILE), _F32)] * 2
                   + [shape for _, shape in cast_out]),
        scratch_shapes=[
            pltpu.VMEM((2, n_copies, ext_rows, FF_TILE), _F32),
            pltpu.VMEM((2, n_copies, ext_rows, FF_TILE), _F32),
        ],
        compiler_params=_params(1),
        name="ffn_ln",
    )(xb, x, hist, hist, w_up, w_up, conv_w, conv_w, conv_b, conv_b, w_down, g, b, *cast_in)


def _proj_kernel(x_ref, w_ref, *out_refs, scale):
    r = jnp.dot(x_ref[...], w_ref[...], preferred_element_type=_F32)
    for o_ref in out_refs:
        if o_ref.dtype == _F32:
            for h in range(o_ref.shape[1]):
                o_ref[:, h, :] = r[:, h * HEAD_W:(h + 1) * HEAD_W]
        else:
            o_ref[...] = (r * scale).astype(o_ref.dtype)


def _proj(x, w, out_dtypes, *, n, col0=0, layer=None, scale=1.0):
    m, k = x.shape
    rows = min(2 * ROW_TILE, m)
    cols = min(MM_COL_TILE, n)
    assert m % rows == 0 and n % cols == 0 and col0 % cols == 0 and cols % HEAD_W == 0
    if layer is None:
        w_spec = pl.BlockSpec((k, cols), lambda i, j: (0, j + col0 // cols))
    else:
        w_spec = pl.BlockSpec((None, k, cols), lambda i, j: (layer, 0, j + col0 // cols))
    flat = pl.BlockSpec((rows, cols), lambda i, j: (i, j)), (m, n)
    heads = pl.BlockSpec((rows, cols // HEAD_W, HEAD_W), lambda i, j: (i, j, 0)), (m, n // HEAD_W, HEAD_W)
    outs = [heads if dt == _F32 else flat for dt in out_dtypes]
    return pl.pallas_call(
        functools.partial(_proj_kernel, scale=scale),
        grid=(m // rows, n // cols),
        in_specs=[pl.BlockSpec((rows, k), lambda i, j: (i, 0)), w_spec],
        out_specs=[spec for spec, _ in outs],
        out_shape=[jax.ShapeDtypeStruct(shape, dt) for (_, shape), dt in zip(outs, out_dtypes)],
        compiler_params=_params(2),
        name="proj",
    )(x, w)


def _diff_lambda(lam_ref, lam_init):
    lam = lam_ref[...]
    e1 = jnp.exp(jnp.sum(lam[0:1, :] * lam[1:2, :], axis=-1, keepdims=True))
    e2 = jnp.exp(jnp.sum(lam[2:3, :] * lam[3:4, :], axis=-1, keepdims=True))
    return e1 - e2 + lam_init


def _split_parts(q):
    first = lax.broadcasted_iota(jnp.int32, (1, HEAD_W), 1) < QK_DIM
    zero = jnp.zeros_like(q)
    return jnp.concatenate([jnp.where(first, q, zero), jnp.where(first, zero, q)], axis=0)


def _sub_norm(o, sg_ref, lam_init):
    return o * lax.rsqrt(jnp.mean(o * o, axis=-1, keepdims=True) + LN_EPS) * sg_ref[...] * (1.0 - lam_init)


def _score(qz, kb):
    return lax.dot_general(qz, kb, (((1,), (1,)), ((), ())), preferred_element_type=_F32)


def _flash_kernel(qt_ref, k_ref, vt_ref, bias_ref, lam_ref, sg_ref, o_ref,
                  qz_ref, s_ref, p_ref, a_ref, m_ref, l_ref, acc_ref, *, lam_init, n_q_blocks):
    lam = _diff_lambda(lam_ref, lam_init)
    n_cols = 2 * Q_TILE
    first_part = lax.broadcasted_iota(jnp.int32, (HEAD_W, 1), 0) < QK_DIM

    l_ref[...] = jnp.zeros_like(l_ref)
    acc_ref[...] = jnp.zeros_like(acc_ref)
    p_ref[1] = jnp.zeros(p_ref.shape[1:], _BF16)
    a_ref[1] = jnp.ones(a_ref.shape[1:], _F32)

    def load_queries(qi):
        qt = qt_ref[qi]
        zero = jnp.zeros_like(qt)
        qz_ref[qi & 1, :, 0:Q_TILE] = jnp.where(first_part, qt, zero)
        qz_ref[qi & 1, :, Q_TILE:n_cols] = jnp.where(first_part, zero, qt)

    def scores(qi, b, slot):
        k0 = pl.multiple_of(b * K_TILE, K_TILE)
        s_ref[slot] = jnp.dot(k_ref[pl.ds(k0, K_TILE), :], qz_ref[qi & 1], preferred_element_type=_F32)

    def softmax(slot, bias):
        s = s_ref[slot]
        if bias is not None:
            s = s + bias
        m_prev = m_ref[...]
        m_new = jnp.maximum(m_prev, jnp.max(s, axis=0, keepdims=True))
        a = jnp.exp2(m_prev - m_new)
        p = jnp.exp2(s - m_new)
        l_ref[...] = a * l_ref[...] + jnp.sum(p, axis=0, keepdims=True)
        m_ref[...] = m_new
        a_ref[slot] = a
        p_ref[slot] = p.astype(_BF16)

    def values(b, slot):
        vb = vt_ref[jnp.maximum(b, 0)]
        acc_ref[...] = a_ref[slot] * acc_ref[...] + jnp.dot(vb, p_ref[slot], preferred_element_type=_F32)

    def step(qi, t, slot):
        values(t - 1, 1 - slot)
        softmax(slot, None)
        scores(qi, t + 1, 1 - slot)

    def q_block(qi, carry):
        m_ref[...] = jnp.full(m_ref.shape, NEG_INF, _F32)
        n_full = lax.div(qi, K_TILE // Q_TILE)

        def unrolled(u, c):
            for i in range(STEP_UNROLL):
                step(qi, STEP_UNROLL * u + i, i & 1)
            return c

        def single(t, c):
            step(qi, t, t & 1)
            return c

        n_unrolled = lax.div(n_full, STEP_UNROLL)
        lax.fori_loop(0, n_unrolled, unrolled, 0)
        lax.fori_loop(n_unrolled * STEP_UNROLL, n_full, single, 0)

        last = n_full & 1
        values(n_full - 1, 1 - last)
        softmax(last, bias_ref[qi - n_full * (K_TILE // Q_TILE)])
        nxt = jnp.minimum(qi + 1, n_q_blocks - 1)
        load_queries(nxt)
        scores(nxt, 0, 0)
        values(n_full, last)

        o2 = acc_ref[...] / l_ref[...]
        o = o2[:, 0:Q_TILE] - lam * o2[:, Q_TILE:n_cols]
        norm = lax.rsqrt(jnp.mean(o * o, axis=0, keepdims=True) + LN_EPS)
        o_ref[qi] = (o * norm * sg_ref[...] * (1.0 - lam_init)).astype(o_ref.dtype)
        return carry

    load_queries(0)
    scores(0, 0, 0)
    lax.fori_loop(0, n_q_blocks, q_block, 0)


def _flash_prompt(q, k, v, lam, sg, *, layer, lam_init):
    t = q.shape[0]
    assert t % K_TILE == 0 and K_TILE % Q_TILE == 0 and Q_TILE % CHUNK == 0 and STEP_UNROLL % 2 == 0
    nq, nk = t // Q_TILE, t // K_TILE
    qt = jnp.transpose(q.reshape(nq, Q_TILE, N_HEADS, HEAD_W), (2, 0, 3, 1))
    vt = jnp.transpose(v.reshape(nk, K_TILE, N_HEADS, V_DIM), (2, 0, 3, 1))
    sub = K_TILE // Q_TILE
    key_chunk = (jnp.arange(K_TILE) // CHUNK)[None, :, None]
    query = jnp.arange(sub)[:, None, None] * Q_TILE + (jnp.arange(2 * Q_TILE) % Q_TILE)[None, None, :]
    tail_bias = jnp.where(key_chunk <= query // CHUNK, 0.0, NEG_INF).astype(_F32)
    kern = functools.partial(_flash_kernel, lam_init=lam_init, n_q_blocks=nq)
    out = pl.pallas_call(
        kern,
        grid=(N_HEADS,),
        in_specs=[pl.BlockSpec((None, nq, HEAD_W, Q_TILE), lambda h: (h, 0, 0, 0)),
                  pl.BlockSpec((t, HEAD_W), lambda h: (0, h)),
                  pl.BlockSpec((None, nk, V_DIM, K_TILE), lambda h: (h, 0, 0, 0)),
                  pl.BlockSpec((sub, K_TILE, 2 * Q_TILE), lambda h: (0, 0, 0)),
                  pl.BlockSpec((None, 4, QK_DIM), lambda h: (layer, 0, 0)),
                  pl.BlockSpec((None, V_DIM, 1), lambda h: (layer, 0, 0))],
        out_specs=pl.BlockSpec((None, nq, V_DIM, Q_TILE), lambda h: (h, 0, 0, 0)),
        out_shape=jax.ShapeDtypeStruct((N_HEADS, nq, V_DIM, Q_TILE), _BF16),
        scratch_shapes=[pltpu.VMEM((2, HEAD_W, 2 * Q_TILE), _BF16),
                        pltpu.VMEM((2, K_TILE, 2 * Q_TILE), _F32),
                        pltpu.VMEM((2, K_TILE, 2 * Q_TILE), _BF16),
                        pltpu.VMEM((2, 1, 2 * Q_TILE), _F32),
                        pltpu.VMEM((1, 2 * Q_TILE), _F32),
                        pltpu.VMEM((1, 2 * Q_TILE), _F32),
                        pltpu.VMEM((V_DIM, 2 * Q_TILE), _F32)],
        compiler_params=_params(1),
        name="flash_prompt",
    )(qt, k, vt, tail_bias, lam, sg.reshape(sg.shape[0], V_DIM, 1))
    return jnp.transpose(out, (1, 3, 0, 2)).reshape(t, N_HEADS * V_DIM)


def _sample_attn_kernel(q_ref, kn_ref, vn_ref, kc_ref, vc_ref, lam_ref, sg_ref, o_ref, *, lam_init, steps):
    heads = SAMPLE_HEADS_PER_STEP
    per_head = 2 * steps
    past = kc_ref.shape[1]
    lam = _diff_lambda(lam_ref, lam_init)
    head_cols = [slice(h * HEAD_W, (h + 1) * HEAD_W) for h in range(heads)]
    head_rows = [slice(h * per_head, (h + 1) * per_head) for h in range(heads)]

    qz = jnp.concatenate([_split_parts(q_ref[:, c]) for c in head_cols], axis=0)
    kc = kc_ref[0].reshape(past * heads, HEAD_W).astype(_BF16)
    vc = vc_ref[0].reshape(past * heads, V_DIM).astype(_BF16)
    sc = _score(qz, kc)
    row_head = lax.shift_right_logical(lax.broadcasted_iota(jnp.int32, (heads * per_head, 1), 0),
                                       int(math.log2(per_head)))
    col_head = lax.broadcasted_iota(jnp.int32, (1, past * heads), 1) & (heads - 1)
    sc = jnp.where(row_head == col_head, sc, NEG_INF)
    sn = jnp.concatenate([_score(qz[r, :], kn_ref[:, c]) for r, c in zip(head_rows, head_cols)], axis=0)
    m = jnp.maximum(jnp.max(sc, axis=1, keepdims=True), jnp.max(sn, axis=1, keepdims=True))
    pc = jnp.exp2(sc - m)
    pn = jnp.exp2(sn - m)
    l = jnp.sum(pc, axis=1, keepdims=True) + jnp.sum(pn, axis=1, keepdims=True)
    pn = pn.astype(_BF16)
    on = jnp.concatenate([jnp.dot(pn[r, :], vn_ref[:, c], preferred_element_type=_F32)
                          for r, c in zip(head_rows, head_cols)], axis=0)
    o2 = (jnp.dot(pc.astype(_BF16), vc, preferred_element_type=_F32) + on) / l
    for h in range(heads):
        first = h * per_head
        o = o2[first:first + steps, :] - lam * o2[first + steps:first + per_head, :]
        o_ref[:, head_cols[h]] = _sub_norm(o, sg_ref, lam_init).astype(o_ref.dtype)


def _sample_attn(q, k_new, v_new, cache_k, cache_v, lam, sg, *, layer, lam_init, seqs):
    m, width = q.shape
    steps = m // seqs
    past = cache_k.shape[1]
    group_w = SAMPLE_HEADS_PER_STEP * HEAD_W
    groups = width // group_w
    kern = functools.partial(_sample_attn_kernel, lam_init=lam_init, steps=steps)
    new = pl.BlockSpec((steps, group_w), lambda b, g: (0, b * groups + g))
    cache = pl.BlockSpec((1, past, SAMPLE_HEADS_PER_STEP, HEAD_W), lambda b, g: (b, 0, g, 0))
    out = pl.pallas_call(
        kern,
        grid=(seqs, groups),
        in_specs=[new, new, new, cache, cache,
                  pl.BlockSpec((None, 4, QK_DIM), lambda b, g: (layer, 0, 0)),
                  pl.BlockSpec((None, 1, V_DIM), lambda b, g: (layer, 0, 0))],
        out_specs=new,
        out_shape=jax.ShapeDtypeStruct((steps, seqs * width), _BF16),
        compiler_params=_params(2),
        name="sample_attn",
    )(q.reshape(steps, seqs * width), k_new.reshape(steps, seqs * width),
      v_new.reshape(steps, seqs * width), cache_k, cache_v, lam, sg)
    return out.reshape(m, width)


def _oproj_ln_kernel(x_ref, a_ref, w_ref, g_ref, b_ref, o_ref, ob_ref):
    mix = jnp.dot(a_ref[...], w_ref[...], preferred_element_type=_F32)
    out = _layer_norm(ALPHA * x_ref[...] + mix, g_ref[...], b_ref[...])
    o_ref[...] = out
    ob_ref[...] = out.astype(_BF16)


def _oproj_ln(x, attn, w_o, g, b, *, layer, ln_layer):
    m = x.shape[0]
    rows = min(ROW_TILE, m)
    assert m % rows == 0
    const = lambda i: (ln_layer, 0, 0)
    return pl.pallas_call(
        _oproj_ln_kernel,
        grid=(m // rows,),
        in_specs=[pl.BlockSpec((rows, D_MODEL), lambda i: (i, 0)),
                  pl.BlockSpec((rows, N_HEADS * V_DIM), lambda i: (i, 0)),
                  pl.BlockSpec((None, N_HEADS * V_DIM, D_MODEL), lambda i: (layer, 0, 0)),
                  pl.BlockSpec((None, 1, D_MODEL), const),
                  pl.BlockSpec((None, 1, D_MODEL), const)],
        out_specs=[pl.BlockSpec((rows, D_MODEL), lambda i: (i, 0))] * 2,
        out_shape=[jax.ShapeDtypeStruct((m, D_MODEL), _F32), jax.ShapeDtypeStruct((m, D_MODEL), _BF16)],
        compiler_params=_params(1),
        name="oproj_ln",
    )(x, attn, w_o, g, b)


def _run_group(x, pool_hist, conv_hist, caches, pos0, seqs, p):
    layer_inputs, new_conv = [], []
    k_f32 = v_f32 = k_b = v_b = xb = None
    for l in range(DEPTH):
        if l < N_A_LAYERS:
            layer_inputs.append(x)
            x, xb = _pool_ln(x, pool_hist[l], p['w_pool'], p['pool_scale'], p['ln1_g'], p['ln1_b'],
                             layer=l, seqs=seqs, pos0=pos0)
        else:
            j = l - N_A_LAYERS
            lam_init = 0.8 - 0.6 * math.exp(-0.3 * l)
            (q,) = _proj(xb, p['w_q'], (_BF16,), n=N_HEADS * HEAD_W, layer=j, scale=Q_SCALE)
            if caches is None:
                attn = _flash_prompt(q, k_b, v_b, p['lam'], p['subln_g'], layer=j, lam_init=lam_init)
            else:
                attn = _sample_attn(q, k_b, v_b, caches[0], caches[1], p['lam'], p['subln_g'],
                                    layer=j, lam_init=lam_init, seqs=seqs)
            x, xb = _oproj_ln(x, attn, p['w_o'], p['ln1_g'], p['ln1_b'], layer=j, ln_layer=l)
        bf16_copy = N_A_LAYERS - 1 <= l < DEPTH - 1
        cast_next = l + 1 < DEPTH and len(p['w_up']) == l + 1
        outs = _ffn_ln(xb, x, conv_hist[l], p['w_up'][l], p['conv_w'], p['conv_b'], p['w_down'][l],
                       p['ln2_g'], p['ln2_b'], layer=l, seqs=seqs, bf16_copy=bf16_copy,
                       next_weights=(p['w_up_f32'], p['w_down_f32']) if cast_next else None)
        if cast_next:
            *outs, w_up_next, w_down_next = outs
            p['w_up'].append(w_up_next)
            p['w_down'].append(w_down_next)
        x, *xb, new_a, new_v = outs
        xb = xb[0] if xb else None
        new_conv.append(jnp.concatenate([_untile_cols(new_a), _untile_cols(new_v)], axis=1))
        if l == N_A_LAYERS - 1:
            k_width = N_HEADS * HEAD_W
            k_f32, k_b = _proj(xb, p['w_kv'], (_F32, _BF16), n=k_width)
            v_f32, v_b = _proj(xb, p['w_kv'], (_F32, _BF16), n=N_HEADS * V_DIM, col0=k_width)
    return x, layer_inputs, new_conv, k_f32, v_f32


def _untile_cols(a):
    tiles, rows, w = a.shape
    return jnp.transpose(a, (1, 0, 2)).reshape(rows, tiles * w)


def _to_time_major(a):
    seqs, steps, w = a.shape
    return jnp.transpose(a, (1, 0, 2)).reshape(steps * seqs, w)


def _to_batch_major(a, seqs):
    return jnp.swapaxes(a.reshape(a.shape[0] // seqs, seqs, *a.shape[1:]), 0, 1)


def kernel(x_prompt, x_sample, state_pool, state_ffn_conv, cache_k, cache_v, ln1_g, ln1_b, ln2_g, ln2_b,
           w_pool, pool_scale, w_up, conv_w, conv_b, w_down, w_kv, w_q, lam, subln_g, w_o):
    rows = lambda a: a[:, None, :]
    p = dict(ln1_g=rows(ln1_g), ln1_b=rows(ln1_b), ln2_g=rows(ln2_g), ln2_b=rows(ln2_b),
             pool_scale=rows(pool_scale), conv_w=conv_w, conv_b=rows(conv_b), lam=lam, subln_g=rows(subln_g),
             w_pool=w_pool.astype(_BF16), w_up_f32=w_up, w_down_f32=w_down,
             w_up=[w_up[0].astype(_BF16)], w_down=[w_down[0].astype(_BF16)],
             w_kv=w_kv.astype(_BF16), w_q=w_q.astype(_BF16), w_o=w_o.astype(_BF16))

    def finish(x, layer_inputs, new_conv, k_new, v_new, seqs):
        steps = x.shape[0] // seqs
        y = _to_batch_major(x, seqs)
        pool = jnp.stack([_to_batch_major(u[(steps - POOL_HIST) * seqs:], seqs) for u in layer_inputs])
        conv = jnp.stack([_to_batch_major(c, seqs) for c in new_conv])
        return y, pool, conv, _to_batch_major(k_new, seqs), _to_batch_major(v_new, seqs)

    b, t, _ = x_prompt.shape
    assert b == 1 and t >= POOL_HIST
    zeros_pool = jnp.zeros(((POOL_HIST + 1) * b, D_MODEL), _F32)
    zeros_conv = jnp.zeros(((CONV_W - 1) * b, 2 * D_FF), _F32)
    out_p = _run_group(_to_time_major(x_prompt), [zeros_pool] * N_A_LAYERS, [zeros_conv] * DEPTH,
                       None, 0, b, p)
    y_p, pool_p, conv_p, k_p, v_p = finish(*out_p, b)

    sb, steps, _ = x_sample.shape
    past = cache_k.shape[1]
    assert steps >= POOL_HIST
    pad = jnp.zeros((sb, D_MODEL), _F32)
    pool_hist = [jnp.concatenate([pad, _to_time_major(state_pool[l])], axis=0) for l in range(N_A_LAYERS)]
    conv_hist = [_to_time_major(state_ffn_conv[l]) for l in range(DEPTH)]
    out_s = _run_group(_to_time_major(x_sample), pool_hist, conv_hist, (cache_k, cache_v), past, sb, p)
    y_s, pool_s, conv_s, k_s, v_s = finish(*out_s, sb)

    return (y_p, y_s, pool_p, pool_s, conv_p, conv_s, k_p, v_p, k_s, v_s)
```

```python
import functools
import math

import jax
import jax.numpy as jnp
from jax import lax
from jax.experimental import pallas as pl
from jax.experimental.pallas import tpu as pltpu

D_MODEL = 2048
DEPTH = 4
N_A_LAYERS = DEPTH // 2
CHUNK = 64
POOL_WINDOWS = (2, 4, 8, 16)
POOL_GROUP = D_MODEL // len(POOL_WINDOWS)
POOL_HIST = max(POOL_WINDOWS) - 1
N_HEADS = 16
QK_DIM = 64
V_DIM = 2 * QK_DIM
HEAD_W = 2 * QK_DIM
D_FF = 5632
CONV_W = 3
LN_EPS = 1e-5
ALPHA = (2 * DEPTH) ** 0.25
ATTN_SCALE = QK_DIM ** -0.5
Q_SCALE = ATTN_SCALE * math.log2(math.e)
NEG_INF = -1e30

V7X_VMEM_BYTES = 64 * 1024 * 1024
VMEM_LIMIT_BYTES = V7X_VMEM_BYTES - 6 * 1024 * 1024
SUBLANES = 8

ROW_TILE = 512
FF_TILE = 512
CAST_BLOCK_ELEMS = 64 * 1024
N_FF_TILES = D_FF // FF_TILE
MM_COL_TILE = 1024
Q_TILE = 256
K_TILE = 512
STEP_UNROLL = 4
SAMPLE_HEADS_PER_STEP = 8

_F32 = jnp.float32
_BF16 = jnp.bfloat16


def _params(n_axes):
    return pltpu.CompilerParams(dimension_semantics=("arbitrary",) * n_axes,
                                vmem_limit_bytes=VMEM_LIMIT_BYTES)


def _layer_norm(z, g, b):
    mu = jnp.mean(z, axis=-1, keepdims=True)
    zc = z - mu
    var = jnp.mean(zc * zc, axis=-1, keepdims=True)
    return zc * lax.rsqrt(var + LN_EPS) * g + b


def _gelu_tanh(x):
    cdf = 0.5 * (1.0 + jnp.tanh(math.sqrt(2.0 / math.pi) * (x + 0.044715 * (x * x * x))))
    return x * cdf


def _pool_ln_kernel(x_ref, halo_ref, hist_ref, w_ref, scale_ref, g_ref, b_ref, o_ref, ob_ref,
                    ext_ref, z_ref, *, seqs, rows, pos0):
    i = pl.program_id(0)
    halo_rows = (POOL_HIST + 1) * seqs

    @pl.when(i == 0)
    def _():
        ext_ref[0:halo_rows, :] = hist_ref[...]

    @pl.when(i > 0)
    def _():
        ext_ref[0:halo_rows, :] = halo_ref[...]

    ext_ref[halo_rows:halo_rows + rows, :] = x_ref[...]

    row = lax.broadcasted_iota(jnp.int32, (rows, 1), 0) + i * rows
    step = lax.shift_right_logical(row, int(math.log2(seqs)))
    pos_plus_1 = (step + (pos0 + 1)).astype(_F32)

    for g, w in enumerate(POOL_WINDOWS):
        cols = slice(g * POOL_GROUP, (g + 1) * POOL_GROUP)
        cur = x_ref[:, cols]
        win = cur
        for k in range(1, w):
            start = halo_rows - k * seqs
            win = win + ext_ref[start:start + rows, cols]
        cnt = jnp.minimum(float(w), pos_plus_1)
        d = win / cnt - cur
        mix = jnp.dot(d.astype(_BF16), w_ref[g], preferred_element_type=_F32)
        z_ref[:, cols] = ALPHA * cur + mix * scale_ref[:, cols]

    out = _layer_norm(z_ref[...], g_ref[...], b_ref[...])
    o_ref[...] = out
    ob_ref[...] = out.astype(_BF16)


def _pool_ln(x, hist, w_pool, scale, g, b, *, layer, seqs, pos0):
    m = x.shape[0]
    rows = min(ROW_TILE, m)
    halo_rows = (POOL_HIST + 1) * seqs
    assert m % rows == 0 and rows % halo_rows == 0 and seqs & (seqs - 1) == 0
    halo_blocks_per_tile = rows // halo_rows
    kern = functools.partial(_pool_ln_kernel, seqs=seqs, rows=rows, pos0=pos0)
    return pl.pallas_call(
        kern,
        grid=(m // rows,),
        in_specs=[
            pl.BlockSpec((rows, D_MODEL), lambda i: (i, 0)),
            pl.BlockSpec((halo_rows, D_MODEL),
                         lambda i: (jnp.maximum(i * halo_blocks_per_tile - 1, 0), 0)),
            pl.BlockSpec((halo_rows, D_MODEL), lambda i: (0, 0)),
            pl.BlockSpec((None, len(POOL_WINDOWS), POOL_GROUP, POOL_GROUP), lambda i: (layer, 0, 0, 0)),
            pl.BlockSpec((None, 1, D_MODEL), lambda i: (layer, 0, 0)),
            pl.BlockSpec((None, 1, D_MODEL), lambda i: (layer, 0, 0)),
            pl.BlockSpec((None, 1, D_MODEL), lambda i: (layer, 0, 0)),
        ],
        out_specs=[pl.BlockSpec((rows, D_MODEL), lambda i: (i, 0))] * 2,
        out_shape=[jax.ShapeDtypeStruct((m, D_MODEL), _F32), jax.ShapeDtypeStruct((m, D_MODEL), _BF16)],
        scratch_shapes=[pltpu.VMEM((halo_rows + rows, D_MODEL), _F32),
                        pltpu.VMEM((rows, D_MODEL), _F32)],
        compiler_params=_params(1),
        name="pool_ln",
    )(x, x, hist, w_pool, scale, g, b)


def _ffn_ln_kernel(xb_ref, x_ref, hist_a_ref, hist_v_ref, wa_ref, wv_ref, cwa_ref, cwv_ref, cba_ref, cbv_ref,
                   wd_ref, g_ref, b_ref, *refs, seqs, rows, n_items, bf16_copy, cast_next, side_blocks):
    refs = list(refs)
    s = pl.program_id(0)
    next_up_ref, next_down_ref = (refs.pop(0), refs.pop(0)) if cast_next else (None, None)
    side_in = [refs.pop(0) for _ in side_blocks]
    o_ref = refs.pop(0)
    ob_ref = refs.pop(0) if bf16_copy else None
    new_a_ref, new_v_ref = refs.pop(0), refs.pop(0)
    next_up_b_ref, next_down_b_ref = (refs.pop(0), refs.pop(0)) if cast_next else (None, None)
    side_out = [refs.pop(0) for _ in side_blocks]
    ext_a_ref, ext_v_ref = refs
    if cast_next:
        next_up_b_ref[...] = next_up_ref[...].astype(_BF16)
        next_down_b_ref[...] = next_down_ref[...].astype(_BF16)
    for src_ref, dst_ref, n_blocks in zip(side_in, side_out, side_blocks):
        @pl.when(s < n_blocks)
        def _(src_ref=src_ref, dst_ref=dst_ref):
            dst_ref[...] = src_ref[...].astype(_BF16)

    halo = (CONV_W - 1) * seqs
    a_item = jnp.minimum(s, n_items - 1)
    i_a = lax.div(a_item, N_FF_TILES)
    j_a = a_item - i_a * N_FF_TILES
    j_b = lax.rem(jnp.maximum(s - 1, 0), N_FF_TILES)
    parity = s & 1

    @pl.when(s == 0)
    def _():
        ext_a_ref[1] = jnp.zeros(ext_a_ref.shape[1:], _F32)
        ext_v_ref[1] = jnp.zeros(ext_v_ref.shape[1:], _F32)

    n_copies = ext_a_ref.shape[1]
    delays = range(CONV_W)
    if n_copies == 1:
        copy_of, write_at, read_at = [0] * CONV_W, [halo] * CONV_W, [halo - d * seqs for d in delays]
    else:
        copy_of, write_at, read_at = list(delays), [d * seqs for d in delays], [0] * CONV_W

    def fill_history(src_a, src_v):
        for c in range(n_copies):
            n = write_at[copy_of.index(c)]
            if n:
                ext_a_ref[parity, c, 0:n, :] = src_a[halo - n:halo, :]
                ext_v_ref[parity, c, 0:n, :] = src_v[halo - n:halo, :]

    @pl.when(i_a == 0)
    def _():
        fill_history(hist_a_ref[...], hist_v_ref[...])

    @pl.when(i_a > 0)
    def _():
        fill_history(new_a_ref[j_a], new_v_ref[j_a])

    @pl.when(j_b == 0)
    def _():
        o_ref[...] = jnp.zeros_like(o_ref)

    def both_halves(slot):
        xb = xb_ref[...]
        ha = jnp.dot(xb, wa_ref[...], preferred_element_type=_F32)
        hv = jnp.dot(xb, wv_ref[...], preferred_element_type=_F32)
        for c in range(n_copies):
            at = write_at[copy_of.index(c)]
            ext_a_ref[slot, c, at:at + rows, :] = ha
            ext_v_ref[slot, c, at:at + rows, :] = hv
        new_a_ref[j_a] = ha[rows - halo:, :]
        new_v_ref[j_a] = hv[rows - halo:, :]

        def conv(ext_ref, cw_ref, cb_ref):
            cw = cw_ref[...]
            c = cb_ref[...]
            for tap in range(CONV_W):
                d = CONV_W - 1 - tap
                c = c + cw[tap:tap + 1, :] * ext_ref[1 - slot, copy_of[d], read_at[d]:read_at[d] + rows, :]
            return c

        ca = conv(ext_a_ref, cwa_ref, cba_ref)
        cv = conv(ext_v_ref, cwv_ref, cbv_ref)
        act = (_gelu_tanh(ca) * cv).astype(_BF16)
        o_ref[...] += jnp.dot(act, wd_ref[...], preferred_element_type=_F32)

    for slot in (0, 1):
        pl.when(parity == slot)(functools.partial(both_halves, slot))

    @pl.when(jnp.logical_and(s > 0, j_b == N_FF_TILES - 1))
    def _():
        out = _layer_norm(ALPHA * x_ref[...] + o_ref[...], g_ref[...], b_ref[...])
        o_ref[...] = out
        if ob_ref is not None:
            ob_ref[...] = out.astype(_BF16)


def _ffn_ln(xb, x, hist, w_up, conv_w, conv_b, w_down, g, b, *, layer, seqs, bf16_copy, next_weights=None,
            side_casts=()):
    m = x.shape[0]
    rows = min(ROW_TILE, m)
    halo = (CONV_W - 1) * seqs
    ext_rows = -(-(rows + halo) // SUBLANES) * SUBLANES
    n_copies = 1 if seqs % SUBLANES == 0 else CONV_W
    assert m % rows == 0 and rows >= halo
    n_items = (m // rows) * N_FF_TILES
    kern = functools.partial(_ffn_ln_kernel, seqs=seqs, rows=rows, n_items=n_items, bf16_copy=bf16_copy,
                             cast_next=next_weights is not None, side_blocks=tuple(n for _, _, n in side_casts))

    def up_item(s):
        item = jnp.minimum(s, n_items - 1)
        return lax.div(item, N_FF_TILES), lax.rem(item, N_FF_TILES)

    def down_item(s):
        item = jnp.maximum(s - 1, 0)
        return lax.div(item, N_FF_TILES), lax.rem(item, N_FF_TILES)

    up_a = lambda s: (0, up_item(s)[1])
    up_v = lambda s: (0, up_item(s)[1] + N_FF_TILES)
    conv_w_a = lambda s: (layer, 0, down_item(s)[1])
    conv_w_v = lambda s: (layer, 0, down_item(s)[1] + N_FF_TILES)
    ln_row = lambda s: (layer, 0, 0)
    down_rows = pl.BlockSpec((rows, D_MODEL), lambda s: (down_item(s)[0], 0))
    new_hist = pl.BlockSpec((N_FF_TILES, halo, FF_TILE), lambda s: (0, 0, 0))
    row_out = [jax.ShapeDtypeStruct((m, D_MODEL), _F32)]
    if bf16_copy:
        row_out.append(jax.ShapeDtypeStruct((m, D_MODEL), _BF16))
    cast_in, cast_specs, cast_out = [], [], []
    if next_weights is not None:
        n_row_tiles = m // rows
        up_blk = (D_MODEL // n_row_tiles, 2 * D_FF // N_FF_TILES)
        down_blk = (D_FF // n_items, D_MODEL)
        assert up_blk[0] * n_row_tiles == D_MODEL and down_blk[0] * n_items == D_FF
        assert up_blk[0] % (2 * SUBLANES) == 0 and down_blk[0] % (2 * SUBLANES) == 0
        cast_in = list(next_weights)
        cast_specs = [pl.BlockSpec((None,) + up_blk, lambda s: (layer + 1,) + up_item(s)),
                      pl.BlockSpec((None,) + down_blk, lambda s: (layer + 1, jnp.minimum(s, n_items - 1), 0))]
        cast_out = [(pl.BlockSpec(up_blk, up_item), jax.ShapeDtypeStruct((D_MODEL, 2 * D_FF), _BF16)),
                    (pl.BlockSpec(down_blk, lambda s: (jnp.minimum(s, n_items - 1), 0)),
                     jax.ShapeDtypeStruct((D_FF, D_MODEL), _BF16))]
    for blocks, first, count in side_casts:
        assert count <= n_items
        blk = (None,) + blocks.shape[1:]
        cast_in.append(blocks)
        cast_specs.append(pl.BlockSpec(blk, lambda s, first=first, count=count: (first + jnp.minimum(s, count - 1), 0, 0)))
        cast_out.append((pl.BlockSpec(blk, lambda s, count=count: (jnp.minimum(s, count - 1), 0, 0)),
                         jax.ShapeDtypeStruct((count,) + blocks.shape[1:], _BF16)))
    return pl.pallas_call(
        kern,
        grid=(n_items + 1,),
        in_specs=[
            pl.BlockSpec((rows, D_MODEL), lambda s: (up_item(s)[0], 0)),
            down_rows,
            pl.BlockSpec((halo, FF_TILE), up_a),
            pl.BlockSpec((halo, FF_TILE), up_v),
            pl.BlockSpec((D_MODEL, FF_TILE), up_a),
            pl.BlockSpec((D_MODEL, FF_TILE), up_v),
            pl.BlockSpec((None, CONV_W, FF_TILE), conv_w_a),
            pl.BlockSpec((None, CONV_W, FF_TILE), conv_w_v),
            pl.BlockSpec((None, 1, FF_TILE), conv_w_a),
            pl.BlockSpec((None, 1, FF_TILE), conv_w_v),
            pl.BlockSpec((FF_TILE, D_MODEL), lambda s: (down_item(s)[1], 0)),
            pl.BlockSpec((None, 1, D_MODEL), ln_row),
            pl.BlockSpec((None, 1, D_MODEL), ln_row),
        ] + cast_specs,
        out_specs=[down_rows] * len(row_out) + [new_hist, new_hist] + [spec for spec, _ in cast_out],
        out_shape=(row_out + [jax.ShapeDtypeStruct((N_FF_TILES, halo, FF_TILE), _F32)] * 2
                   + [shape for _, shape in cast_out]),
        scratch_shapes=[
            pltpu.VMEM((2, n_copies, ext_rows, FF_TILE), _F32),
            pltpu.VMEM((2, n_copies, ext_rows, FF_TILE), _F32),
        ],
        compiler_params=_params(1),
        name="ffn_ln",
    )(xb, x, hist, hist, w_up, w_up, conv_w, conv_w, conv_b, conv_b, w_down, g, b, *cast_in)


def _proj_kernel(x_ref, w_ref, *out_refs, scale):
    r = jnp.dot(x_ref[...], w_ref[...], preferred_element_type=_F32)
    for o_ref in out_refs:
        if o_ref.dtype == _F32:
            for h in range(o_ref.shape[1]):
                o_ref[:, h, :] = r[:, h * HEAD_W:(h + 1) * HEAD_W]
        else:
            o_ref[...] = (r * scale).astype(o_ref.dtype)


def _proj(x, w, out_dtypes, *, n, col0=0, layer=None, scale=1.0):
    m, k = x.shape
    rows = min(2 * ROW_TILE, m)
    cols = min(MM_COL_TILE, n)
    assert m % rows == 0 and n % cols == 0 and col0 % cols == 0 and cols % HEAD_W == 0
    if layer is None:
        w_spec = pl.BlockSpec((k, cols), lambda i, j: (0, j + col0 // cols))
    else:
        w_spec = pl.BlockSpec((None, k, cols), lambda i, j: (layer, 0, j + col0 // cols))
    flat = pl.BlockSpec((rows, cols), lambda i, j: (i, j)), (m, n)
    heads = pl.BlockSpec((rows, cols // HEAD_W, HEAD_W), lambda i, j: (i, j, 0)), (m, n // HEAD_W, HEAD_W)
    outs = [heads if dt == _F32 else flat for dt in out_dtypes]
    return pl.pallas_call(
        functools.partial(_proj_kernel, scale=scale),
        grid=(m // rows, n // cols),
        in_specs=[pl.BlockSpec((rows, k), lambda i, j: (i, 0)), w_spec],
        out_specs=[spec for spec, _ in outs],
        out_shape=[jax.ShapeDtypeStruct(shape, dt) for (_, shape), dt in zip(outs, out_dtypes)],
        compiler_params=_params(2),
        name="proj",
    )(x, w)


def _diff_lambda(lam_ref, lam_init):
    lam = lam_ref[...]
    e1 = jnp.exp(jnp.sum(lam[0:1, :] * lam[1:2, :], axis=-1, keepdims=True))
    e2 = jnp.exp(jnp.sum(lam[2:3, :] * lam[3:4, :], axis=-1, keepdims=True))
    return e1 - e2 + lam_init


def _split_parts(q):
    first = lax.broadcasted_iota(jnp.int32, (1, HEAD_W), 1) < QK_DIM
    zero = jnp.zeros_like(q)
    return jnp.concatenate([jnp.where(first, q, zero), jnp.where(first, zero, q)], axis=0)


def _sub_norm(o, sg_ref, lam_init):
    return o * lax.rsqrt(jnp.mean(o * o, axis=-1, keepdims=True) + LN_EPS) * sg_ref[...] * (1.0 - lam_init)


def _score(qz, kb):
    return lax.dot_general(qz, kb, (((1,), (1,)), ((), ())), preferred_element_type=_F32)


def _flash_kernel(qt_ref, k_ref, vt_ref, bias_ref, lam_ref, sg_ref, o_ref,
                  qz_ref, s_ref, p_ref, a_ref, m_ref, l_ref, acc_ref, *, lam_init, n_q_blocks):
    lam = _diff_lambda(lam_ref, lam_init)
    n_cols = 2 * Q_TILE
    first_part = lax.broadcasted_iota(jnp.int32, (HEAD_W, 1), 0) < QK_DIM

    l_ref[...] = jnp.zeros_like(l_ref)
    acc_ref[...] = jnp.zeros_like(acc_ref)
    p_ref[1] = jnp.zeros(p_ref.shape[1:], _BF16)
    a_ref[1] = jnp.ones(a_ref.shape[1:], _F32)

    def load_queries(qi):
        qt = qt_ref[qi]
        zero = jnp.zeros_like(qt)
        qz_ref[qi & 1, :, 0:Q_TILE] = jnp.where(first_part, qt, zero)
        qz_ref[qi & 1, :, Q_TILE:n_cols] = jnp.where(first_part, zero, qt)

    def scores(qi, b, slot):
        k0 = pl.multiple_of(b * K_TILE, K_TILE)
        s_ref[slot] = jnp.dot(k_ref[pl.ds(k0, K_TILE), :], qz_ref[qi & 1], preferred_element_type=_F32)

    def softmax(slot, bias):
        s = s_ref[slot]
        if bias is not None:
            s = s + bias
        m_prev = m_ref[...]
        m_new = jnp.maximum(m_prev, jnp.max(s, axis=0, keepdims=True))
        a = jnp.exp2(m_prev - m_new)
        p = jnp.exp2(s - m_new)
        l_ref[...] = a * l_ref[...] + jnp.sum(p, axis=0, keepdims=True)
        m_ref[...] = m_new
        a_ref[slot] = a
        p_ref[slot] = p.astype(_BF16)

    def values(b, slot):
        vb = vt_ref[jnp.maximum(b, 0)]
        acc_ref[...] = a_ref[slot] * acc_ref[...] + jnp.dot(vb, p_ref[slot], preferred_element_type=_F32)

    def step(qi, t, slot):
        values(t - 1, 1 - slot)
        softmax(slot, None)
        scores(qi, t + 1, 1 - slot)

    def q_block(qi, carry):
        m_ref[...] = jnp.full(m_ref.shape, NEG_INF, _F32)
        n_full = lax.div(qi, K_TILE // Q_TILE)

        def unrolled(u, c):
            for i in range(STEP_UNROLL):
                step(qi, STEP_UNROLL * u + i, i & 1)
            return c

        def single(t, c):
            step(qi, t, t & 1)
            return c

        n_unrolled = lax.div(n_full, STEP_UNROLL)
        lax.fori_loop(0, n_unrolled, unrolled, 0)
        lax.fori_loop(n_unrolled * STEP_UNROLL, n_full, single, 0)

        last = n_full & 1
        values(n_full - 1, 1 - last)
        softmax(last, bias_ref[qi - n_full * (K_TILE // Q_TILE)])
        nxt = jnp.minimum(qi + 1, n_q_blocks - 1)
        load_queries(nxt)
        scores(nxt, 0, 0)
        values(n_full, last)

        o2 = acc_ref[...] / l_ref[...]
        o = o2[:, 0:Q_TILE] - lam * o2[:, Q_TILE:n_cols]
        norm = lax.rsqrt(jnp.mean(o * o, axis=0, keepdims=True) + LN_EPS)
        o_ref[qi] = (o * norm * sg_ref[...] * (1.0 - lam_init)).astype(o_ref.dtype)
        return carry

    load_queries(0)
    scores(0, 0, 0)
    lax.fori_loop(0, n_q_blocks, q_block, 0)


def _flash_prompt(q, k, v, lam, sg, *, layer, lam_init):
    t = q.shape[0]
    assert t % K_TILE == 0 and K_TILE % Q_TILE == 0 and Q_TILE % CHUNK == 0 and STEP_UNROLL % 2 == 0
    nq, nk = t // Q_TILE, t // K_TILE
    qt = jnp.transpose(q.reshape(nq, Q_TILE, N_HEADS, HEAD_W), (2, 0, 3, 1))
    vt = jnp.transpose(v.reshape(nk, K_TILE, N_HEADS, V_DIM), (2, 0, 3, 1))
    sub = K_TILE // Q_TILE
    key_chunk = (jnp.arange(K_TILE) // CHUNK)[None, :, None]
    query = jnp.arange(sub)[:, None, None] * Q_TILE + (jnp.arange(2 * Q_TILE) % Q_TILE)[None, None, :]
    tail_bias = jnp.where(key_chunk <= query // CHUNK, 0.0, NEG_INF).astype(_F32)
    kern = functools.partial(_flash_kernel, lam_init=lam_init, n_q_blocks=nq)
    out = pl.pallas_call(
        kern,
        grid=(N_HEADS,),
        in_specs=[pl.BlockSpec((None, nq, HEAD_W, Q_TILE), lambda h: (h, 0, 0, 0)),
                  pl.BlockSpec((t, HEAD_W), lambda h: (0, h)),
                  pl.BlockSpec((None, nk, V_DIM, K_TILE), lambda h: (h, 0, 0, 0)),
                  pl.BlockSpec((sub, K_TILE, 2 * Q_TILE), lambda h: (0, 0, 0)),
                  pl.BlockSpec((None, 4, QK_DIM), lambda h: (layer, 0, 0)),
                  pl.BlockSpec((None, V_DIM, 1), lambda h: (layer, 0, 0))],
        out_specs=pl.BlockSpec((None, nq, V_DIM, Q_TILE), lambda h: (h, 0, 0, 0)),
        out_shape=jax.ShapeDtypeStruct((N_HEADS, nq, V_DIM, Q_TILE), _BF16),
        scratch_shapes=[pltpu.VMEM((2, HEAD_W, 2 * Q_TILE), _BF16),
                        pltpu.VMEM((2, K_TILE, 2 * Q_TILE), _F32),
                        pltpu.VMEM((2, K_TILE, 2 * Q_TILE), _BF16),
                        pltpu.VMEM((2, 1, 2 * Q_TILE), _F32),
                        pltpu.VMEM((1, 2 * Q_TILE), _F32),
                        pltpu.VMEM((1, 2 * Q_TILE), _F32),
                        pltpu.VMEM((V_DIM, 2 * Q_TILE), _F32)],
        compiler_params=_params(1),
        name="flash_prompt",
    )(qt, k, vt, tail_bias, lam, sg.reshape(sg.shape[0], V_DIM, 1))
    return jnp.transpose(out, (1, 3, 0, 2)).reshape(t, N_HEADS * V_DIM)


def _sample_attn_kernel(q_ref, kn_ref, vn_ref, kc_ref, vc_ref, lam_ref, sg_ref, o_ref, *, lam_init, steps):
    heads = SAMPLE_HEADS_PER_STEP
    per_head = 2 * steps
    past = kc_ref.shape[1]
    lam = _diff_lambda(lam_ref, lam_init)
    head_cols = [slice(h * HEAD_W, (h + 1) * HEAD_W) for h in range(heads)]
    head_rows = [slice(h * per_head, (h + 1) * per_head) for h in range(heads)]

    qz = jnp.concatenate([_split_parts(q_ref[:, c]) for c in head_cols], axis=0)
    kc = kc_ref[0].reshape(past * heads, HEAD_W).astype(_BF16)
    vc = vc_ref[0].reshape(past * heads, V_DIM).astype(_BF16)
    sc = _score(qz, kc)
    row_head = lax.shift_right_logical(lax.broadcasted_iota(jnp.int32, (heads * per_head, 1), 0),
                                       int(math.log2(per_head)))
    col_head = lax.broadcasted_iota(jnp.int32, (1, past * heads), 1) & (heads - 1)
    sc = jnp.where(row_head == col_head, sc, NEG_INF)
    sn = jnp.concatenate([_score(qz[r, :], kn_ref[:, c]) for r, c in zip(head_rows, head_cols)], axis=0)
    m = jnp.maximum(jnp.max(sc, axis=1, keepdims=True), jnp.max(sn, axis=1, keepdims=True))
    pc = jnp.exp2(sc - m)
    pn = jnp.exp2(sn - m)
    l = jnp.sum(pc, axis=1, keepdims=True) + jnp.sum(pn, axis=1, keepdims=True)
    pn = pn.astype(_BF16)
    on = jnp.concatenate([jnp.dot(pn[r, :], vn_ref[:, c], preferred_element_type=_F32)
                          for r, c in zip(head_rows, head_cols)], axis=0)
    o2 = (jnp.dot(pc.astype(_BF16), vc, preferred_element_type=_F32) + on) / l
    for h in range(heads):
        first = h * per_head
        o = o2[first:first + steps, :] - lam * o2[first + steps:first + per_head, :]
        o_ref[:, head_cols[h]] = _sub_norm(o, sg_ref, lam_init).astype(o_ref.dtype)


def _sample_attn(q, k_new, v_new, cache_k, cache_v, lam, sg, *, layer, lam_init, seqs):
    m, width = q.shape
    steps = m // seqs
    past = cache_k.shape[1]
    group_w = SAMPLE_HEADS_PER_STEP * HEAD_W
    groups = width // group_w
    kern = functools.partial(_sample_attn_kernel, lam_init=lam_init, steps=steps)
    new = pl.BlockSpec((steps, group_w), lambda b, g: (0, b * groups + g))
    cache = pl.BlockSpec((1, past, SAMPLE_HEADS_PER_STEP, HEAD_W), lambda b, g: (b, 0, g, 0))
    out = pl.pallas_call(
        kern,
        grid=(seqs, groups),
        in_specs=[new, new, new, cache, cache,
                  pl.BlockSpec((None, 4, QK_DIM), lambda b, g: (layer, 0, 0)),
                  pl.BlockSpec((None, 1, V_DIM), lambda b, g: (layer, 0, 0))],
        out_specs=new,
        out_shape=jax.ShapeDtypeStruct((steps, seqs * width), _BF16),
        compiler_params=_params(2),
        name="sample_attn",
    )(q.reshape(steps, seqs * width), k_new.reshape(steps, seqs * width),
      v_new.reshape(steps, seqs * width), cache_k, cache_v, lam, sg)
    return out.reshape(m, width)


def _oproj_ln_kernel(x_ref, a_ref, w_ref, g_ref, b_ref, o_ref, ob_ref):
    mix = jnp.dot(a_ref[...], w_ref[...], preferred_element_type=_F32)
    out = _layer_norm(ALPHA * x_ref[...] + mix, g_ref[...], b_ref[...])
    o_ref[...] = out
    ob_ref[...] = out.astype(_BF16)


def _oproj_ln(x, attn, w_o, g, b, *, ln_layer):
    m = x.shape[0]
    rows = min(ROW_TILE, m)
    assert m % rows == 0
    const = lambda i: (ln_layer, 0, 0)
    return pl.pallas_call(
        _oproj_ln_kernel,
        grid=(m // rows,),
        in_specs=[pl.BlockSpec((rows, D_MODEL), lambda i: (i, 0)),
                  pl.BlockSpec((rows, N_HEADS * V_DIM), lambda i: (i, 0)),
                  pl.BlockSpec((N_HEADS * V_DIM, D_MODEL), lambda i: (0, 0)),
                  pl.BlockSpec((None, 1, D_MODEL), const),
                  pl.BlockSpec((None, 1, D_MODEL), const)],
        out_specs=[pl.BlockSpec((rows, D_MODEL), lambda i: (i, 0))] * 2,
        out_shape=[jax.ShapeDtypeStruct((m, D_MODEL), _F32), jax.ShapeDtypeStruct((m, D_MODEL), _BF16)],
        compiler_params=_params(1),
        name="oproj_ln",
    )(x, attn, w_o, g, b)


def _run_group(x, pool_hist, conv_hist, caches, pos0, seqs, p):
    layer_inputs, new_conv = [], []
    k_f32 = v_f32 = k_b = v_b = xb = None
    for l in range(DEPTH):
        if l < N_A_LAYERS:
            layer_inputs.append(x)
            x, xb = _pool_ln(x, pool_hist[l], p['w_pool'], p['pool_scale'], p['ln1_g'], p['ln1_b'],
                             layer=l, seqs=seqs, pos0=pos0)
        else:
            j = l - N_A_LAYERS
            lam_init = 0.8 - 0.6 * math.exp(-0.3 * l)
            (q,) = _proj(xb, p['w_q%d' % j], (_BF16,), n=N_HEADS * HEAD_W, scale=Q_SCALE)
            if caches is None:
                attn = _flash_prompt(q, k_b, v_b, p['lam'], p['subln_g'], layer=j, lam_init=lam_init)
            else:
                attn = _sample_attn(q, k_b, v_b, caches[0], caches[1], p['lam'], p['subln_g'],
                                    layer=j, lam_init=lam_init, seqs=seqs)
            x, xb = _oproj_ln(x, attn, p['w_o%d' % j], p['ln1_g'], p['ln1_b'], ln_layer=l)
        bf16_copy = N_A_LAYERS - 1 <= l < DEPTH - 1
        cast_next = l + 1 < DEPTH and len(p['w_up']) == l + 1
        side = [(name, spec) for name, spec in p['to_cast'].get(l, []) if name not in p]
        outs = _ffn_ln(xb, x, conv_hist[l], p['w_up'][l], p['conv_w'], p['conv_b'], p['w_down'][l],
                       p['ln2_g'], p['ln2_b'], layer=l, seqs=seqs, bf16_copy=bf16_copy,
                       next_weights=(p['w_up_f32'], p['w_down_f32']) if cast_next else None,
                       side_casts=[spec for _, spec in side])
        for (name, (blocks, _, _)), cast in zip(side, outs[len(outs) - len(side):]):
            p[name] = cast.reshape(-1, blocks.shape[-1])
        outs = outs[:len(outs) - len(side)]
        if cast_next:
            *outs, w_up_next, w_down_next = outs
            p['w_up'].append(w_up_next)
            p['w_down'].append(w_down_next)
        x, *xb, new_a, new_v = outs
        xb = xb[0] if xb else None
        new_conv.append(jnp.concatenate([_untile_cols(new_a), _untile_cols(new_v)], axis=1))
        if l == N_A_LAYERS - 1:
            k_width = N_HEADS * HEAD_W
            k_f32, k_b = _proj(xb, p['w_kv'], (_F32, _BF16), n=k_width)
            v_f32, v_b = _proj(xb, p['w_kv'], (_F32, _BF16), n=N_HEADS * V_DIM, col0=k_width)
    return x, layer_inputs, new_conv, k_f32, v_f32


def _untile_cols(a):
    tiles, rows, w = a.shape
    return jnp.transpose(a, (1, 0, 2)).reshape(rows, tiles * w)


def _to_time_major(a):
    seqs, steps, w = a.shape
    return jnp.transpose(a, (1, 0, 2)).reshape(steps * seqs, w)


def _to_batch_major(a, seqs):
    return jnp.swapaxes(a.reshape(a.shape[0] // seqs, seqs, *a.shape[1:]), 0, 1)


def kernel(x_prompt, x_sample, state_pool, state_ffn_conv, cache_k, cache_v, ln1_g, ln1_b, ln2_g, ln2_b,
           w_pool, pool_scale, w_up, conv_w, conv_b, w_down, w_kv, w_q, lam, subln_g, w_o):
    rows = lambda a: a[:, None, :]
    p = dict(ln1_g=rows(ln1_g), ln1_b=rows(ln1_b), ln2_g=rows(ln2_g), ln2_b=rows(ln2_b),
             pool_scale=rows(pool_scale), conv_w=conv_w, conv_b=rows(conv_b), lam=lam, subln_g=rows(subln_g),
             w_pool=w_pool.astype(_BF16), w_up_f32=w_up, w_down_f32=w_down,
             w_up=[w_up[0].astype(_BF16)], w_down=[w_down[0].astype(_BF16)])

    def blocks(w):
        return w.reshape(-1, CAST_BLOCK_ELEMS // w.shape[-1], w.shape[-1])

    per_layer = D_MODEL * N_HEADS * HEAD_W // CAST_BLOCK_ELEMS
    kv_blocks, q_blocks, o_blocks = blocks(w_kv), blocks(w_q), blocks(w_o)
    p['to_cast'] = {N_A_LAYERS - 2: [('w_kv', (kv_blocks, 0, kv_blocks.shape[0]))]}
    for j in range(DEPTH - N_A_LAYERS):
        p['to_cast'][N_A_LAYERS - 1 + j] = [('w_q%d' % j, (q_blocks, j * per_layer, per_layer)),
                                            ('w_o%d' % j, (o_blocks, j * per_layer, per_layer))]

    def finish(x, layer_inputs, new_conv, k_new, v_new, seqs):
        steps = x.shape[0] // seqs
        y = _to_batch_major(x, seqs)
        pool = jnp.stack([_to_batch_major(u[(steps - POOL_HIST) * seqs:], seqs) for u in layer_inputs])
        conv = jnp.stack([_to_batch_major(c, seqs) for c in new_conv])
        return y, pool, conv, _to_batch_major(k_new, seqs), _to_batch_major(v_new, seqs)

    b, t, _ = x_prompt.shape
    assert b == 1 and t >= POOL_HIST
    zeros_pool = jnp.zeros(((POOL_HIST + 1) * b, D_MODEL), _F32)
    zeros_conv = jnp.zeros(((CONV_W - 1) * b, 2 * D_FF), _F32)
    out_p = _run_group(_to_time_major(x_prompt), [zeros_pool] * N_A_LAYERS, [zeros_conv] * DEPTH,
                       None, 0, b, p)
    y_p, pool_p, conv_p, k_p, v_p = finish(*out_p, b)

    sb, steps, _ = x_sample.shape
    past = cache_k.shape[1]
    assert steps >= POOL_HIST
    pad = jnp.zeros((sb, D_MODEL), _F32)
    pool_hist = [jnp.concatenate([pad, _to_time_major(state_pool[l])], axis=0) for l in range(N_A_LAYERS)]
    conv_hist = [_to_time_major(state_ffn_conv[l]) for l in range(DEPTH)]
    out_s = _run_group(_to_time_major(x_sample), pool_hist, conv_hist, (cache_k, cache_v), past, sb, p)
    y_s, pool_s, conv_s, k_s, v_s = finish(*out_s, sb)

    return (y_p, y_s, pool_p, pool_s, conv_p, conv_s, k_p, v_p, k_s, v_s)
```

```python
import functools
import math

import jax
import jax.numpy as jnp
from jax import lax
from jax.experimental import pallas as pl
from jax.experimental.pallas import tpu as pltpu

D_MODEL = 2048
DEPTH = 4
N_A_LAYERS = DEPTH // 2
CHUNK = 64
POOL_WINDOWS = (2, 4, 8, 16)
POOL_GROUP = D_MODEL // len(POOL_WINDOWS)
POOL_HIST = max(POOL_WINDOWS) - 1
N_HEADS = 16
QK_DIM = 64
V_DIM = 2 * QK_DIM
HEAD_W = 2 * QK_DIM
D_FF = 5632
CONV_W = 3
LN_EPS = 1e-5
ALPHA = (2 * DEPTH) ** 0.25
ATTN_SCALE = QK_DIM ** -0.5
Q_SCALE = ATTN_SCALE * math.log2(math.e)
NEG_INF = -1e30

V7X_VMEM_BYTES = 64 * 1024 * 1024
VMEM_LIMIT_BYTES = V7X_VMEM_BYTES - 6 * 1024 * 1024
SUBLANES = 8

ROW_TILE = 512
FF_TILE = 512
CAST_BLOCK_ELEMS = 64 * 1024
N_FF_TILES = D_FF // FF_TILE
MM_COL_TILE = 1024
Q_TILE = 256
K_TILE = 512
STEP_UNROLL = 4
SAMPLE_HEADS_PER_STEP = 8

_F32 = jnp.float32
_BF16 = jnp.bfloat16


def _params(n_axes):
    return pltpu.CompilerParams(dimension_semantics=("arbitrary",) * n_axes,
                                vmem_limit_bytes=VMEM_LIMIT_BYTES)


def _layer_norm(z, g, b):
    mu = jnp.mean(z, axis=-1, keepdims=True)
    zc = z - mu
    var = jnp.mean(zc * zc, axis=-1, keepdims=True)
    return zc * lax.rsqrt(var + LN_EPS) * g + b


def _gelu_tanh(x):
    cdf = 0.5 * (1.0 + jnp.tanh(math.sqrt(2.0 / math.pi) * (x + 0.044715 * (x * x * x))))
    return x * cdf


def _pool_ln_kernel(x_ref, halo_ref, hist_ref, w_ref, scale_ref, g_ref, b_ref, o_ref, ob_ref,
                    ext_ref, z_ref, *, seqs, rows, pos0):
    i = pl.program_id(0)
    halo_rows = (POOL_HIST + 1) * seqs

    @pl.when(i == 0)
    def _():
        ext_ref[0:halo_rows, :] = hist_ref[...]

    @pl.when(i > 0)
    def _():
        ext_ref[0:halo_rows, :] = halo_ref[...]

    ext_ref[halo_rows:halo_rows + rows, :] = x_ref[...]

    row = lax.broadcasted_iota(jnp.int32, (rows, 1), 0) + i * rows
    step = lax.shift_right_logical(row, int(math.log2(seqs)))
    pos_plus_1 = (step + (pos0 + 1)).astype(_F32)

    for g, w in enumerate(POOL_WINDOWS):
        cols = slice(g * POOL_GROUP, (g + 1) * POOL_GROUP)
        cur = x_ref[:, cols]
        win = cur
        for k in range(1, w):
            start = halo_rows - k * seqs
            win = win + ext_ref[start:start + rows, cols]
        cnt = jnp.minimum(float(w), pos_plus_1)
        d = win / cnt - cur
        mix = jnp.dot(d.astype(_BF16), w_ref[g], preferred_element_type=_F32)
        z_ref[:, cols] = ALPHA * cur + mix * scale_ref[:, cols]

    out = _layer_norm(z_ref[...], g_ref[...], b_ref[...])
    o_ref[...] = out
    ob_ref[...] = out.astype(_BF16)


def _pool_ln(x, hist, w_pool, scale, g, b, *, layer, seqs, pos0):
    m = x.shape[0]
    rows = min(ROW_TILE, m)
    halo_rows = (POOL_HIST + 1) * seqs
    assert m % rows == 0 and rows % halo_rows == 0 and seqs & (seqs - 1) == 0
    halo_blocks_per_tile = rows // halo_rows
    kern = functools.partial(_pool_ln_kernel, seqs=seqs, rows=rows, pos0=pos0)
    return pl.pallas_call(
        kern,
        grid=(m // rows,),
        in_specs=[
            pl.BlockSpec((rows, D_MODEL), lambda i: (i, 0)),
            pl.BlockSpec((halo_rows, D_MODEL),
                         lambda i: (jnp.maximum(i * halo_blocks_per_tile - 1, 0), 0)),
            pl.BlockSpec((halo_rows, D_MODEL), lambda i: (0, 0)),
            pl.BlockSpec((None, len(POOL_WINDOWS), POOL_GROUP, POOL_GROUP), lambda i: (layer, 0, 0, 0)),
            pl.BlockSpec((None, 1, D_MODEL), lambda i: (layer, 0, 0)),
            pl.BlockSpec((None, 1, D_MODEL), lambda i: (layer, 0, 0)),
            pl.BlockSpec((None, 1, D_MODEL), lambda i: (layer, 0, 0)),
        ],
        out_specs=[pl.BlockSpec((rows, D_MODEL), lambda i: (i, 0))] * 2,
        out_shape=[jax.ShapeDtypeStruct((m, D_MODEL), _F32), jax.ShapeDtypeStruct((m, D_MODEL), _BF16)],
        scratch_shapes=[pltpu.VMEM((halo_rows + rows, D_MODEL), _F32),
                        pltpu.VMEM((rows, D_MODEL), _F32)],
        compiler_params=_params(1),
        name="pool_ln",
    )(x, x, hist, w_pool, scale, g, b)


def _ffn_ln_kernel(xb_ref, x_ref, hist_a_ref, hist_v_ref, wa_ref, wv_ref, cwa_ref, cwv_ref, cba_ref, cbv_ref,
                   wd_ref, g_ref, b_ref, *refs, seqs, rows, n_items, bf16_copy, cast_next, side_blocks):
    refs = list(refs)
    s = pl.program_id(0)
    next_up_ref, next_down_ref = (refs.pop(0), refs.pop(0)) if cast_next else (None, None)
    side_in = [refs.pop(0) for _ in side_blocks]
    o_ref = refs.pop(0)
    ob_ref = refs.pop(0) if bf16_copy else None
    new_a_ref, new_v_ref = refs.pop(0), refs.pop(0)
    next_up_b_ref, next_down_b_ref = (refs.pop(0), refs.pop(0)) if cast_next else (None, None)
    side_out = [refs.pop(0) for _ in side_blocks]
    ext_a_ref, ext_v_ref = refs
    if cast_next:
        next_up_b_ref[...] = next_up_ref[...].astype(_BF16)
        next_down_b_ref[...] = next_down_ref[...].astype(_BF16)
    for src_ref, dst_ref, n_blocks in zip(side_in, side_out, side_blocks):
        @pl.when(s < n_blocks)
        def _(src_ref=src_ref, dst_ref=dst_ref):
            dst_ref[...] = src_ref[...].astype(_BF16)

    halo = (CONV_W - 1) * seqs
    a_item = jnp.minimum(s, n_items - 1)
    i_a = lax.div(a_item, N_FF_TILES)
    j_a = a_item - i_a * N_FF_TILES
    j_b = lax.rem(jnp.maximum(s - 1, 0), N_FF_TILES)
    parity = s & 1

    @pl.when(s == 0)
    def _():
        ext_a_ref[1] = jnp.zeros(ext_a_ref.shape[1:], _F32)
        ext_v_ref[1] = jnp.zeros(ext_v_ref.shape[1:], _F32)

    n_copies = ext_a_ref.shape[1]
    delays = range(CONV_W)
    if n_copies == 1:
        copy_of, write_at, read_at = [0] * CONV_W, [halo] * CONV_W, [halo - d * seqs for d in delays]
    else:
        copy_of, write_at, read_at = list(delays), [d * seqs for d in delays], [0] * CONV_W

    def fill_history(src_a, src_v):
        for c in range(n_copies):
            n = write_at[copy_of.index(c)]
            if n:
                ext_a_ref[parity, c, 0:n, :] = src_a[halo - n:halo, :]
                ext_v_ref[parity, c, 0:n, :] = src_v[halo - n:halo, :]

    @pl.when(i_a == 0)
    def _():
        fill_history(hist_a_ref[...], hist_v_ref[...])

    @pl.when(i_a > 0)
    def _():
        fill_history(new_a_ref[j_a], new_v_ref[j_a])

    @pl.when(j_b == 0)
    def _():
        o_ref[...] = jnp.zeros_like(o_ref)

    def both_halves(slot):
        xb = xb_ref[...]
        ha = jnp.dot(xb, wa_ref[...], preferred_element_type=_F32)
        hv = jnp.dot(xb, wv_ref[...], preferred_element_type=_F32)
        for c in range(n_copies):
            at = write_at[copy_of.index(c)]
            ext_a_ref[slot, c, at:at + rows, :] = ha
            ext_v_ref[slot, c, at:at + rows, :] = hv
        new_a_ref[j_a] = ha[rows - halo:, :]
        new_v_ref[j_a] = hv[rows - halo:, :]

        def conv(ext_ref, cw_ref, cb_ref):
            cw = cw_ref[...]
            c = cb_ref[...]
            for tap in range(CONV_W):
                d = CONV_W - 1 - tap
                c = c + cw[tap:tap + 1, :] * ext_ref[1 - slot, copy_of[d], read_at[d]:read_at[d] + rows, :]
            return c

        ca = conv(ext_a_ref, cwa_ref, cba_ref)
        cv = conv(ext_v_ref, cwv_ref, cbv_ref)
        act = (_gelu_tanh(ca) * cv).astype(_BF16)
        o_ref[...] += jnp.dot(act, wd_ref[...], preferred_element_type=_F32)

    for slot in (0, 1):
        pl.when(parity == slot)(functools.partial(both_halves, slot))

    @pl.when(jnp.logical_and(s > 0, j_b == N_FF_TILES - 1))
    def _():
        out = _layer_norm(ALPHA * x_ref[...] + o_ref[...], g_ref[...], b_ref[...])
        o_ref[...] = out
        if ob_ref is not None:
            ob_ref[...] = out.astype(_BF16)


def _ffn_ln(xb, x, hist, w_up, conv_w, conv_b, w_down, g, b, *, layer, seqs, bf16_copy, next_weights=None,
            side_casts=()):
    m = x.shape[0]
    rows = min(ROW_TILE, m)
    halo = (CONV_W - 1) * seqs
    ext_rows = -(-(rows + halo) // SUBLANES) * SUBLANES
    n_copies = 1 if seqs % SUBLANES == 0 else CONV_W
    assert m % rows == 0 and rows >= halo
    n_items = (m // rows) * N_FF_TILES
    kern = functools.partial(_ffn_ln_kernel, seqs=seqs, rows=rows, n_items=n_items, bf16_copy=bf16_copy,
                             cast_next=next_weights is not None, side_blocks=tuple(n for _, _, n in side_casts))

    def up_item(s):
        item = jnp.minimum(s, n_items - 1)
        return lax.div(item, N_FF_TILES), lax.rem(item, N_FF_TILES)

    def down_item(s):
        item = jnp.maximum(s - 1, 0)
        return lax.div(item, N_FF_TILES), lax.rem(item, N_FF_TILES)

    up_a = lambda s: (0, up_item(s)[1])
    up_v = lambda s: (0, up_item(s)[1] + N_FF_TILES)
    conv_w_a = lambda s: (layer, 0, down_item(s)[1])
    conv_w_v = lambda s: (layer, 0, down_item(s)[1] + N_FF_TILES)
    ln_row = lambda s: (layer, 0, 0)
    down_rows = pl.BlockSpec((rows, D_MODEL), lambda s: (down_item(s)[0], 0))
    new_hist = pl.BlockSpec((N_FF_TILES, halo, FF_TILE), lambda s: (0, 0, 0))
    row_out = [jax.ShapeDtypeStruct((m, D_MODEL), _F32)]
    if bf16_copy:
        row_out.append(jax.ShapeDtypeStruct((m, D_MODEL), _BF16))
    cast_in, cast_specs, cast_out = [], [], []
    if next_weights is not None:
        n_row_tiles = m // rows
        up_blk = (D_MODEL // n_row_tiles, 2 * D_FF // N_FF_TILES)
        down_blk = (D_FF // n_items, D_MODEL)
        assert up_blk[0] * n_row_tiles == D_MODEL and down_blk[0] * n_items == D_FF
        assert up_blk[0] % (2 * SUBLANES) == 0 and down_blk[0] % (2 * SUBLANES) == 0
        cast_in = list(next_weights)
        cast_specs = [pl.BlockSpec((None,) + up_blk, lambda s: (layer + 1,) + up_item(s)),
                      pl.BlockSpec((None,) + down_blk, lambda s: (layer + 1, jnp.minimum(s, n_items - 1), 0))]
        cast_out = [(pl.BlockSpec(up_blk, up_item), jax.ShapeDtypeStruct((D_MODEL, 2 * D_FF), _BF16)),
                    (pl.BlockSpec(down_blk, lambda s: (jnp.minimum(s, n_items - 1), 0)),
                     jax.ShapeDtypeStruct((D_FF, D_MODEL), _BF16))]
    for blocks, first, count in side_casts:
        assert count <= n_items
        blk = (None,) + blocks.shape[1:]
        cast_in.append(blocks)
        cast_specs.append(pl.BlockSpec(blk, lambda s, first=first, count=count: (first + jnp.minimum(s, count - 1), 0, 0)))
        cast_out.append((pl.BlockSpec(blk, lambda s, count=count: (jnp.minimum(s, count - 1), 0, 0)),
                         jax.ShapeDtypeStruct((count,) + blocks.shape[1:], _BF16)))
    return pl.pallas_call(
        kern,
        grid=(n_items + 1,),
        in_specs=[
            pl.BlockSpec((rows, D_MODEL), lambda s: (up_item(s)[0], 0)),
            down_rows,
            pl.BlockSpec((halo, FF_TILE), up_a),
            pl.BlockSpec((halo, FF_TILE), up_v),
            pl.BlockSpec((D_MODEL, FF_TILE), up_a),
            pl.BlockSpec((D_MODEL, FF_TILE), up_v),
            pl.BlockSpec((None, CONV_W, FF_TILE), conv_w_a),
            pl.BlockSpec((None, CONV_W, FF_TILE), conv_w_v),
            pl.BlockSpec((None, 1, FF_TILE), conv_w_a),
            pl.BlockSpec((None, 1, FF_TILE), conv_w_v),
            pl.BlockSpec((FF_TILE, D_MODEL), lambda s: (down_item(s)[1], 0)),
            pl.BlockSpec((None, 1, D_MODEL), ln_row),
            pl.BlockSpec((None, 1, D_MODEL), ln_row),
        ] + cast_specs,
        out_specs=[down_rows] * len(row_out) + [new_hist, new_hist] + [spec for spec, _ in cast_out],
        out_shape=(row_out + [jax.ShapeDtypeStruct((N_FF_TILES, halo, FF_TILE), _F32)] * 2
                   + [shape for _, shape in cast_out]),
        scratch_shapes=[
            pltpu.VMEM((2, n_copies, ext_rows, FF_TILE), _F32),
            pltpu.VMEM((2, n_copies, ext_rows, FF_TILE), _F32),
        ],
        compiler_params=_params(1),
        name="ffn_ln",
    )(xb, x, hist, hist, w_up, w_up, conv_w, conv_w, conv_b, conv_b, w_down, g, b, *cast_in)


def _proj_kernel(x_ref, w_ref, *out_refs, scale):
    r = jnp.dot(x_ref[...], w_ref[...], preferred_element_type=_F32)
    for o_ref in out_refs:
        if o_ref.dtype == _F32:
            o_ref[...] = r.reshape(o_ref.shape)
        else:
            o_ref[...] = (r * scale).astype(o_ref.dtype)


def _proj(x, w, out_dtypes, *, n, col0=0, layer=None, scale=1.0):
    m, k = x.shape
    rows = min(2 * ROW_TILE, m)
    cols = min(MM_COL_TILE, n)
    assert m % rows == 0 and n % cols == 0 and col0 % cols == 0 and cols % HEAD_W == 0
    if layer is None:
        w_spec = pl.BlockSpec((k, cols), lambda i, j: (0, j + col0 // cols))
    else:
        w_spec = pl.BlockSpec((None, k, cols), lambda i, j: (layer, 0, j + col0 // cols))
    flat = pl.BlockSpec((rows, cols), lambda i, j: (i, j)), (m, n)
    heads = pl.BlockSpec((rows, cols // HEAD_W, HEAD_W), lambda i, j: (i, j, 0)), (m, n // HEAD_W, HEAD_W)
    outs = [heads if dt == _F32 else flat for dt in out_dtypes]
    return pl.pallas_call(
        functools.partial(_proj_kernel, scale=scale),
        grid=(m // rows, n // cols),
        in_specs=[pl.BlockSpec((rows, k), lambda i, j: (i, 0)), w_spec],
        out_specs=[spec for spec, _ in outs],
        out_shape=[jax.ShapeDtypeStruct(shape, dt) for (_, shape), dt in zip(outs, out_dtypes)],
        compiler_params=_params(2),
        name="proj",
    )(x, w)


def _diff_lambda(lam_ref, lam_init):
    lam = lam_ref[...]
    e1 = jnp.exp(jnp.sum(lam[0:1, :] * lam[1:2, :], axis=-1, keepdims=True))
    e2 = jnp.exp(jnp.sum(lam[2:3, :] * lam[3:4, :], axis=-1, keepdims=True))
    return e1 - e2 + lam_init


def _split_parts(q):
    first = lax.broadcasted_iota(jnp.int32, (1, HEAD_W), 1) < QK_DIM
    zero = jnp.zeros_like(q)
    return jnp.concatenate([jnp.where(first, q, zero), jnp.where(first, zero, q)], axis=0)


def _sub_norm(o, sg_ref, lam_init):
    return o * lax.rsqrt(jnp.mean(o * o, axis=-1, keepdims=True) + LN_EPS) * sg_ref[...] * (1.0 - lam_init)


def _score(qz, kb):
    return lax.dot_general(qz, kb, (((1,), (1,)), ((), ())), preferred_element_type=_F32)


def _flash_kernel(qt_ref, k_ref, vt_ref, bias_ref, lam_ref, sg_ref, o_ref,
                  qz_ref, s_ref, p_ref, a_ref, m_ref, l_ref, acc_ref, *, lam_init, n_q_blocks):
    lam = _diff_lambda(lam_ref, lam_init)
    n_cols = 2 * Q_TILE
    first_part = lax.broadcasted_iota(jnp.int32, (HEAD_W, 1), 0) < QK_DIM

    l_ref[...] = jnp.zeros_like(l_ref)
    acc_ref[...] = jnp.zeros_like(acc_ref)
    p_ref[1] = jnp.zeros(p_ref.shape[1:], _BF16)
    a_ref[1] = jnp.ones(a_ref.shape[1:], _F32)

    def load_queries(qi):
        qt = qt_ref[qi]
        zero = jnp.zeros_like(qt)
        qz_ref[qi & 1, :, 0:Q_TILE] = jnp.where(first_part, qt, zero)
        qz_ref[qi & 1, :, Q_TILE:n_cols] = jnp.where(first_part, zero, qt)

    def scores(qi, b, slot):
        k0 = pl.multiple_of(b * K_TILE, K_TILE)
        s_ref[slot] = jnp.dot(k_ref[pl.ds(k0, K_TILE), :], qz_ref[qi & 1], preferred_element_type=_F32)

    def softmax(slot, bias):
        s = s_ref[slot]
        if bias is not None:
            s = s + bias
        m_prev = m_ref[...]
        m_new = jnp.maximum(m_prev, jnp.max(s, axis=0, keepdims=True))
        a = jnp.exp2(m_prev - m_new)
        p = jnp.exp2(s - m_new)
        l_ref[...] = a * l_ref[...] + jnp.sum(p, axis=0, keepdims=True)
        m_ref[...] = m_new
        a_ref[slot] = a
        p_ref[slot] = p.astype(_BF16)

    def values(b, slot):
        vb = vt_ref[jnp.maximum(b, 0)]
        acc_ref[...] = a_ref[slot] * acc_ref[...] + jnp.dot(vb, p_ref[slot], preferred_element_type=_F32)

    def step(qi, t, slot):
        values(t - 1, 1 - slot)
        softmax(slot, None)
        scores(qi, t + 1, 1 - slot)

    def q_block(qi, carry):
        m_ref[...] = jnp.full(m_ref.shape, NEG_INF, _F32)
        n_full = lax.div(qi, K_TILE // Q_TILE)

        done = 0
        unroll = STEP_UNROLL
        while unroll > 1:
            def unrolled(u, c, unroll=unroll, done=done):
                for i in range(unroll):
                    step(qi, done + unroll * u + i, i & 1)
                return c

            trips = lax.div(n_full - done, unroll)
            lax.fori_loop(0, trips, unrolled, 0)
            done = done + trips * unroll
            unroll //= 2

        def single(t, c):
            step(qi, t, t & 1)
            return c

        lax.fori_loop(done, n_full, single, 0)

        last = n_full & 1
        values(n_full - 1, 1 - last)
        softmax(last, bias_ref[qi - n_full * (K_TILE // Q_TILE)])
        nxt = jnp.minimum(qi + 1, n_q_blocks - 1)
        load_queries(nxt)
        scores(nxt, 0, 0)
        values(n_full, last)

        o2 = acc_ref[...] / l_ref[...]
        o = o2[:, 0:Q_TILE] - lam * o2[:, Q_TILE:n_cols]
        norm = lax.rsqrt(jnp.mean(o * o, axis=0, keepdims=True) + LN_EPS)
        o_ref[qi] = (o * norm * sg_ref[...] * (1.0 - lam_init)).astype(o_ref.dtype)
        return carry

    load_queries(0)
    scores(0, 0, 0)
    lax.fori_loop(0, n_q_blocks, q_block, 0)


def _flash_prompt(q, k, v, lam, sg, *, layer, lam_init):
    t = q.shape[0]
    assert t % K_TILE == 0 and K_TILE % Q_TILE == 0 and Q_TILE % CHUNK == 0 and STEP_UNROLL % 2 == 0
    nq, nk = t // Q_TILE, t // K_TILE
    qt = jnp.transpose(q.reshape(nq, Q_TILE, N_HEADS, HEAD_W), (2, 0, 3, 1))
    vt = jnp.transpose(v.reshape(nk, K_TILE, N_HEADS, V_DIM), (2, 0, 3, 1))
    sub = K_TILE // Q_TILE
    key_chunk = (jnp.arange(K_TILE) // CHUNK)[None, :, None]
    query = jnp.arange(sub)[:, None, None] * Q_TILE + (jnp.arange(2 * Q_TILE) % Q_TILE)[None, None, :]
    tail_bias = jnp.where(key_chunk <= query // CHUNK, 0.0, NEG_INF).astype(_F32)
    kern = functools.partial(_flash_kernel, lam_init=lam_init, n_q_blocks=nq)
    out = pl.pallas_call(
        kern,
        grid=(N_HEADS,),
        in_specs=[pl.BlockSpec((None, nq, HEAD_W, Q_TILE), lambda h: (h, 0, 0, 0)),
                  pl.BlockSpec((t, HEAD_W), lambda h: (0, h)),
                  pl.BlockSpec((None, nk, V_DIM, K_TILE), lambda h: (h, 0, 0, 0)),
                  pl.BlockSpec((sub, K_TILE, 2 * Q_TILE), lambda h: (0, 0, 0)),
                  pl.BlockSpec((None, 4, QK_DIM), lambda h: (layer, 0, 0)),
                  pl.BlockSpec((None, V_DIM, 1), lambda h: (layer, 0, 0))],
        out_specs=pl.BlockSpec((None, nq, V_DIM, Q_TILE), lambda h: (h, 0, 0, 0)),
        out_shape=jax.ShapeDtypeStruct((N_HEADS, nq, V_DIM, Q_TILE), _BF16),
        scratch_shapes=[pltpu.VMEM((2, HEAD_W, 2 * Q_TILE), _BF16),
                        pltpu.VMEM((2, K_TILE, 2 * Q_TILE), _F32),
                        pltpu.VMEM((2, K_TILE, 2 * Q_TILE), _BF16),
                        pltpu.VMEM((2, 1, 2 * Q_TILE), _F32),
                        pltpu.VMEM((1, 2 * Q_TILE), _F32),
                        pltpu.VMEM((1, 2 * Q_TILE), _F32),
                        pltpu.VMEM((V_DIM, 2 * Q_TILE), _F32)],
        compiler_params=_params(1),
        name="flash_prompt",
    )(qt, k, vt, tail_bias, lam, sg.reshape(sg.shape[0], V_DIM, 1))
    return jnp.transpose(out, (1, 3, 0, 2)).reshape(t, N_HEADS * V_DIM)


def _sample_attn_kernel(q_ref, kn_ref, vn_ref, kc_ref, vc_ref, lam_ref, sg_ref, o_ref, *, lam_init, steps):
    heads = SAMPLE_HEADS_PER_STEP
    per_head = 2 * steps
    past = kc_ref.shape[1]
    lam = _diff_lambda(lam_ref, lam_init)
    head_cols = [slice(h * HEAD_W, (h + 1) * HEAD_W) for h in range(heads)]
    head_rows = [slice(h * per_head, (h + 1) * per_head) for h in range(heads)]

    qz = jnp.concatenate([_split_parts(q_ref[:, c]) for c in head_cols], axis=0)
    kc = kc_ref[0].reshape(past * heads, HEAD_W).astype(_BF16)
    vc = vc_ref[0].reshape(past * heads, V_DIM).astype(_BF16)
    sc = _score(qz, kc)
    row_head = lax.shift_right_logical(lax.broadcasted_iota(jnp.int32, (heads * per_head, 1), 0),
                                       int(math.log2(per_head)))
    col_head = lax.broadcasted_iota(jnp.int32, (1, past * heads), 1) & (heads - 1)
    sc = jnp.where(row_head == col_head, sc, NEG_INF)
    sn = jnp.concatenate([_score(qz[r, :], kn_ref[:, c]) for r, c in zip(head_rows, head_cols)], axis=0)
    m = jnp.maximum(jnp.max(sc, axis=1, keepdims=True), jnp.max(sn, axis=1, keepdims=True))
    pc = jnp.exp2(sc - m)
    pn = jnp.exp2(sn - m)
    l = jnp.sum(pc, axis=1, keepdims=True) + jnp.sum(pn, axis=1, keepdims=True)
    pn = pn.astype(_BF16)
    on = jnp.concatenate([jnp.dot(pn[r, :], vn_ref[:, c], preferred_element_type=_F32)
                          for r, c in zip(head_rows, head_cols)], axis=0)
    o2 = (jnp.dot(pc.astype(_BF16), vc, preferred_element_type=_F32) + on) / l
    for h in range(heads):
        first = h * per_head
        o = o2[first:first + steps, :] - lam * o2[first + steps:first + per_head, :]
        o_ref[:, head_cols[h]] = _sub_norm(o, sg_ref, lam_init).astype(o_ref.dtype)


def _sample_attn(q, k_new, v_new, cache_k, cache_v, lam, sg, *, layer, lam_init, seqs):
    m, width = q.shape
    steps = m // seqs
    past = cache_k.shape[1]
    group_w = SAMPLE_HEADS_PER_STEP * HEAD_W
    groups = width // group_w
    kern = functools.partial(_sample_attn_kernel, lam_init=lam_init, steps=steps)
    new = pl.BlockSpec((steps, group_w), lambda b, g: (0, b * groups + g))
    cache = pl.BlockSpec((1, past, SAMPLE_HEADS_PER_STEP, HEAD_W), lambda b, g: (b, 0, g, 0))
    out = pl.pallas_call(
        kern,
        grid=(seqs, groups),
        in_specs=[new, new, new, cache, cache,
                  pl.BlockSpec((None, 4, QK_DIM), lambda b, g: (layer, 0, 0)),
                  pl.BlockSpec((None, 1, V_DIM), lambda b, g: (layer, 0, 0))],
        out_specs=new,
        out_shape=jax.ShapeDtypeStruct((steps, seqs * width), _BF16),
        compiler_params=_params(2),
        name="sample_attn",
    )(q.reshape(steps, seqs * width), k_new.reshape(steps, seqs * width),
      v_new.reshape(steps, seqs * width), cache_k, cache_v, lam, sg)
    return out.reshape(m, width)


def _oproj_ln_kernel(x_ref, a_ref, w_ref, g_ref, b_ref, o_ref, ob_ref):
    mix = jnp.dot(a_ref[...], w_ref[...], preferred_element_type=_F32)
    out = _layer_norm(ALPHA * x_ref[...] + mix, g_ref[...], b_ref[...])
    o_ref[...] = out
    ob_ref[...] = out.astype(_BF16)


def _oproj_ln(x, attn, w_o, g, b, *, ln_layer):
    m = x.shape[0]
    rows = min(ROW_TILE, m)
    assert m % rows == 0
    const = lambda i: (ln_layer, 0, 0)
    return pl.pallas_call(
        _oproj_ln_kernel,
        grid=(m // rows,),
        in_specs=[pl.BlockSpec((rows, D_MODEL), lambda i: (i, 0)),
                  pl.BlockSpec((rows, N_HEADS * V_DIM), lambda i: (i, 0)),
                  pl.BlockSpec((N_HEADS * V_DIM, D_MODEL), lambda i: (0, 0)),
                  pl.BlockSpec((None, 1, D_MODEL), const),
                  pl.BlockSpec((None, 1, D_MODEL), const)],
        out_specs=[pl.BlockSpec((rows, D_MODEL), lambda i: (i, 0))] * 2,
        out_shape=[jax.ShapeDtypeStruct((m, D_MODEL), _F32), jax.ShapeDtypeStruct((m, D_MODEL), _BF16)],
        compiler_params=_params(1),
        name="oproj_ln",
    )(x, attn, w_o, g, b)


def _run_group(x, pool_hist, conv_hist, caches, pos0, seqs, p):
    layer_inputs, new_conv = [], []
    k_f32 = v_f32 = k_b = v_b = xb = None
    for l in range(DEPTH):
        if l < N_A_LAYERS:
            layer_inputs.append(x)
            x, xb = _pool_ln(x, pool_hist[l], p['w_pool'], p['pool_scale'], p['ln1_g'], p['ln1_b'],
                             layer=l, seqs=seqs, pos0=pos0)
        else:
            j = l - N_A_LAYERS
            lam_init = 0.8 - 0.6 * math.exp(-0.3 * l)
            (q,) = _proj(xb, p['w_q%d' % j], (_BF16,), n=N_HEADS * HEAD_W, scale=Q_SCALE)
            if caches is None:
                attn = _flash_prompt(q, k_b, v_b, p['lam'], p['subln_g'], layer=j, lam_init=lam_init)
            else:
                attn = _sample_attn(q, k_b, v_b, caches[0], caches[1], p['lam'], p['subln_g'],
                                    layer=j, lam_init=lam_init, seqs=seqs)
            x, xb = _oproj_ln(x, attn, p['w_o%d' % j], p['ln1_g'], p['ln1_b'], ln_layer=l)
        bf16_copy = N_A_LAYERS - 1 <= l < DEPTH - 1
        cast_next = l + 1 < DEPTH and len(p['w_up']) == l + 1
        side = [(name, spec) for name, spec in p['to_cast'].get(l, []) if name not in p]
        outs = _ffn_ln(xb, x, conv_hist[l], p['w_up'][l], p['conv_w'], p['conv_b'], p['w_down'][l],
                       p['ln2_g'], p['ln2_b'], layer=l, seqs=seqs, bf16_copy=bf16_copy,
                       next_weights=(p['w_up_f32'], p['w_down_f32']) if cast_next else None,
                       side_casts=[spec for _, spec in side])
        for (name, (blocks, _, _)), cast in zip(side, outs[len(outs) - len(side):]):
            p[name] = cast.reshape(-1, blocks.shape[-1])
        outs = outs[:len(outs) - len(side)]
        if cast_next:
            *outs, w_up_next, w_down_next = outs
            p['w_up'].append(w_up_next)
            p['w_down'].append(w_down_next)
        x, *xb, new_a, new_v = outs
        xb = xb[0] if xb else None
        new_conv.append(jnp.concatenate([_untile_cols(new_a), _untile_cols(new_v)], axis=1))
        if l == N_A_LAYERS - 1:
            k_width = N_HEADS * HEAD_W
            k_f32, k_b = _proj(xb, p['w_kv'], (_F32, _BF16), n=k_width)
            v_f32, v_b = _proj(xb, p['w_kv'], (_F32, _BF16), n=N_HEADS * V_DIM, col0=k_width)
    return x, layer_inputs, new_conv, k_f32, v_f32


def _untile_cols(a):
    tiles, rows, w = a.shape
    return jnp.transpose(a, (1, 0, 2)).reshape(rows, tiles * w)


def _to_time_major(a):
    seqs, steps, w = a.shape
    return jnp.transpose(a, (1, 0, 2)).reshape(steps * seqs, w)


def _to_batch_major(a, seqs):
    return jnp.swapaxes(a.reshape(a.shape[0] // seqs, seqs, *a.shape[1:]), 0, 1)


def kernel(x_prompt, x_sample, state_pool, state_ffn_conv, cache_k, cache_v, ln1_g, ln1_b, ln2_g, ln2_b,
           w_pool, pool_scale, w_up, conv_w, conv_b, w_down, w_kv, w_q, lam, subln_g, w_o):
    rows = lambda a: a[:, None, :]
    p = dict(ln1_g=rows(ln1_g), ln1_b=rows(ln1_b), ln2_g=rows(ln2_g), ln2_b=rows(ln2_b),
             pool_scale=rows(pool_scale), conv_w=conv_w, conv_b=rows(conv_b), lam=lam, subln_g=rows(subln_g),
             w_pool=w_pool.astype(_BF16), w_up_f32=w_up, w_down_f32=w_down,
             w_up=[w_up[0].astype(_BF16)], w_down=[w_down[0].astype(_BF16)])

    ffn_steps = (x_prompt.shape[1] // ROW_TILE) * N_FF_TILES
    block_elems = CAST_BLOCK_ELEMS
    while w_kv.size // block_elems > ffn_steps:
        block_elems *= 2

    def blocks(w):
        return w.reshape(-1, block_elems // w.shape[-1], w.shape[-1])

    per_layer = D_MODEL * N_HEADS * HEAD_W // block_elems
    kv_blocks, q_blocks, o_blocks = blocks(w_kv), blocks(w_q), blocks(w_o)
    p['to_cast'] = {N_A_LAYERS - 2: [('w_kv', (kv_blocks, 0, kv_blocks.shape[0]))]}
    for j in range(DEPTH - N_A_LAYERS):
        p['to_cast'][N_A_LAYERS - 1 + j] = [('w_q%d' % j, (q_blocks, j * per_layer, per_layer)),
                                            ('w_o%d' % j, (o_blocks, j * per_layer, per_layer))]

    def finish(x, layer_inputs, new_conv, k_new, v_new, seqs):
        steps = x.shape[0] // seqs
        y = _to_batch_major(x, seqs)
        pool = jnp.stack([_to_batch_major(u[(steps - POOL_HIST) * seqs:], seqs) for u in layer_inputs])
        conv = jnp.stack([_to_batch_major(c, seqs) for c in new_conv])
        return y, pool, conv, _to_batch_major(k_new, seqs), _to_batch_major(v_new, seqs)

    b, t, _ = x_prompt.shape
    assert b == 1 and t >= POOL_HIST
    zeros_pool = jnp.zeros(((POOL_HIST + 1) * b, D_MODEL), _F32)
    zeros_conv = jnp.zeros(((CONV_W - 1) * b, 2 * D_FF), _F32)
    out_p = _run_group(_to_time_major(x_prompt), [zeros_pool] * N_A_LAYERS, [zeros_conv] * DEPTH,
                       None, 0, b, p)
    y_p, pool_p, conv_p, k_p, v_p = finish(*out_p, b)

    sb, steps, _ = x_sample.shape
    past = cache_k.shape[1]
    assert steps >= POOL_HIST
    pad = jnp.zeros((sb, D_MODEL), _F32)
    pool_hist = [jnp.concatenate([pad, _to_time_major(state_pool[l])], axis=0) for l in range(N_A_LAYERS)]
    conv_hist = [_to_time_major(state_ffn_conv[l]) for l in range(DEPTH)]
    out_s = _run_group(_to_time_major(x_sample), pool_hist, conv_hist, (cache_k, cache_v), past, sb, p)
    y_s, pool_s, conv_s, k_s, v_s = finish(*out_s, sb)

    return (y_p, y_s, pool_p, pool_s, conv_p, conv_s, k_p, v_p, k_s, v_s)
```

```python
import functools
import math

import jax
import jax.numpy as jnp
from jax import lax
from jax.experimental import pallas as pl
from jax.experimental.pallas import tpu as pltpu

D_MODEL = 2048
DEPTH = 4
N_A_LAYERS = DEPTH // 2
CHUNK = 64
POOL_WINDOWS = (2, 4, 8, 16)
POOL_GROUP = D_MODEL // len(POOL_WINDOWS)
POOL_HIST = max(POOL_WINDOWS) - 1
N_HEADS = 16
QK_DIM = 64
V_DIM = 2 * QK_DIM
HEAD_W = 2 * QK_DIM
D_FF = 5632
CONV_W = 3
LN_EPS = 1e-5
ALPHA = (2 * DEPTH) ** 0.25
ATTN_SCALE = QK_DIM ** -0.5
Q_SCALE = ATTN_SCALE * math.log2(math.e)
NEG_INF = -1e30

V7X_VMEM_BYTES = 64 * 1024 * 1024
VMEM_LIMIT_BYTES = V7X_VMEM_BYTES - 6 * 1024 * 1024
SUBLANES = 8

ROW_TILE = 512
FF_TILE = 512
CAST_BLOCK_ELEMS = 64 * 1024
N_FF_TILES = D_FF // FF_TILE
MM_COL_TILE = 1024
Q_TILE = 256
K_TILE = 512
STEP_UNROLL = 4
SAMPLE_HEADS_PER_STEP = 8

_F32 = jnp.float32
_BF16 = jnp.bfloat16


def _params(n_axes):
    return pltpu.CompilerParams(dimension_semantics=("arbitrary",) * n_axes,
                                vmem_limit_bytes=VMEM_LIMIT_BYTES)


def _layer_norm(z, g, b):
    mu = jnp.mean(z, axis=-1, keepdims=True)
    zc = z - mu
    var = jnp.mean(zc * zc, axis=-1, keepdims=True)
    return zc * lax.rsqrt(var + LN_EPS) * g + b


def _gelu_tanh(x):
    cdf = 0.5 * (1.0 + jnp.tanh(math.sqrt(2.0 / math.pi) * (x + 0.044715 * (x * x * x))))
    return x * cdf


def _pool_window_starts(seqs):
    starts, lo, shift = [], 0, seqs
    for _ in POOL_WINDOWS:
        lo = -(-(lo + shift) // SUBLANES) * SUBLANES
        starts.append(lo)
        shift *= 2
    return starts


def _pool_ln_kernel(x_ref, halo_ref, hist_ref, w_ref, scale_ref, g_ref, b_ref, o_ref, ob_ref,
                    ext_ref, sums_ref, z_ref, *, seqs, rows, pos0):
    i = pl.program_id(0)
    halo_rows = ext_ref.shape[0] - rows
    hist_rows = hist_ref.shape[0]

    @pl.when(i == 0)
    def _():
        if halo_rows > hist_rows:
            ext_ref[0:halo_rows - hist_rows, :] = jnp.zeros((halo_rows - hist_rows, D_MODEL), _F32)
        ext_ref[halo_rows - hist_rows:halo_rows, :] = hist_ref[...]

    @pl.when(i > 0)
    def _():
        ext_ref[0:halo_rows, :] = halo_ref[...]

    ext_ref[halo_rows:halo_rows + rows, :] = x_ref[...]

    row = lax.broadcasted_iota(jnp.int32, (rows, 1), 0) + i * rows
    step = lax.shift_right_logical(row, int(math.log2(seqs)))
    pos_plus_1 = (step + (pos0 + 1)).astype(_F32)

    starts = _pool_window_starts(seqs)
    assert POOL_WINDOWS == tuple(2 ** (k + 1) for k in range(len(POOL_WINDOWS))) and starts[-1] <= halo_rows
    window_sums, src, shift = [], ext_ref, seqs
    for k in range(len(POOL_WINDOWS)):
        last = k == len(POOL_WINDOWS) - 1
        first = halo_rows if last else starts[k]
        n = halo_rows + rows - first
        cols = slice(k * POOL_GROUP, D_MODEL)
        total = src[first:first + n, cols] + src[first - shift:first - shift + n, cols]
        if last:
            window_sums.append(total)
        else:
            sums_ref[k, first:first + n, cols] = total
            window_sums.append(None)
            src, shift = sums_ref.at[k], 2 * shift

    for g, w in enumerate(POOL_WINDOWS):
        cols = slice(g * POOL_GROUP, (g + 1) * POOL_GROUP)
        cur = x_ref[:, cols]
        win = window_sums[g] if window_sums[g] is not None else sums_ref[g, halo_rows:halo_rows + rows, cols]
        cnt = jnp.minimum(float(w), pos_plus_1)
        d = win / cnt - cur
        mix = jnp.dot(d.astype(_BF16), w_ref[g], preferred_element_type=_F32)
        z_ref[:, cols] = ALPHA * cur + mix * scale_ref[:, cols]

    out = _layer_norm(z_ref[...], g_ref[...], b_ref[...])
    o_ref[...] = out
    ob_ref[...] = out.astype(_BF16)


def _pool_ln(x, hist, w_pool, scale, g, b, *, layer, seqs, pos0):
    m = x.shape[0]
    rows = min(ROW_TILE, m)
    hist_rows = (POOL_HIST + 1) * seqs
    halo_rows = max(hist_rows, _pool_window_starts(seqs)[-1])
    assert m % rows == 0 and rows % halo_rows == 0 and seqs & (seqs - 1) == 0 and hist.shape[0] == hist_rows
    halo_blocks_per_tile = rows // halo_rows
    kern = functools.partial(_pool_ln_kernel, seqs=seqs, rows=rows, pos0=pos0)
    return pl.pallas_call(
        kern,
        grid=(m // rows,),
        in_specs=[
            pl.BlockSpec((rows, D_MODEL), lambda i: (i, 0)),
            pl.BlockSpec((halo_rows, D_MODEL),
                         lambda i: (jnp.maximum(i * halo_blocks_per_tile - 1, 0), 0)),
            pl.BlockSpec((hist_rows, D_MODEL), lambda i: (0, 0)),
            pl.BlockSpec((None, len(POOL_WINDOWS), POOL_GROUP, POOL_GROUP), lambda i: (layer, 0, 0, 0)),
            pl.BlockSpec((None, 1, D_MODEL), lambda i: (layer, 0, 0)),
            pl.BlockSpec((None, 1, D_MODEL), lambda i: (layer, 0, 0)),
            pl.BlockSpec((None, 1, D_MODEL), lambda i: (layer, 0, 0)),
        ],
        out_specs=[pl.BlockSpec((rows, D_MODEL), lambda i: (i, 0))] * 2,
        out_shape=[jax.ShapeDtypeStruct((m, D_MODEL), _F32), jax.ShapeDtypeStruct((m, D_MODEL), _BF16)],
        scratch_shapes=[pltpu.VMEM((halo_rows + rows, D_MODEL), _F32),
                        pltpu.VMEM((len(POOL_WINDOWS) - 1, halo_rows + rows, D_MODEL), _F32),
                        pltpu.VMEM((rows, D_MODEL), _F32)],
        compiler_params=_params(1),
        name="pool_ln",
    )(x, x, hist, w_pool, scale, g, b)


def _ffn_ln_kernel(xb_ref, x_ref, hist_a_ref, hist_v_ref, wa_ref, wv_ref, cwa_ref, cwv_ref, cba_ref, cbv_ref,
                   wd_ref, g_ref, b_ref, *refs, seqs, rows, n_items, bf16_copy, cast_next, side_blocks):
    refs = list(refs)
    s = pl.program_id(0)
    next_up_ref, next_down_ref = (refs.pop(0), refs.pop(0)) if cast_next else (None, None)
    side_in = [refs.pop(0) for _ in side_blocks]
    o_ref = refs.pop(0)
    ob_ref = refs.pop(0) if bf16_copy else None
    new_a_ref, new_v_ref = refs.pop(0), refs.pop(0)
    next_up_b_ref, next_down_b_ref = (refs.pop(0), refs.pop(0)) if cast_next else (None, None)
    side_out = [refs.pop(0) for _ in side_blocks]
    ext_a_ref, ext_v_ref = refs
    if cast_next:
        next_up_b_ref[...] = next_up_ref[...].astype(_BF16)
        next_down_b_ref[...] = next_down_ref[...].astype(_BF16)
    for src_ref, dst_ref, n_blocks in zip(side_in, side_out, side_blocks):
        @pl.when(s < n_blocks)
        def _(src_ref=src_ref, dst_ref=dst_ref):
            dst_ref[...] = src_ref[...].astype(_BF16)

    halo = (CONV_W - 1) * seqs
    a_item = jnp.minimum(s, n_items - 1)
    i_a = lax.div(a_item, N_FF_TILES)
    j_a = a_item - i_a * N_FF_TILES
    j_b = lax.rem(jnp.maximum(s - 1, 0), N_FF_TILES)
    parity = s & 1

    @pl.when(s == 0)
    def _():
        ext_a_ref[1] = jnp.zeros(ext_a_ref.shape[1:], _F32)
        ext_v_ref[1] = jnp.zeros(ext_v_ref.shape[1:], _F32)

    n_copies = ext_a_ref.shape[1]
    delays = range(CONV_W)
    if n_copies == 1:
        copy_of, write_at, read_at = [0] * CONV_W, [halo] * CONV_W, [halo - d * seqs for d in delays]
    else:
        copy_of, write_at, read_at = list(delays), [d * seqs for d in delays], [0] * CONV_W

    def fill_history(src_a, src_v):
        for c in range(n_copies):
            n = write_at[copy_of.index(c)]
            if n:
                ext_a_ref[parity, c, 0:n, :] = src_a[halo - n:halo, :]
                ext_v_ref[parity, c, 0:n, :] = src_v[halo - n:halo, :]

    @pl.when(i_a == 0)
    def _():
        fill_history(hist_a_ref[...], hist_v_ref[...])

    @pl.when(i_a > 0)
    def _():
        fill_history(new_a_ref[j_a], new_v_ref[j_a])

    @pl.when(j_b == 0)
    def _():
        o_ref[...] = jnp.zeros_like(o_ref)

    def both_halves(slot):
        xb = xb_ref[...]
        ha = jnp.dot(xb, wa_ref[...], preferred_element_type=_F32)
        hv = jnp.dot(xb, wv_ref[...], preferred_element_type=_F32)
        for c in range(n_copies):
            at = write_at[copy_of.index(c)]
            ext_a_ref[slot, c, at:at + rows, :] = ha
            ext_v_ref[slot, c, at:at + rows, :] = hv
        new_a_ref[j_a] = ha[rows - halo:, :]
        new_v_ref[j_a] = hv[rows - halo:, :]

        def conv(ext_ref, cw_ref, cb_ref):
            cw = cw_ref[...]
            c = cb_ref[...]
            for tap in range(CONV_W):
                d = CONV_W - 1 - tap
                c = c + cw[tap:tap + 1, :] * ext_ref[1 - slot, copy_of[d], read_at[d]:read_at[d] + rows, :]
            return c

        ca = conv(ext_a_ref, cwa_ref, cba_ref)
        cv = conv(ext_v_ref, cwv_ref, cbv_ref)
        act = (_gelu_tanh(ca) * cv).astype(_BF16)
        o_ref[...] += jnp.dot(act, wd_ref[...], preferred_element_type=_F32)

    for slot in (0, 1):
        pl.when(parity == slot)(functools.partial(both_halves, slot))

    @pl.when(jnp.logical_and(s > 0, j_b == N_FF_TILES - 1))
    def _():
        out = _layer_norm(ALPHA * x_ref[...] + o_ref[...], g_ref[...], b_ref[...])
        o_ref[...] = out
        if ob_ref is not None:
            ob_ref[...] = out.astype(_BF16)


def _ffn_ln(xb, x, hist, w_up, conv_w, conv_b, w_down, g, b, *, layer, seqs, bf16_copy, next_weights=None,
            side_casts=()):
    m = x.shape[0]
    rows = min(ROW_TILE, m)
    halo = (CONV_W - 1) * seqs
    ext_rows = -(-(rows + halo) // SUBLANES) * SUBLANES
    n_copies = 1 if seqs % SUBLANES == 0 else CONV_W
    assert m % rows == 0 and rows >= halo
    n_items = (m // rows) * N_FF_TILES
    kern = functools.partial(_ffn_ln_kernel, seqs=seqs, rows=rows, n_items=n_items, bf16_copy=bf16_copy,
                             cast_next=next_weights is not None, side_blocks=tuple(n for _, _, n in side_casts))

    def up_item(s):
        item = jnp.minimum(s, n_items - 1)
        return lax.div(item, N_FF_TILES), lax.rem(item, N_FF_TILES)

    def down_item(s):
        item = jnp.maximum(s - 1, 0)
        return lax.div(item, N_FF_TILES), lax.rem(item, N_FF_TILES)

    up_a = lambda s: (0, up_item(s)[1])
    up_v = lambda s: (0, up_item(s)[1] + N_FF_TILES)
    conv_w_a = lambda s: (layer, 0, down_item(s)[1])
    conv_w_v = lambda s: (layer, 0, down_item(s)[1] + N_FF_TILES)
    ln_row = lambda s: (layer, 0, 0)
    down_rows = pl.BlockSpec((rows, D_MODEL), lambda s: (down_item(s)[0], 0))
    new_hist = pl.BlockSpec((N_FF_TILES, halo, FF_TILE), lambda s: (0, 0, 0))
    row_out = [jax.ShapeDtypeStruct((m, D_MODEL), _F32)]
    if bf16_copy:
        row_out.append(jax.ShapeDtypeStruct((m, D_MODEL), _BF16))
    cast_in, cast_specs, cast_out = [], [], []
    if next_weights is not None:
        n_row_tiles = m // rows
        up_blk = (D_MODEL // n_row_tiles, 2 * D_FF // N_FF_TILES)
        down_blk = (D_FF // n_items, D_MODEL)
        assert up_blk[0] * n_row_tiles == D_MODEL and down_blk[0] * n_items == D_FF
        assert up_blk[0] % (2 * SUBLANES) == 0 and down_blk[0] % (2 * SUBLANES) == 0
        cast_in = list(next_weights)
        cast_specs = [pl.BlockSpec((None,) + up_blk, lambda s: (layer + 1,) + up_item(s)),
                      pl.BlockSpec((None,) + down_blk, lambda s: (layer + 1, jnp.minimum(s, n_items - 1), 0))]
        cast_out = [(pl.BlockSpec(up_blk, up_item), jax.ShapeDtypeStruct((D_MODEL, 2 * D_FF), _BF16)),
                    (pl.BlockSpec(down_blk, lambda s: (jnp.minimum(s, n_items - 1), 0)),
                     jax.ShapeDtypeStruct((D_FF, D_MODEL), _BF16))]
    for blocks, first, count in side_casts:
        assert count <= n_items
        blk = (None,) + blocks.shape[1:]
        cast_in.append(blocks)
        cast_specs.append(pl.BlockSpec(blk, lambda s, first=first, count=count: (first + jnp.minimum(s, count - 1), 0, 0)))
        cast_out.append((pl.BlockSpec(blk, lambda s, count=count: (jnp.minimum(s, count - 1), 0, 0)),
                         jax.ShapeDtypeStruct((count,) + blocks.shape[1:], _BF16)))
    return pl.pallas_call(
        kern,
        grid=(n_items + 1,),
        in_specs=[
            pl.BlockSpec((rows, D_MODEL), lambda s: (up_item(s)[0], 0)),
            down_rows,
            pl.BlockSpec((halo, FF_TILE), up_a),
            pl.BlockSpec((halo, FF_TILE), up_v),
            pl.BlockSpec((D_MODEL, FF_TILE), up_a),
            pl.BlockSpec((D_MODEL, FF_TILE), up_v),
            pl.BlockSpec((None, CONV_W, FF_TILE), conv_w_a),
            pl.BlockSpec((None, CONV_W, FF_TILE), conv_w_v),
            pl.BlockSpec((None, 1, FF_TILE), conv_w_a),
            pl.BlockSpec((None, 1, FF_TILE), conv_w_v),
            pl.BlockSpec((FF_TILE, D_MODEL), lambda s: (down_item(s)[1], 0)),
            pl.BlockSpec((None, 1, D_MODEL), ln_row),
            pl.BlockSpec((None, 1, D_MODEL), ln_row),
        ] + cast_specs,
        out_specs=[down_rows] * len(row_out) + [new_hist, new_hist] + [spec for spec, _ in cast_out],
        out_shape=(row_out + [jax.ShapeDtypeStruct((N_FF_TILES, halo, FF_TILE), _F32)] * 2
                   + [shape for _, shape in cast_out]),
        scratch_shapes=[
            pltpu.VMEM((2, n_copies, ext_rows, FF_TILE), _F32),
            pltpu.VMEM((2, n_copies, ext_rows, FF_TILE), _F32),
        ],
        compiler_params=_params(1),
        name="ffn_ln",
    )(xb, x, hist, hist, w_up, w_up, conv_w, conv_w, conv_b, conv_b, w_down, g, b, *cast_in)


def _proj_kernel(x_ref, w_ref, *out_refs, scale):
    r = jnp.dot(x_ref[...], w_ref[...], preferred_element_type=_F32)
    for o_ref in out_refs:
        if o_ref.dtype == _F32:
            o_ref[...] = r.reshape(o_ref.shape)
        else:
            o_ref[...] = (r * scale).astype(o_ref.dtype)


def _proj(x, w, out_dtypes, *, n, col0=0, layer=None, scale=1.0):
    m, k = x.shape
    rows = min(2 * ROW_TILE, m)
    cols = min(MM_COL_TILE, n)
    assert m % rows == 0 and n % cols == 0 and col0 % cols == 0 and cols % HEAD_W == 0
    if layer is None:
        w_spec = pl.BlockSpec((k, cols), lambda i, j: (0, j + col0 // cols))
    else:
        w_spec = pl.BlockSpec((None, k, cols), lambda i, j: (layer, 0, j + col0 // cols))
    flat = pl.BlockSpec((rows, cols), lambda i, j: (i, j)), (m, n)
    heads = pl.BlockSpec((rows, cols // HEAD_W, HEAD_W), lambda i, j: (i, j, 0)), (m, n // HEAD_W, HEAD_W)
    outs = [heads if dt == _F32 else flat for dt in out_dtypes]
    return pl.pallas_call(
        functools.partial(_proj_kernel, scale=scale),
        grid=(m // rows, n // cols),
        in_specs=[pl.BlockSpec((rows, k), lambda i, j: (i, 0)), w_spec],
        out_specs=[spec for spec, _ in outs],
        out_shape=[jax.ShapeDtypeStruct(shape, dt) for (_, shape), dt in zip(outs, out_dtypes)],
        compiler_params=_params(2),
        name="proj",
    )(x, w)


def _diff_lambda(lam_ref, lam_init):
    lam = lam_ref[...]
    e1 = jnp.exp(jnp.sum(lam[0:1, :] * lam[1:2, :], axis=-1, keepdims=True))
    e2 = jnp.exp(jnp.sum(lam[2:3, :] * lam[3:4, :], axis=-1, keepdims=True))
    return e1 - e2 + lam_init


def _split_parts(q):
    first = lax.broadcasted_iota(jnp.int32, (1, HEAD_W), 1) < QK_DIM
    zero = jnp.zeros_like(q)
    return jnp.concatenate([jnp.where(first, q, zero), jnp.where(first, zero, q)], axis=0)


def _sub_norm(o, sg_ref, lam_init):
    return o * lax.rsqrt(jnp.mean(o * o, axis=-1, keepdims=True) + LN_EPS) * sg_ref[...] * (1.0 - lam_init)


def _score(qz, kb):
    return lax.dot_general(qz, kb, (((1,), (1,)), ((), ())), preferred_element_type=_F32)


def _flash_kernel(qt_ref, k_ref, vt_ref, bias_ref, lam_ref, sg_ref, o_ref,
                  qz_ref, s_ref, p_ref, a_ref, m_ref, l_ref, acc_ref, *, lam_init, n_q_blocks):
    lam = _diff_lambda(lam_ref, lam_init)
    n_cols = 2 * Q_TILE
    first_part = lax.broadcasted_iota(jnp.int32, (HEAD_W, 1), 0) < QK_DIM

    l_ref[...] = jnp.zeros_like(l_ref)
    acc_ref[...] = jnp.zeros_like(acc_ref)
    p_ref[1] = jnp.zeros(p_ref.shape[1:], _BF16)
    a_ref[1] = jnp.ones(a_ref.shape[1:], _F32)

    def load_queries(qi):
        qt = qt_ref[qi]
        zero = jnp.zeros_like(qt)
        qz_ref[qi & 1, :, 0:Q_TILE] = jnp.where(first_part, qt, zero)
        qz_ref[qi & 1, :, Q_TILE:n_cols] = jnp.where(first_part, zero, qt)

    def scores(qi, b, slot):
        k0 = pl.multiple_of(b * K_TILE, K_TILE)
        s_ref[slot] = jnp.dot(k_ref[pl.ds(k0, K_TILE), :], qz_ref[qi & 1], preferred_element_type=_F32)

    def softmax(slot, bias):
        s = s_ref[slot]
        if bias is not None:
            s = s + bias
        m_prev = m_ref[...]
        m_new = jnp.maximum(m_prev, jnp.max(s, axis=0, keepdims=True))
        a = jnp.exp2(m_prev - m_new)
        p = jnp.exp2(s - m_new)
        l_ref[...] = a * l_ref[...] + jnp.sum(p, axis=0, keepdims=True)
        m_ref[...] = m_new
        a_ref[slot] = a
        p_ref[slot] = p.astype(_BF16)

    def values(b, slot):
        vb = vt_ref[jnp.maximum(b, 0)]
        acc_ref[...] = a_ref[slot] * acc_ref[...] + jnp.dot(vb, p_ref[slot], preferred_element_type=_F32)

    def step(qi, t, slot):
        values(t - 1, 1 - slot)
        softmax(slot, None)
        scores(qi, t + 1, 1 - slot)

    def q_block(qi, carry):
        m_ref[...] = jnp.full(m_ref.shape, NEG_INF, _F32)
        n_full = lax.div(qi, K_TILE // Q_TILE)

        done = 0
        unroll = STEP_UNROLL
        while unroll > 1:
            def unrolled(u, c, unroll=unroll, done=done):
                for i in range(unroll):
                    step(qi, done + unroll * u + i, i & 1)
                return c

            trips = lax.div(n_full - done, unroll)
            lax.fori_loop(0, trips, unrolled, 0)
            done = done + trips * unroll
            unroll //= 2

        def single(t, c):
            step(qi, t, t & 1)
            return c

        lax.fori_loop(done, n_full, single, 0)

        last = n_full & 1
        values(n_full - 1, 1 - last)
        softmax(last, bias_ref[qi - n_full * (K_TILE // Q_TILE)])
        nxt = jnp.minimum(qi + 1, n_q_blocks - 1)
        load_queries(nxt)
        scores(nxt, 0, 0)
        values(n_full, last)

        o2 = acc_ref[...] / l_ref[...]
        o = o2[:, 0:Q_TILE] - lam * o2[:, Q_TILE:n_cols]
        norm = lax.rsqrt(jnp.mean(o * o, axis=0, keepdims=True) + LN_EPS)
        o_ref[qi] = (o * norm * sg_ref[...] * (1.0 - lam_init)).astype(o_ref.dtype)
        return carry

    load_queries(0)
    scores(0, 0, 0)
    lax.fori_loop(0, n_q_blocks, q_block, 0)


def _flash_prompt(q, k, v, lam, sg, *, layer, lam_init):
    t = q.shape[0]
    assert t % K_TILE == 0 and K_TILE % Q_TILE == 0 and Q_TILE % CHUNK == 0 and STEP_UNROLL % 2 == 0
    nq, nk = t // Q_TILE, t // K_TILE
    qt = jnp.transpose(q.reshape(nq, Q_TILE, N_HEADS, HEAD_W), (2, 0, 3, 1))
    vt = jnp.transpose(v.reshape(nk, K_TILE, N_HEADS, V_DIM), (2, 0, 3, 1))
    sub = K_TILE // Q_TILE
    key_chunk = (jnp.arange(K_TILE) // CHUNK)[None, :, None]
    query = jnp.arange(sub)[:, None, None] * Q_TILE + (jnp.arange(2 * Q_TILE) % Q_TILE)[None, None, :]
    tail_bias = jnp.where(key_chunk <= query // CHUNK, 0.0, NEG_INF).astype(_F32)
    kern = functools.partial(_flash_kernel, lam_init=lam_init, n_q_blocks=nq)
    out = pl.pallas_call(
        kern,
        grid=(N_HEADS,),
        in_specs=[pl.BlockSpec((None, nq, HEAD_W, Q_TILE), lambda h: (h, 0, 0, 0)),
                  pl.BlockSpec((t, HEAD_W), lambda h: (0, h)),
                  pl.BlockSpec((None, nk, V_DIM, K_TILE), lambda h: (h, 0, 0, 0)),
                  pl.BlockSpec((sub, K_TILE, 2 * Q_TILE), lambda h: (0, 0, 0)),
                  pl.BlockSpec((None, 4, QK_DIM), lambda h: (layer, 0, 0)),
                  pl.BlockSpec((None, V_DIM, 1), lambda h: (layer, 0, 0))],
        out_specs=pl.BlockSpec((None, nq, V_DIM, Q_TILE), lambda h: (h, 0, 0, 0)),
        out_shape=jax.ShapeDtypeStruct((N_HEADS, nq, V_DIM, Q_TILE), _BF16),
        scratch_shapes=[pltpu.VMEM((2, HEAD_W, 2 * Q_TILE), _BF16),
                        pltpu.VMEM((2, K_TILE, 2 * Q_TILE), _F32),
                        pltpu.VMEM((2, K_TILE, 2 * Q_TILE), _BF16),
                        pltpu.VMEM((2, 1, 2 * Q_TILE), _F32),
                        pltpu.VMEM((1, 2 * Q_TILE), _F32),
                        pltpu.VMEM((1, 2 * Q_TILE), _F32),
                        pltpu.VMEM((V_DIM, 2 * Q_TILE), _F32)],
        compiler_params=_params(1),
        name="flash_prompt",
    )(qt, k, vt, tail_bias, lam, sg.reshape(sg.shape[0], V_DIM, 1))
    return jnp.transpose(out, (1, 3, 0, 2)).reshape(t, N_HEADS * V_DIM)


def _sample_attn_kernel(q_ref, kn_ref, vn_ref, kc_ref, vc_ref, lam_ref, sg_ref, o_ref, *, lam_init, steps):
    heads = SAMPLE_HEADS_PER_STEP
    per_head = 2 * steps
    past = kc_ref.shape[1]
    lam = _diff_lambda(lam_ref, lam_init)
    head_cols = [slice(h * HEAD_W, (h + 1) * HEAD_W) for h in range(heads)]
    head_rows = [slice(h * per_head, (h + 1) * per_head) for h in range(heads)]

    qz = jnp.concatenate([_split_parts(q_ref[:, c]) for c in head_cols], axis=0)
    kc = kc_ref[0].reshape(past * heads, HEAD_W).astype(_BF16)
    vc = vc_ref[0].reshape(past * heads, V_DIM).astype(_BF16)
    sc = _score(qz, kc)
    row_head = lax.shift_right_logical(lax.broadcasted_iota(jnp.int32, (heads * per_head, 1), 0),
                                       int(math.log2(per_head)))
    col_head = lax.broadcasted_iota(jnp.int32, (1, past * heads), 1) & (heads - 1)
    sc = jnp.where(row_head == col_head, sc, NEG_INF)
    sn = jnp.concatenate([_score(qz[r, :], kn_ref[:, c]) for r, c in zip(head_rows, head_cols)], axis=0)
    m = jnp.maximum(jnp.max(sc, axis=1, keepdims=True), jnp.max(sn, axis=1, keepdims=True))
    pc = jnp.exp2(sc - m)
    pn = jnp.exp2(sn - m)
    l = jnp.sum(pc, axis=1, keepdims=True) + jnp.sum(pn, axis=1, keepdims=True)
    pn = pn.astype(_BF16)
    on = jnp.concatenate([jnp.dot(pn[r, :], vn_ref[:, c], preferred_element_type=_F32)
                          for r, c in zip(head_rows, head_cols)], axis=0)
    o2 = (jnp.dot(pc.astype(_BF16), vc, preferred_element_type=_F32) + on) / l
    for h in range(heads):
        first = h * per_head
        o = o2[first:first + steps, :] - lam * o2[first + steps:first + per_head, :]
        o_ref[:, head_cols[h]] = _sub_norm(o, sg_ref, lam_init).astype(o_ref.dtype)


def _sample_attn(q, k_new, v_new, cache_k, cache_v, lam, sg, *, layer, lam_init, seqs):
    m, width = q.shape
    steps = m // seqs
    past = cache_k.shape[1]
    group_w = SAMPLE_HEADS_PER_STEP * HEAD_W
    groups = width // group_w
    kern = functools.partial(_sample_attn_kernel, lam_init=lam_init, steps=steps)
    new = pl.BlockSpec((steps, group_w), lambda b, g: (0, b * groups + g))
    cache = pl.BlockSpec((1, past, SAMPLE_HEADS_PER_STEP, HEAD_W), lambda b, g: (b, 0, g, 0))
    out = pl.pallas_call(
        kern,
        grid=(seqs, groups),
        in_specs=[new, new, new, cache, cache,
                  pl.BlockSpec((None, 4, QK_DIM), lambda b, g: (layer, 0, 0)),
                  pl.BlockSpec((None, 1, V_DIM), lambda b, g: (layer, 0, 0))],
        out_specs=new,
        out_shape=jax.ShapeDtypeStruct((steps, seqs * width), _BF16),
        compiler_params=_params(2),
        name="sample_attn",
    )(q.reshape(steps, seqs * width), k_new.reshape(steps, seqs * width),
      v_new.reshape(steps, seqs * width), cache_k, cache_v, lam, sg)
    return out.reshape(m, width)


def _oproj_ln_kernel(x_ref, a_ref, w_ref, g_ref, b_ref, o_ref, ob_ref):
    mix = jnp.dot(a_ref[...], w_ref[...], preferred_element_type=_F32)
    out = _layer_norm(ALPHA * x_ref[...] + mix, g_ref[...], b_ref[...])
    o_ref[...] = out
    ob_ref[...] = out.astype(_BF16)


def _oproj_ln(x, attn, w_o, g, b, *, ln_layer):
    m = x.shape[0]
    rows = min(ROW_TILE, m)
    assert m % rows == 0
    const = lambda i: (ln_layer, 0, 0)
    return pl.pallas_call(
        _oproj_ln_kernel,
        grid=(m // rows,),
        in_specs=[pl.BlockSpec((rows, D_MODEL), lambda i: (i, 0)),
                  pl.BlockSpec((rows, N_HEADS * V_DIM), lambda i: (i, 0)),
                  pl.BlockSpec((N_HEADS * V_DIM, D_MODEL), lambda i: (0, 0)),
                  pl.BlockSpec((None, 1, D_MODEL), const),
                  pl.BlockSpec((None, 1, D_MODEL), const)],
        out_specs=[pl.BlockSpec((rows, D_MODEL), lambda i: (i, 0))] * 2,
        out_shape=[jax.ShapeDtypeStruct((m, D_MODEL), _F32), jax.ShapeDtypeStruct((m, D_MODEL), _BF16)],
        compiler_params=_params(1),
        name="oproj_ln",
    )(x, attn, w_o, g, b)


def _run_group(x, pool_hist, conv_hist, caches, pos0, seqs, p):
    layer_inputs, new_conv = [], []
    k_f32 = v_f32 = k_b = v_b = xb = None
    for l in range(DEPTH):
        if l < N_A_LAYERS:
            layer_inputs.append(x)
            x, xb = _pool_ln(x, pool_hist[l], p['w_pool'], p['pool_scale'], p['ln1_g'], p['ln1_b'],
                             layer=l, seqs=seqs, pos0=pos0)
        else:
            j = l - N_A_LAYERS
            lam_init = 0.8 - 0.6 * math.exp(-0.3 * l)
            (q,) = _proj(xb, p['w_q%d' % j], (_BF16,), n=N_HEADS * HEAD_W, scale=Q_SCALE)
            if caches is None:
                attn = _flash_prompt(q, k_b, v_b, p['lam'], p['subln_g'], layer=j, lam_init=lam_init)
            else:
                attn = _sample_attn(q, k_b, v_b, caches[0], caches[1], p['lam'], p['subln_g'],
                                    layer=j, lam_init=lam_init, seqs=seqs)
            x, xb = _oproj_ln(x, attn, p['w_o%d' % j], p['ln1_g'], p['ln1_b'], ln_layer=l)
        bf16_copy = N_A_LAYERS - 1 <= l < DEPTH - 1
        cast_next = l + 1 < DEPTH and len(p['w_up']) == l + 1
        side = [(name, spec) for name, spec in p['to_cast'].get(l, []) if name not in p]
        outs = _ffn_ln(xb, x, conv_hist[l], p['w_up'][l], p['conv_w'], p['conv_b'], p['w_down'][l],
                       p['ln2_g'], p['ln2_b'], layer=l, seqs=seqs, bf16_copy=bf16_copy,
                       next_weights=(p['w_up_f32'], p['w_down_f32']) if cast_next else None,
                       side_casts=[spec for _, spec in side])
        for (name, (blocks, _, _)), cast in zip(side, outs[len(outs) - len(side):]):
            p[name] = cast.reshape(-1, blocks.shape[-1])
        outs = outs[:len(outs) - len(side)]
        if cast_next:
            *outs, w_up_next, w_down_next = outs
            p['w_up'].append(w_up_next)
            p['w_down'].append(w_down_next)
        x, *xb, new_a, new_v = outs
        xb = xb[0] if xb else None
        new_conv.append(jnp.concatenate([_untile_cols(new_a), _untile_cols(new_v)], axis=1))
        if l == N_A_LAYERS - 1:
            k_width = N_HEADS * HEAD_W
            k_f32, k_b = _proj(xb, p['w_kv'], (_F32, _BF16), n=k_width)
            v_f32, v_b = _proj(xb, p['w_kv'], (_F32, _BF16), n=N_HEADS * V_DIM, col0=k_width)
    return x, layer_inputs, new_conv, k_f32, v_f32


def _untile_cols(a):
    tiles, rows, w = a.shape
    return jnp.transpose(a, (1, 0, 2)).reshape(rows, tiles * w)


def _to_time_major(a):
    seqs, steps, w = a.shape
    return jnp.transpose(a, (1, 0, 2)).reshape(steps * seqs, w)


def _to_batch_major(a, seqs):
    return jnp.swapaxes(a.reshape(a.shape[0] // seqs, seqs, *a.shape[1:]), 0, 1)


def kernel(x_prompt, x_sample, state_pool, state_ffn_conv, cache_k, cache_v, ln1_g, ln1_b, ln2_g, ln2_b,
           w_pool, pool_scale, w_up, conv_w, conv_b, w_down, w_kv, w_q, lam, subln_g, w_o):
    rows = lambda a: a[:, None, :]
    p = dict(ln1_g=rows(ln1_g), ln1_b=rows(ln1_b), ln2_g=rows(ln2_g), ln2_b=rows(ln2_b),
             pool_scale=rows(pool_scale), conv_w=conv_w, conv_b=rows(conv_b), lam=lam, subln_g=rows(subln_g),
             w_pool=w_pool.astype(_BF16), w_up_f32=w_up, w_down_f32=w_down,
             w_up=[w_up[0].astype(_BF16)], w_down=[w_down[0].astype(_BF16)])

    ffn_steps = (x_prompt.shape[1] // ROW_TILE) * N_FF_TILES
    block_elems = CAST_BLOCK_ELEMS
    while w_kv.size // block_elems > ffn_steps:
        block_elems *= 2

    def blocks(w):
        return w.reshape(-1, block_elems // w.shape[-1], w.shape[-1])

    per_layer = D_MODEL * N_HEADS * HEAD_W // block_elems
    kv_blocks, q_blocks, o_blocks = blocks(w_kv), blocks(w_q), blocks(w_o)
    p['to_cast'] = {N_A_LAYERS - 2: [('w_kv', (kv_blocks, 0, kv_blocks.shape[0]))]}
    for j in range(DEPTH - N_A_LAYERS):
        p['to_cast'][N_A_LAYERS - 1 + j] = [('w_q%d' % j, (q_blocks, j * per_layer, per_layer)),
                                            ('w_o%d' % j, (o_blocks, j * per_layer, per_layer))]

    def finish(x, layer_inputs, new_conv, k_new, v_new, seqs):
        steps = x.shape[0] // seqs
        y = _to_batch_major(x, seqs)
        pool = jnp.stack([_to_batch_major(u[(steps - POOL_HIST) * seqs:], seqs) for u in layer_inputs])
        conv = jnp.stack([_to_batch_major(c, seqs) for c in new_conv])
        return y, pool, conv, _to_batch_major(k_new, seqs), _to_batch_major(v_new, seqs)

    b, t, _ = x_prompt.shape
    assert b == 1 and t >= POOL_HIST
    zeros_pool = jnp.zeros(((POOL_HIST + 1) * b, D_MODEL), _F32)
    zeros_conv = jnp.zeros(((CONV_W - 1) * b, 2 * D_FF), _F32)
    out_p = _run_group(_to_time_major(x_prompt), [zeros_pool] * N_A_LAYERS, [zeros_conv] * DEPTH,
                       None, 0, b, p)
    y_p, pool_p, conv_p, k_p, v_p = finish(*out_p, b)

    sb, steps, _ = x_sample.shape
    past = cache_k.shape[1]
    assert steps >= POOL_HIST
    pad = jnp.zeros((sb, D_MODEL), _F32)
    pool_hist = [jnp.concatenate([pad, _to_time_major(state_pool[l])], axis=0) for l in range(N_A_LAYERS)]
    conv_hist = [_to_time_major(state_ffn_conv[l]) for l in range(DEPTH)]
    out_s = _run_group(_to_time_major(x_sample), pool_hist, conv_hist, (cache_k, cache_v), past, sb, p)
    y_s, pool_s, conv_s, k_s, v_s = finish(*out_s, sb)

    return (y_p, y_s, pool_p, pool_s, conv_p, conv_s, k_p, v_p, k_s, v_s)
```

```python
import functools
import math

import jax
import jax.numpy as jnp
from jax import lax
from jax.experimental import pallas as pl
from jax.experimental.pallas import tpu as pltpu

D_MODEL = 2048
DEPTH = 4
N_A_LAYERS = DEPTH // 2
CHUNK = 64
POOL_WINDOWS = (2, 4, 8, 16)
POOL_GROUP = D_MODEL // len(POOL_WINDOWS)
POOL_HIST = max(POOL_WINDOWS) - 1
N_HEADS = 16
QK_DIM = 64
V_DIM = 2 * QK_DIM
HEAD_W = 2 * QK_DIM
D_FF = 5632
CONV_W = 3
LN_EPS = 1e-5
ALPHA = (2 * DEPTH) ** 0.25
ATTN_SCALE = QK_DIM ** -0.5
Q_SCALE = ATTN_SCALE * math.log2(math.e)
NEG_INF = -1e30

V7X_VMEM_BYTES = 64 * 1024 * 1024
VMEM_LIMIT_BYTES = V7X_VMEM_BYTES - 6 * 1024 * 1024
SUBLANES = 8

ROW_TILE = 512
FF_TILE = 512
CAST_BLOCK_ELEMS = 64 * 1024
N_FF_TILES = D_FF // FF_TILE
MM_COL_TILE = 1024
Q_TILE = 256
K_TILE = 512
STEP_UNROLL = 4
SAMPLE_HEADS_PER_STEP = 8

_F32 = jnp.float32
_BF16 = jnp.bfloat16


def _params(n_axes):
    return pltpu.CompilerParams(dimension_semantics=("arbitrary",) * n_axes,
                                vmem_limit_bytes=VMEM_LIMIT_BYTES)


def _layer_norm(z, g, b):
    mu = jnp.mean(z, axis=-1, keepdims=True)
    zc = z - mu
    var = jnp.mean(zc * zc, axis=-1, keepdims=True)
    return zc * lax.rsqrt(var + LN_EPS) * g + b


def _gelu_tanh(x):
    cdf = 0.5 * (1.0 + jnp.tanh(math.sqrt(2.0 / math.pi) * (x + 0.044715 * (x * x * x))))
    return x * cdf


def _pool_window_starts(seqs):
    starts, lo, shift = [], 0, seqs
    for _ in POOL_WINDOWS:
        lo = -(-(lo + shift) // SUBLANES) * SUBLANES
        starts.append(lo)
        shift *= 2
    return starts


def _pool_ln_kernel(x_ref, halo_ref, hist_ref, w_ref, scale_ref, g_ref, b_ref, o_ref, ob_ref,
                    ext_ref, sums_ref, z_ref, *, seqs, rows, pos0):
    i = pl.program_id(0)
    halo_rows = ext_ref.shape[0] - rows
    hist_rows = hist_ref.shape[0]

    @pl.when(i == 0)
    def _():
        if halo_rows > hist_rows:
            ext_ref[0:halo_rows - hist_rows, :] = jnp.zeros((halo_rows - hist_rows, D_MODEL), _F32)
        ext_ref[halo_rows - hist_rows:halo_rows, :] = hist_ref[...]

    @pl.when(i > 0)
    def _():
        ext_ref[0:halo_rows, :] = halo_ref[...]

    ext_ref[halo_rows:halo_rows + rows, :] = x_ref[...]

    row = lax.broadcasted_iota(jnp.int32, (rows, 1), 0) + i * rows
    step = lax.shift_right_logical(row, int(math.log2(seqs)))
    pos_plus_1 = (step + (pos0 + 1)).astype(_F32)

    starts = _pool_window_starts(seqs)
    assert POOL_WINDOWS == tuple(2 ** (k + 1) for k in range(len(POOL_WINDOWS))) and starts[-1] <= halo_rows
    window_sums, src, shift = [], ext_ref, seqs
    for k in range(len(POOL_WINDOWS)):
        last = k == len(POOL_WINDOWS) - 1
        first = halo_rows if last else starts[k]
        n = halo_rows + rows - first
        cols = slice(k * POOL_GROUP, D_MODEL)
        total = src[first:first + n, cols] + src[first - shift:first - shift + n, cols]
        if last:
            window_sums.append(total)
        else:
            sums_ref[k, first:first + n, cols] = total
            window_sums.append(None)
            src, shift = sums_ref.at[k], 2 * shift

    for g, w in enumerate(POOL_WINDOWS):
        cols = slice(g * POOL_GROUP, (g + 1) * POOL_GROUP)
        cur = x_ref[:, cols]
        win = window_sums[g] if window_sums[g] is not None else sums_ref[g, halo_rows:halo_rows + rows, cols]
        cnt = jnp.minimum(float(w), pos_plus_1)
        d = win / cnt - cur
        mix = jnp.dot(d.astype(_BF16), w_ref[g], preferred_element_type=_F32)
        z_ref[:, cols] = ALPHA * cur + mix * scale_ref[:, cols]

    out = _layer_norm(z_ref[...], g_ref[...], b_ref[...])
    o_ref[...] = out
    ob_ref[...] = out.astype(_BF16)


def _pool_ln(x, hist, w_pool, scale, g, b, *, layer, seqs, pos0):
    m = x.shape[0]
    rows = min(ROW_TILE, m)
    hist_rows = (POOL_HIST + 1) * seqs
    halo_rows = max(hist_rows, _pool_window_starts(seqs)[-1])
    assert m % rows == 0 and rows % halo_rows == 0 and seqs & (seqs - 1) == 0 and hist.shape[0] == hist_rows
    halo_blocks_per_tile = rows // halo_rows
    kern = functools.partial(_pool_ln_kernel, seqs=seqs, rows=rows, pos0=pos0)
    return pl.pallas_call(
        kern,
        grid=(m // rows,),
        in_specs=[
            pl.BlockSpec((rows, D_MODEL), lambda i: (i, 0)),
            pl.BlockSpec((halo_rows, D_MODEL),
                         lambda i: (jnp.maximum(i * halo_blocks_per_tile - 1, 0), 0)),
            pl.BlockSpec((hist_rows, D_MODEL), lambda i: (0, 0)),
            pl.BlockSpec((None, len(POOL_WINDOWS), POOL_GROUP, POOL_GROUP), lambda i: (layer, 0, 0, 0)),
            pl.BlockSpec((None, 1, D_MODEL), lambda i: (layer, 0, 0)),
            pl.BlockSpec((None, 1, D_MODEL), lambda i: (layer, 0, 0)),
            pl.BlockSpec((None, 1, D_MODEL), lambda i: (layer, 0, 0)),
        ],
        out_specs=[pl.BlockSpec((rows, D_MODEL), lambda i: (i, 0))] * 2,
        out_shape=[jax.ShapeDtypeStruct((m, D_MODEL), _F32), jax.ShapeDtypeStruct((m, D_MODEL), _BF16)],
        scratch_shapes=[pltpu.VMEM((halo_rows + rows, D_MODEL), _F32),
                        pltpu.VMEM((len(POOL_WINDOWS) - 1, halo_rows + rows, D_MODEL), _F32),
                        pltpu.VMEM((rows, D_MODEL), _F32)],
        compiler_params=_params(1),
        name="pool_ln",
    )(x, x, hist, w_pool, scale, g, b)


def _ffn_ln_kernel(xb_ref, x_ref, hist_a_ref, hist_v_ref, wa_ref, wv_ref, cwa_ref, cwv_ref, cba_ref, cbv_ref,
                   wd_ref, g_ref, b_ref, *refs, seqs, rows, n_items, bf16_copy, cast_next, side_blocks):
    refs = list(refs)
    s = pl.program_id(0)
    next_up_ref, next_down_ref = (refs.pop(0), refs.pop(0)) if cast_next else (None, None)
    side_in = [refs.pop(0) for _ in side_blocks]
    o_ref = refs.pop(0)
    ob_ref = refs.pop(0) if bf16_copy else None
    new_a_ref, new_v_ref = refs.pop(0), refs.pop(0)
    next_up_b_ref, next_down_b_ref = (refs.pop(0), refs.pop(0)) if cast_next else (None, None)
    side_out = [refs.pop(0) for _ in side_blocks]
    ext_a_ref, ext_v_ref = refs
    if cast_next:
        next_up_b_ref[...] = next_up_ref[...].astype(_BF16)
        next_down_b_ref[...] = next_down_ref[...].astype(_BF16)
    for src_ref, dst_ref, n_blocks in zip(side_in, side_out, side_blocks):
        @pl.when(s < n_blocks)
        def _(src_ref=src_ref, dst_ref=dst_ref):
            dst_ref[...] = src_ref[...].astype(_BF16)

    halo = (CONV_W - 1) * seqs
    a_item = jnp.minimum(s, n_items - 1)
    i_a = lax.div(a_item, N_FF_TILES)
    j_a = a_item - i_a * N_FF_TILES
    j_b = lax.rem(jnp.maximum(s - 1, 0), N_FF_TILES)
    parity = s & 1

    @pl.when(s == 0)
    def _():
        ext_a_ref[1] = jnp.zeros(ext_a_ref.shape[1:], _F32)
        ext_v_ref[1] = jnp.zeros(ext_v_ref.shape[1:], _F32)

    n_copies = ext_a_ref.shape[1]
    delays = range(CONV_W)
    if n_copies == 1:
        copy_of, write_at, read_at = [0] * CONV_W, [halo] * CONV_W, [halo - d * seqs for d in delays]
    else:
        copy_of, write_at, read_at = list(delays), [d * seqs for d in delays], [0] * CONV_W

    def fill_history(src_a, src_v):
        for c in range(n_copies):
            n = write_at[copy_of.index(c)]
            if n:
                ext_a_ref[parity, c, 0:n, :] = src_a[halo - n:halo, :]
                ext_v_ref[parity, c, 0:n, :] = src_v[halo - n:halo, :]

    @pl.when(i_a == 0)
    def _():
        fill_history(hist_a_ref[...], hist_v_ref[...])

    @pl.when(i_a > 0)
    def _():
        fill_history(new_a_ref[j_a], new_v_ref[j_a])

    @pl.when(j_b == 0)
    def _():
        o_ref[...] = jnp.zeros_like(o_ref)

    def both_halves(slot):
        xb = xb_ref[...]
        ha = jnp.dot(xb, wa_ref[...], preferred_element_type=_F32)
        hv = jnp.dot(xb, wv_ref[...], preferred_element_type=_F32)
        for c in range(n_copies):
            at = write_at[copy_of.index(c)]
            ext_a_ref[slot, c, at:at + rows, :] = ha
            ext_v_ref[slot, c, at:at + rows, :] = hv
        new_a_ref[j_a] = ha[rows - halo:, :]
        new_v_ref[j_a] = hv[rows - halo:, :]

        def conv(ext_ref, cw_ref, cb_ref):
            cw = cw_ref[...]
            c = cb_ref[...]
            for tap in range(CONV_W):
                d = CONV_W - 1 - tap
                c = c + cw[tap:tap + 1, :] * ext_ref[1 - slot, copy_of[d], read_at[d]:read_at[d] + rows, :]
            return c

        ca = conv(ext_a_ref, cwa_ref, cba_ref)
        cv = conv(ext_v_ref, cwv_ref, cbv_ref)
        act = (_gelu_tanh(ca) * cv).astype(_BF16)
        o_ref[...] += jnp.dot(act, wd_ref[...], preferred_element_type=_F32)

    for slot in (0, 1):
        pl.when(parity == slot)(functools.partial(both_halves, slot))

    @pl.when(jnp.logical_and(s > 0, j_b == N_FF_TILES - 1))
    def _():
        out = _layer_norm(ALPHA * x_ref[...] + o_ref[...], g_ref[...], b_ref[...])
        o_ref[...] = out
        if ob_ref is not None:
            ob_ref[...] = out.astype(_BF16)


def _ffn_ln(xb, x, hist, w_up, conv_w, conv_b, w_down, g, b, *, layer, seqs, bf16_copy, next_weights=None,
            side_casts=()):
    m = x.shape[0]
    rows = min(ROW_TILE, m)
    halo = (CONV_W - 1) * seqs
    ext_rows = -(-(rows + halo) // SUBLANES) * SUBLANES
    n_copies = 1 if seqs % SUBLANES == 0 else CONV_W
    assert m % rows == 0 and rows >= halo
    n_items = (m // rows) * N_FF_TILES
    kern = functools.partial(_ffn_ln_kernel, seqs=seqs, rows=rows, n_items=n_items, bf16_copy=bf16_copy,
                             cast_next=next_weights is not None, side_blocks=tuple(n for _, _, n in side_casts))

    def up_item(s):
        item = jnp.minimum(s, n_items - 1)
        return lax.div(item, N_FF_TILES), lax.rem(item, N_FF_TILES)

    def down_item(s):
        item = jnp.maximum(s - 1, 0)
        return lax.div(item, N_FF_TILES), lax.rem(item, N_FF_TILES)

    up_a = lambda s: (0, up_item(s)[1])
    up_v = lambda s: (0, up_item(s)[1] + N_FF_TILES)
    conv_w_a = lambda s: (layer, 0, down_item(s)[1])
    conv_w_v = lambda s: (layer, 0, down_item(s)[1] + N_FF_TILES)
    ln_row = lambda s: (layer, 0, 0)
    down_rows = pl.BlockSpec((rows, D_MODEL), lambda s: (down_item(s)[0], 0))
    new_hist = pl.BlockSpec((N_FF_TILES, halo, FF_TILE), lambda s: (0, 0, 0))
    row_out = [jax.ShapeDtypeStruct((m, D_MODEL), _F32)]
    if bf16_copy:
        row_out.append(jax.ShapeDtypeStruct((m, D_MODEL), _BF16))
    cast_in, cast_specs, cast_out = [], [], []
    if next_weights is not None:
        n_row_tiles = m // rows
        up_blk = (D_MODEL // n_row_tiles, 2 * D_FF // N_FF_TILES)
        down_blk = (D_FF // n_items, D_MODEL)
        assert up_blk[0] * n_row_tiles == D_MODEL and down_blk[0] * n_items == D_FF
        assert up_blk[0] % (2 * SUBLANES) == 0 and down_blk[0] % (2 * SUBLANES) == 0
        cast_in = list(next_weights)
        cast_specs = [pl.BlockSpec((None,) + up_blk, lambda s: (layer + 1,) + up_item(s)),
                      pl.BlockSpec((None,) + down_blk, lambda s: (layer + 1, jnp.minimum(s, n_items - 1), 0))]
        cast_out = [(pl.BlockSpec(up_blk, up_item), jax.ShapeDtypeStruct((D_MODEL, 2 * D_FF), _BF16)),
                    (pl.BlockSpec(down_blk, lambda s: (jnp.minimum(s, n_items - 1), 0)),
                     jax.ShapeDtypeStruct((D_FF, D_MODEL), _BF16))]
    for blocks, first, count in side_casts:
        assert count <= n_items
        blk = (None,) + blocks.shape[1:]
        cast_in.append(blocks)
        cast_specs.append(pl.BlockSpec(blk, lambda s, first=first, count=count: (first + jnp.minimum(s, count - 1), 0, 0)))
        cast_out.append((pl.BlockSpec(blk, lambda s, count=count: (jnp.minimum(s, count - 1), 0, 0)),
                         jax.ShapeDtypeStruct((count,) + blocks.shape[1:], _BF16)))
    return pl.pallas_call(
        kern,
        grid=(n_items + 1,),
        in_specs=[
            pl.BlockSpec((rows, D_MODEL), lambda s: (up_item(s)[0], 0)),
            down_rows,
            pl.BlockSpec((halo, FF_TILE), up_a),
            pl.BlockSpec((halo, FF_TILE), up_v),
            pl.BlockSpec((D_MODEL, FF_TILE), up_a),
            pl.BlockSpec((D_MODEL, FF_TILE), up_v),
            pl.BlockSpec((None, CONV_W, FF_TILE), conv_w_a),
            pl.BlockSpec((None, CONV_W, FF_TILE), conv_w_v),
            pl.BlockSpec((None, 1, FF_TILE), conv_w_a),
            pl.BlockSpec((None, 1, FF_TILE), conv_w_v),
            pl.BlockSpec((FF_TILE, D_MODEL), lambda s: (down_item(s)[1], 0)),
            pl.BlockSpec((None, 1, D_MODEL), ln_row),
            pl.BlockSpec((None, 1, D_MODEL), ln_row),
        ] + cast_specs,
        out_specs=[down_rows] * len(row_out) + [new_hist, new_hist] + [spec for spec, _ in cast_out],
        out_shape=(row_out + [jax.ShapeDtypeStruct((N_FF_TILES, halo, FF_TILE), _F32)] * 2
                   + [shape for _, shape in cast_out]),
        scratch_shapes=[
            pltpu.VMEM((2, n_copies, ext_rows, FF_TILE), _F32),
            pltpu.VMEM((2, n_copies, ext_rows, FF_TILE), _F32),
        ],
        compiler_params=_params(1),
        name="ffn_ln",
    )(xb, x, hist, hist, w_up, w_up, conv_w, conv_w, conv_b, conv_b, w_down, g, b, *cast_in)


def _proj_kernel(x_ref, w_ref, *out_refs, scale):
    r = jnp.dot(x_ref[...], w_ref[...], preferred_element_type=_F32)
    for o_ref in out_refs:
        if o_ref.dtype == _F32:
            o_ref[...] = r.reshape(o_ref.shape)
        else:
            o_ref[...] = (r * scale).astype(o_ref.dtype)


def _proj(x, w, out_dtypes, *, n, col0=0, scale=1.0):
    m, k = x.shape
    rows = min(2 * ROW_TILE, m)
    cols = min(MM_COL_TILE, n)
    assert m % rows == 0 and n % cols == 0 and col0 % cols == 0 and cols % HEAD_W == 0
    w_spec = pl.BlockSpec((k, cols), lambda i, j: (0, j + col0 // cols))
    flat = pl.BlockSpec((rows, cols), lambda i, j: (i, j)), (m, n)
    heads = pl.BlockSpec((rows, cols // HEAD_W, HEAD_W), lambda i, j: (i, j, 0)), (m, n // HEAD_W, HEAD_W)
    outs = [heads if dt == _F32 else flat for dt in out_dtypes]
    return pl.pallas_call(
        functools.partial(_proj_kernel, scale=scale),
        grid=(m // rows, n // cols),
        in_specs=[pl.BlockSpec((rows, k), lambda i, j: (i, 0)), w_spec],
        out_specs=[spec for spec, _ in outs],
        out_shape=[jax.ShapeDtypeStruct(shape, dt) for (_, shape), dt in zip(outs, out_dtypes)],
        compiler_params=_params(2),
        name="proj",
    )(x, w)


def _diff_lambda(lam_ref, lam_init):
    lam = lam_ref[...]
    e1 = jnp.exp(jnp.sum(lam[0:1, :] * lam[1:2, :], axis=-1, keepdims=True))
    e2 = jnp.exp(jnp.sum(lam[2:3, :] * lam[3:4, :], axis=-1, keepdims=True))
    return e1 - e2 + lam_init


def _split_parts(q):
    first = lax.broadcasted_iota(jnp.int32, (1, HEAD_W), 1) < QK_DIM
    zero = jnp.zeros_like(q)
    return jnp.concatenate([jnp.where(first, q, zero), jnp.where(first, zero, q)], axis=0)


def _sub_norm(o, sg_ref, lam_init):
    return o * lax.rsqrt(jnp.mean(o * o, axis=-1, keepdims=True) + LN_EPS) * sg_ref[...] * (1.0 - lam_init)


def _score(qz, kb):
    return lax.dot_general(qz, kb, (((1,), (1,)), ((), ())), preferred_element_type=_F32)


def _flash_kernel(qt_ref, k_ref, vt_ref, bias_ref, lam_ref, sg_ref, o_ref,
                  qz_ref, s_ref, p_ref, a_ref, m_ref, l_ref, acc_ref, *, lam_init, n_q_blocks):
    lam = _diff_lambda(lam_ref, lam_init)
    n_cols = 2 * Q_TILE
    first_part = lax.broadcasted_iota(jnp.int32, (HEAD_W, 1), 0) < QK_DIM

    l_ref[...] = jnp.zeros_like(l_ref)
    acc_ref[...] = jnp.zeros_like(acc_ref)
    p_ref[1] = jnp.zeros(p_ref.shape[1:], _BF16)
    a_ref[1] = jnp.ones(a_ref.shape[1:], _F32)

    def load_queries(qi):
        qt = qt_ref[qi]
        zero = jnp.zeros_like(qt)
        qz_ref[qi & 1, :, 0:Q_TILE] = jnp.where(first_part, qt, zero)
        qz_ref[qi & 1, :, Q_TILE:n_cols] = jnp.where(first_part, zero, qt)

    def scores(qi, b, slot):
        k0 = pl.multiple_of(b * K_TILE, K_TILE)
        s_ref[slot] = jnp.dot(k_ref[pl.ds(k0, K_TILE), :], qz_ref[qi & 1], preferred_element_type=_F32)

    def softmax(slot, bias):
        s = s_ref[slot]
        if bias is not None:
            s = s + bias
        m_prev = m_ref[...]
        m_new = jnp.maximum(m_prev, jnp.max(s, axis=0, keepdims=True))
        a = jnp.exp2(m_prev - m_new)
        p = jnp.exp2(s - m_new)
        l_ref[...] = a * l_ref[...] + jnp.sum(p, axis=0, keepdims=True)
        m_ref[...] = m_new
        a_ref[slot] = a
        p_ref[slot] = p.astype(_BF16)

    def values(b, slot):
        vb = vt_ref[jnp.maximum(b, 0)]
        acc_ref[...] = a_ref[slot] * acc_ref[...] + jnp.dot(vb, p_ref[slot], preferred_element_type=_F32)

    def step(qi, t, slot):
        values(t - 1, 1 - slot)
        softmax(slot, None)
        scores(qi, t + 1, 1 - slot)

    def q_block(qi, carry):
        m_ref[...] = jnp.full(m_ref.shape, NEG_INF, _F32)
        n_full = lax.div(qi, K_TILE // Q_TILE)

        done = 0
        unroll = STEP_UNROLL
        while unroll > 1:
            def unrolled(u, c, unroll=unroll, done=done):
                for i in range(unroll):
                    step(qi, done + unroll * u + i, i & 1)
                return c

            trips = lax.div(n_full - done, unroll)
            lax.fori_loop(0, trips, unrolled, 0)
            done = done + trips * unroll
            unroll //= 2

        def single(t, c):
            step(qi, t, t & 1)
            return c

        lax.fori_loop(done, n_full, single, 0)

        last = n_full & 1
        values(n_full - 1, 1 - last)
        softmax(last, bias_ref[qi - n_full * (K_TILE // Q_TILE)])
        nxt = jnp.minimum(qi + 1, n_q_blocks - 1)
        load_queries(nxt)
        scores(nxt, 0, 0)
        values(n_full, last)

        o2 = acc_ref[...] / l_ref[...]
        o = o2[:, 0:Q_TILE] - lam * o2[:, Q_TILE:n_cols]
        norm = lax.rsqrt(jnp.mean(o * o, axis=0, keepdims=True) + LN_EPS)
        o_ref[qi] = (o * norm * sg_ref[...] * (1.0 - lam_init)).astype(o_ref.dtype)
        return carry

    load_queries(0)
    scores(0, 0, 0)
    lax.fori_loop(0, n_q_blocks, q_block, 0)


def _flash_prompt(q, k, v, lam, sg, *, layer, lam_init):
    t = q.shape[0]
    assert t % K_TILE == 0 and K_TILE % Q_TILE == 0 and Q_TILE % CHUNK == 0 and STEP_UNROLL % 2 == 0
    nq, nk = t // Q_TILE, t // K_TILE
    qt = jnp.transpose(q.reshape(nq, Q_TILE, N_HEADS, HEAD_W), (2, 0, 3, 1))
    vt = jnp.transpose(v.reshape(nk, K_TILE, N_HEADS, V_DIM), (2, 0, 3, 1))
    sub = K_TILE // Q_TILE
    key_chunk = (jnp.arange(K_TILE) // CHUNK)[None, :, None]
    query = jnp.arange(sub)[:, None, None] * Q_TILE + (jnp.arange(2 * Q_TILE) % Q_TILE)[None, None, :]
    tail_bias = jnp.where(key_chunk <= query // CHUNK, 0.0, NEG_INF).astype(_F32)
    kern = functools.partial(_flash_kernel, lam_init=lam_init, n_q_blocks=nq)
    out = pl.pallas_call(
        kern,
        grid=(N_HEADS,),
        in_specs=[pl.BlockSpec((None, nq, HEAD_W, Q_TILE), lambda h: (h, 0, 0, 0)),
                  pl.BlockSpec((t, HEAD_W), lambda h: (0, h)),
                  pl.BlockSpec((None, nk, V_DIM, K_TILE), lambda h: (h, 0, 0, 0)),
                  pl.BlockSpec((sub, K_TILE, 2 * Q_TILE), lambda h: (0, 0, 0)),
                  pl.BlockSpec((None, 4, QK_DIM), lambda h: (layer, 0, 0)),
                  pl.BlockSpec((None, V_DIM, 1), lambda h: (layer, 0, 0))],
        out_specs=pl.BlockSpec((None, nq, V_DIM, Q_TILE), lambda h: (h, 0, 0, 0)),
        out_shape=jax.ShapeDtypeStruct((N_HEADS, nq, V_DIM, Q_TILE), _BF16),
        scratch_shapes=[pltpu.VMEM((2, HEAD_W, 2 * Q_TILE), _BF16),
                        pltpu.VMEM((2, K_TILE, 2 * Q_TILE), _F32),
                        pltpu.VMEM((2, K_TILE, 2 * Q_TILE), _BF16),
                        pltpu.VMEM((2, 1, 2 * Q_TILE), _F32),
                        pltpu.VMEM((1, 2 * Q_TILE), _F32),
                        pltpu.VMEM((1, 2 * Q_TILE), _F32),
                        pltpu.VMEM((V_DIM, 2 * Q_TILE), _F32)],
        compiler_params=_params(1),
        name="flash_prompt",
    )(qt, k, vt, tail_bias, lam, sg.reshape(sg.shape[0], V_DIM, 1))
    return jnp.transpose(out, (1, 3, 0, 2)).reshape(t, N_HEADS * V_DIM)


def _sample_attn_kernel(q_ref, kn_ref, vn_ref, kc_ref, vc_ref, lam_ref, sg_ref, o_ref, *, lam_init, steps):
    heads = SAMPLE_HEADS_PER_STEP
    per_head = 2 * steps
    past = kc_ref.shape[1]
    lam = _diff_lambda(lam_ref, lam_init)
    head_cols = [slice(h * HEAD_W, (h + 1) * HEAD_W) for h in range(heads)]
    head_rows = [slice(h * per_head, (h + 1) * per_head) for h in range(heads)]

    qz = jnp.concatenate([_split_parts(q_ref[:, c]) for c in head_cols], axis=0)
    kc = kc_ref[0].reshape(past * heads, HEAD_W).astype(_BF16)
    vc = vc_ref[0].reshape(past * heads, V_DIM).astype(_BF16)
    sc = _score(qz, kc)
    row_head = lax.shift_right_logical(lax.broadcasted_iota(jnp.int32, (heads * per_head, 1), 0),
                                       int(math.log2(per_head)))
    col_head = lax.broadcasted_iota(jnp.int32, (1, past * heads), 1) & (heads - 1)
    sc = jnp.where(row_head == col_head, sc, NEG_INF)
    sn = jnp.concatenate([_score(qz[r, :], kn_ref[:, c]) for r, c in zip(head_rows, head_cols)], axis=0)
    m = jnp.maximum(jnp.max(sc, axis=1, keepdims=True), jnp.max(sn, axis=1, keepdims=True))
    pc = jnp.exp2(sc - m)
    pn = jnp.exp2(sn - m)
    l = jnp.sum(pc, axis=1, keepdims=True) + jnp.sum(pn, axis=1, keepdims=True)
    pn = pn.astype(_BF16)
    on = jnp.concatenate([jnp.dot(pn[r, :], vn_ref[:, c], preferred_element_type=_F32)
                          for r, c in zip(head_rows, head_cols)], axis=0)
    o2 = (jnp.dot(pc.astype(_BF16), vc, preferred_element_type=_F32) + on) / l
    for h in range(heads):
        first = h * per_head
        o = o2[first:first + steps, :] - lam * o2[first + steps:first + per_head, :]
        o_ref[:, head_cols[h]] = _sub_norm(o, sg_ref, lam_init).astype(o_ref.dtype)


def _sample_attn(q, k_new, v_new, cache_k, cache_v, lam, sg, *, layer, lam_init, seqs):
    m, width = q.shape
    steps = m // seqs
    past = cache_k.shape[1]
    group_w = SAMPLE_HEADS_PER_STEP * HEAD_W
    groups = width // group_w
    kern = functools.partial(_sample_attn_kernel, lam_init=lam_init, steps=steps)
    new = pl.BlockSpec((steps, group_w), lambda b, g: (0, b * groups + g))
    cache = pl.BlockSpec((1, past, SAMPLE_HEADS_PER_STEP, HEAD_W), lambda b, g: (b, 0, g, 0))
    out = pl.pallas_call(
        kern,
        grid=(seqs, groups),
        in_specs=[new, new, new, cache, cache,
                  pl.BlockSpec((None, 4, QK_DIM), lambda b, g: (layer, 0, 0)),
                  pl.BlockSpec((None, 1, V_DIM), lambda b, g: (layer, 0, 0))],
        out_specs=new,
        out_shape=jax.ShapeDtypeStruct((steps, seqs * width), _BF16),
        compiler_params=_params(2),
        name="sample_attn",
    )(q.reshape(steps, seqs * width), k_new.reshape(steps, seqs * width),
      v_new.reshape(steps, seqs * width), cache_k, cache_v, lam, sg)
    return out.reshape(m, width)


def _oproj_ln_kernel(x_ref, a_ref, w_ref, g_ref, b_ref, o_ref, ob_ref):
    mix = jnp.dot(a_ref[...], w_ref[...], preferred_element_type=_F32)
    out = _layer_norm(ALPHA * x_ref[...] + mix, g_ref[...], b_ref[...])
    o_ref[...] = out
    ob_ref[...] = out.astype(_BF16)


def _oproj_ln(x, attn, w_o, g, b, *, ln_layer):
    m = x.shape[0]
    rows = min(ROW_TILE, m)
    assert m % rows == 0
    const = lambda i: (ln_layer, 0, 0)
    return pl.pallas_call(
        _oproj_ln_kernel,
        grid=(m // rows,),
        in_specs=[pl.BlockSpec((rows, D_MODEL), lambda i: (i, 0)),
                  pl.BlockSpec((rows, N_HEADS * V_DIM), lambda i: (i, 0)),
                  pl.BlockSpec((N_HEADS * V_DIM, D_MODEL), lambda i: (0, 0)),
                  pl.BlockSpec((None, 1, D_MODEL), const),
                  pl.BlockSpec((None, 1, D_MODEL), const)],
        out_specs=[pl.BlockSpec((rows, D_MODEL), lambda i: (i, 0))] * 2,
        out_shape=[jax.ShapeDtypeStruct((m, D_MODEL), _F32), jax.ShapeDtypeStruct((m, D_MODEL), _BF16)],
        compiler_params=_params(1),
        name="oproj_ln",
    )(x, attn, w_o, g, b)


def _run_group(x, pool_hist, conv_hist, caches, pos0, seqs, p):
    layer_inputs, new_conv = [], []
    k_f32 = v_f32 = k_b = v_b = xb = None
    for l in range(DEPTH):
        if l < N_A_LAYERS:
            layer_inputs.append(x)
            x, xb = _pool_ln(x, pool_hist[l], p['w_pool'], p['pool_scale'], p['ln1_g'], p['ln1_b'],
                             layer=l, seqs=seqs, pos0=pos0)
        else:
            j = l - N_A_LAYERS
            lam_init = 0.8 - 0.6 * math.exp(-0.3 * l)
            (q,) = _proj(xb, p['w_q%d' % j], (_BF16,), n=N_HEADS * HEAD_W, scale=Q_SCALE)
            if caches is None:
                attn = _flash_prompt(q, k_b, v_b, p['lam'], p['subln_g'], layer=j, lam_init=lam_init)
            else:
                attn = _sample_attn(q, k_b, v_b, caches[0], caches[1], p['lam'], p['subln_g'],
                                    layer=j, lam_init=lam_init, seqs=seqs)
            x, xb = _oproj_ln(x, attn, p['w_o%d' % j], p['ln1_g'], p['ln1_b'], ln_layer=l)
        bf16_copy = N_A_LAYERS - 1 <= l < DEPTH - 1
        cast_next = l + 1 < DEPTH and len(p['w_up']) == l + 1
        side = [(name, spec) for name, spec in p['to_cast'].get(l, []) if name not in p]
        outs = _ffn_ln(xb, x, conv_hist[l], p['w_up'][l], p['conv_w'], p['conv_b'], p['w_down'][l],
                       p['ln2_g'], p['ln2_b'], layer=l, seqs=seqs, bf16_copy=bf16_copy,
                       next_weights=(p['w_up_f32'], p['w_down_f32']) if cast_next else None,
                       side_casts=[spec for _, spec in side])
        for (name, (blocks, _, _)), cast in zip(side, outs[len(outs) - len(side):]):
            p[name] = cast.reshape(-1, blocks.shape[-1])
        outs = outs[:len(outs) - len(side)]
        if cast_next:
            *outs, w_up_next, w_down_next = outs
            p['w_up'].append(w_up_next)
            p['w_down'].append(w_down_next)
        x, *xb, new_a, new_v = outs
        xb = xb[0] if xb else None
        new_conv.append(jnp.concatenate([_untile_cols(new_a), _untile_cols(new_v)], axis=1))
        if l == N_A_LAYERS - 1:
            k_width = N_HEADS * HEAD_W
            k_f32, k_b = _proj(xb, p['w_kv'], (_F32, _BF16), n=k_width)
            v_f32, v_b = _proj(xb, p['w_kv'], (_F32, _BF16), n=N_HEADS * V_DIM, col0=k_width)
    return x, layer_inputs, new_conv, k_f32, v_f32


def _untile_cols(a):
    tiles, rows, w = a.shape
    return jnp.transpose(a, (1, 0, 2)).reshape(rows, tiles * w)


def _to_time_major(a):
    seqs, steps, w = a.shape
    return jnp.transpose(a, (1, 0, 2)).reshape(steps * seqs, w)


def _to_batch_major(a, seqs):
    return jnp.swapaxes(a.reshape(a.shape[0] // seqs, seqs, *a.shape[1:]), 0, 1)


def kernel(x_prompt, x_sample, state_pool, state_ffn_conv, cache_k, cache_v, ln1_g, ln1_b, ln2_g, ln2_b,
           w_pool, pool_scale, w_up, conv_w, conv_b, w_down, w_kv, w_q, lam, subln_g, w_o):
    rows = lambda a: a[:, None, :]
    p = dict(ln1_g=rows(ln1_g), ln1_b=rows(ln1_b), ln2_g=rows(ln2_g), ln2_b=rows(ln2_b),
             pool_scale=rows(pool_scale), conv_w=conv_w, conv_b=rows(conv_b), lam=lam, subln_g=rows(subln_g),
             w_pool=w_pool.astype(_BF16), w_up_f32=w_up, w_down_f32=w_down,
             w_up=[w_up[0].astype(_BF16)], w_down=[w_down[0].astype(_BF16)])

    ffn_steps = (x_prompt.shape[1] // ROW_TILE) * N_FF_TILES
    block_elems = CAST_BLOCK_ELEMS
    while w_kv.size // block_elems > ffn_steps:
        block_elems *= 2

    def blocks(w):
        return w.reshape(-1, block_elems // w.shape[-1], w.shape[-1])

    per_layer = D_MODEL * N_HEADS * HEAD_W // block_elems
    kv_blocks, q_blocks, o_blocks = blocks(w_kv), blocks(w_q), blocks(w_o)
    p['to_cast'] = {N_A_LAYERS - 2: [('w_kv', (kv_blocks, 0, kv_blocks.shape[0]))]}
    for j in range(DEPTH - N_A_LAYERS):
        p['to_cast'][N_A_LAYERS - 1 + j] = [('w_q%d' % j, (q_blocks, j * per_layer, per_layer)),
                                            ('w_o%d' % j, (o_blocks, j * per_layer, per_layer))]

    def finish(x, layer_inputs, new_conv, k_new, v_new, seqs):
        steps = x.shape[0] // seqs
        y = _to_batch_major(x, seqs)
        pool = jnp.stack([_to_batch_major(u[(steps - POOL_HIST) * seqs:], seqs) for u in layer_inputs])
        conv = jnp.stack([_to_batch_major(c, seqs) for c in new_conv])
        return y, pool, conv, _to_batch_major(k_new, seqs), _to_batch_major(v_new, seqs)

    b, t, _ = x_prompt.shape
    assert b == 1 and t >= POOL_HIST
    zeros_pool = jnp.zeros(((POOL_HIST + 1) * b, D_MODEL), _F32)
    zeros_conv = jnp.zeros(((CONV_W - 1) * b, 2 * D_FF), _F32)
    out_p = _run_group(_to_time_major(x_prompt), [zeros_pool] * N_A_LAYERS, [zeros_conv] * DEPTH,
                       None, 0, b, p)
    y_p, pool_p, conv_p, k_p, v_p = finish(*out_p, b)

    sb, steps, _ = x_sample.shape
    past = cache_k.shape[1]
    assert steps >= POOL_HIST
    pad = jnp.zeros((sb, D_MODEL), _F32)
    pool_hist = [jnp.concatenate([pad, _to_time_major(state_pool[l])], axis=0) for l in range(N_A_LAYERS)]
    conv_hist = [_to_time_major(state_ffn_conv[l]) for l in range(DEPTH)]
    out_s = _run_group(_to_time_major(x_sample), pool_hist, conv_hist, (cache_k, cache_v), past, sb, p)
    y_s, pool_s, conv_s, k_s, v_s = finish(*out_s, sb)

    return (y_p, y_s, pool_p, pool_s, conv_p, conv_s, k_p, v_p, k_s, v_s)
```

```python
import functools
import math

import jax
import jax.numpy as jnp
from jax import lax
from jax.experimental import pallas as pl
from jax.experimental.pallas import tpu as pltpu

D_MODEL = 2048
DEPTH = 4
N_A_LAYERS = DEPTH // 2
CHUNK = 64
POOL_WINDOWS = (2, 4, 8, 16)
POOL_GROUP = D_MODEL // len(POOL_WINDOWS)
POOL_HIST = max(POOL_WINDOWS) - 1
N_HEADS = 16
QK_DIM = 64
V_DIM = 2 * QK_DIM
HEAD_W = 2 * QK_DIM
D_FF = 5632
CONV_W = 3
LN_EPS = 1e-5
ALPHA = (2 * DEPTH) ** 0.25
ATTN_SCALE = QK_DIM ** -0.5
Q_SCALE = ATTN_SCALE * math.log2(math.e)
NEG_INF = -1e30

V7X_VMEM_BYTES = 64 * 1024 * 1024
VMEM_LIMIT_BYTES = V7X_VMEM_BYTES - 6 * 1024 * 1024
SUBLANES = 8

ROW_TILE = 512
FF_TILE = 512
CAST_BLOCK_ELEMS = 64 * 1024
N_FF_TILES = D_FF // FF_TILE
MM_COL_TILE = 1024
Q_TILE = 256
K_TILE = 512
STEP_UNROLL = 4
SAMPLE_HEADS_PER_STEP = 8

_F32 = jnp.float32
_BF16 = jnp.bfloat16


def _params(n_axes):
    return pltpu.CompilerParams(dimension_semantics=("arbitrary",) * n_axes,
                                vmem_limit_bytes=VMEM_LIMIT_BYTES)


def _layer_norm(z, g, b):
    mu = jnp.mean(z, axis=-1, keepdims=True)
    zc = z - mu
    var = jnp.mean(zc * zc, axis=-1, keepdims=True)
    return zc * lax.rsqrt(var + LN_EPS) * g + b


def _gelu_tanh(x):
    cdf = 0.5 * (1.0 + jnp.tanh(math.sqrt(2.0 / math.pi) * (x + 0.044715 * (x * x * x))))
    return x * cdf


def _pool_window_starts(seqs):
    starts, lo, shift = [], 0, seqs
    for _ in POOL_WINDOWS:
        lo = -(-(lo + shift) // SUBLANES) * SUBLANES
        starts.append(lo)
        shift *= 2
    return starts


def _pool_ln_kernel(x_ref, halo_ref, hist_ref, w_ref, scale_ref, g_ref, b_ref, o_ref, ob_ref,
                    ext_ref, sums_ref, z_ref, *, seqs, rows, pos0):
    i = pl.program_id(0)
    halo_rows = ext_ref.shape[0] - rows
    hist_rows = hist_ref.shape[0]

    @pl.when(i == 0)
    def _():
        if halo_rows > hist_rows:
            ext_ref[0:halo_rows - hist_rows, :] = jnp.zeros((halo_rows - hist_rows, D_MODEL), _F32)
        ext_ref[halo_rows - hist_rows:halo_rows, :] = hist_ref[...]

    @pl.when(i > 0)
    def _():
        ext_ref[0:halo_rows, :] = halo_ref[...]

    ext_ref[halo_rows:halo_rows + rows, :] = x_ref[...]

    row = lax.broadcasted_iota(jnp.int32, (rows, 1), 0) + i * rows
    step = lax.shift_right_logical(row, int(math.log2(seqs)))
    pos_plus_1 = (step + (pos0 + 1)).astype(_F32)

    starts = _pool_window_starts(seqs)
    assert POOL_WINDOWS == tuple(2 ** (k + 1) for k in range(len(POOL_WINDOWS))) and starts[-1] <= halo_rows
    window_sums, src, shift = [], ext_ref, seqs
    for k in range(len(POOL_WINDOWS)):
        last = k == len(POOL_WINDOWS) - 1
        first = halo_rows if last else starts[k]
        n = halo_rows + rows - first
        cols = slice(k * POOL_GROUP, D_MODEL)
        total = src[first:first + n, cols] + src[first - shift:first - shift + n, cols]
        if last:
            window_sums.append(total)
        else:
            sums_ref[k, first:first + n, cols] = total
            window_sums.append(None)
            src, shift = sums_ref.at[k], 2 * shift

    for g, w in enumerate(POOL_WINDOWS):
        cols = slice(g * POOL_GROUP, (g + 1) * POOL_GROUP)
        cur = x_ref[:, cols]
        win = window_sums[g] if window_sums[g] is not None else sums_ref[g, halo_rows:halo_rows + rows, cols]
        cnt = jnp.minimum(float(w), pos_plus_1)
        d = win / cnt - cur
        mix = jnp.dot(d.astype(_BF16), w_ref[g], preferred_element_type=_F32)
        z_ref[:, cols] = ALPHA * cur + mix * scale_ref[:, cols]

    out = _layer_norm(z_ref[...], g_ref[...], b_ref[...])
    o_ref[...] = out
    ob_ref[...] = out.astype(_BF16)


def _pool_ln(x, hist, w_pool, scale, g, b, *, layer, seqs, pos0):
    m = x.shape[0]
    rows = min(ROW_TILE, m)
    hist_rows = (POOL_HIST + 1) * seqs
    halo_rows = max(hist_rows, _pool_window_starts(seqs)[-1])
    assert m % rows == 0 and rows % halo_rows == 0 and seqs & (seqs - 1) == 0 and hist.shape[0] == hist_rows
    halo_blocks_per_tile = rows // halo_rows
    kern = functools.partial(_pool_ln_kernel, seqs=seqs, rows=rows, pos0=pos0)
    return pl.pallas_call(
        kern,
        grid=(m // rows,),
        in_specs=[
            pl.BlockSpec((rows, D_MODEL), lambda i: (i, 0)),
            pl.BlockSpec((halo_rows, D_MODEL),
                         lambda i: (jnp.maximum(i * halo_blocks_per_tile - 1, 0), 0)),
            pl.BlockSpec((hist_rows, D_MODEL), lambda i: (0, 0)),
            pl.BlockSpec((None, len(POOL_WINDOWS), POOL_GROUP, POOL_GROUP), lambda i: (layer, 0, 0, 0)),
            pl.BlockSpec((None, 1, D_MODEL), lambda i: (layer, 0, 0)),
            pl.BlockSpec((None, 1, D_MODEL), lambda i: (layer, 0, 0)),
            pl.BlockSpec((None, 1, D_MODEL), lambda i: (layer, 0, 0)),
        ],
        out_specs=[pl.BlockSpec((rows, D_MODEL), lambda i: (i, 0))] * 2,
        out_shape=[jax.ShapeDtypeStruct((m, D_MODEL), _F32), jax.ShapeDtypeStruct((m, D_MODEL), _BF16)],
        scratch_shapes=[pltpu.VMEM((halo_rows + rows, D_MODEL), _F32),
                        pltpu.VMEM((len(POOL_WINDOWS) - 1, halo_rows + rows, D_MODEL), _F32),
                        pltpu.VMEM((rows, D_MODEL), _F32)],
        compiler_params=_params(1),
        name="pool_ln",
    )(x, x, hist, w_pool, scale, g, b)


def _ffn_ln_kernel(xb_ref, x_ref, hist_a_ref, hist_v_ref, wa_ref, wv_ref, cwa_ref, cwv_ref, cba_ref, cbv_ref,
                   wd_ref, g_ref, b_ref, *refs, seqs, rows, n_items, bf16_copy, cast_next, side_blocks):
    refs = list(refs)
    s = pl.program_id(0)
    next_up_ref, next_down_ref = (refs.pop(0), refs.pop(0)) if cast_next else (None, None)
    side_in = [refs.pop(0) for _ in side_blocks]
    o_ref = refs.pop(0)
    ob_ref = refs.pop(0) if bf16_copy else None
    new_a_ref, new_v_ref = refs.pop(0), refs.pop(0)
    next_up_b_ref, next_down_b_ref = (refs.pop(0), refs.pop(0)) if cast_next else (None, None)
    side_out = [refs.pop(0) for _ in side_blocks]
    ext_a_ref, ext_v_ref = refs
    if cast_next:
        next_up_b_ref[...] = next_up_ref[...].astype(_BF16)
        next_down_b_ref[...] = next_down_ref[...].astype(_BF16)
    for src_ref, dst_ref, n_blocks in zip(side_in, side_out, side_blocks):
        @pl.when(s < n_blocks)
        def _(src_ref=src_ref, dst_ref=dst_ref):
            dst_ref[...] = src_ref[...].astype(_BF16)

    halo = (CONV_W - 1) * seqs
    a_item = jnp.minimum(s, n_items - 1)
    i_a = lax.div(a_item, N_FF_TILES)
    j_a = a_item - i_a * N_FF_TILES
    j_b = lax.rem(jnp.maximum(s - 1, 0), N_FF_TILES)
    parity = s & 1

    @pl.when(s == 0)
    def _():
        ext_a_ref[1] = jnp.zeros(ext_a_ref.shape[1:], _F32)
        ext_v_ref[1] = jnp.zeros(ext_v_ref.shape[1:], _F32)

    n_copies = ext_a_ref.shape[1]
    delays = range(CONV_W)
    if n_copies == 1:
        copy_of, write_at, read_at = [0] * CONV_W, [halo] * CONV_W, [halo - d * seqs for d in delays]
    else:
        copy_of, write_at, read_at = list(delays), [d * seqs for d in delays], [0] * CONV_W

    def fill_history(src_a, src_v):
        for c in range(n_copies):
            n = write_at[copy_of.index(c)]
            if n:
                ext_a_ref[parity, c, 0:n, :] = src_a[halo - n:halo, :]
                ext_v_ref[parity, c, 0:n, :] = src_v[halo - n:halo, :]

    @pl.when(i_a == 0)
    def _():
        fill_history(hist_a_ref[...], hist_v_ref[...])

    @pl.when(i_a > 0)
    def _():
        fill_history(new_a_ref[j_a], new_v_ref[j_a])

    @pl.when(j_b == 0)
    def _():
        o_ref[...] = jnp.zeros_like(o_ref)

    def both_halves(slot):
        xb = xb_ref[...]
        ha = jnp.dot(xb, wa_ref[...], preferred_element_type=_F32)
        hv = jnp.dot(xb, wv_ref[...], preferred_element_type=_F32)
        for c in range(n_copies):
            at = write_at[copy_of.index(c)]
            ext_a_ref[slot, c, at:at + rows, :] = ha
            ext_v_ref[slot, c, at:at + rows, :] = hv
        new_a_ref[j_a] = ha[rows - halo:, :]
        new_v_ref[j_a] = hv[rows - halo:, :]

        def conv(ext_ref, cw_ref, cb_ref):
            cw = cw_ref[...]
            c = cb_ref[...]
            for tap in range(CONV_W):
                d = CONV_W - 1 - tap
                c = c + cw[tap:tap + 1, :] * ext_ref[1 - slot, copy_of[d], read_at[d]:read_at[d] + rows, :]
            return c

        ca = conv(ext_a_ref, cwa_ref, cba_ref)
        cv = conv(ext_v_ref, cwv_ref, cbv_ref)
        act = (_gelu_tanh(ca) * cv).astype(_BF16)
        o_ref[...] += jnp.dot(act, wd_ref[...], preferred_element_type=_F32)

    for slot in (0, 1):
        pl.when(parity == slot)(functools.partial(both_halves, slot))

    @pl.when(jnp.logical_and(s > 0, j_b == N_FF_TILES - 1))
    def _():
        out = _layer_norm(ALPHA * x_ref[...] + o_ref[...], g_ref[...], b_ref[...])
        o_ref[...] = out
        if ob_ref is not None:
            ob_ref[...] = out.astype(_BF16)


def _ffn_ln(xb, x, hist, w_up, conv_w, conv_b, w_down, g, b, *, layer, seqs, bf16_copy, next_weights=None,
            side_casts=()):
    m = x.shape[0]
    rows = min(ROW_TILE, m)
    halo = (CONV_W - 1) * seqs
    ext_rows = -(-(rows + halo) // SUBLANES) * SUBLANES
    n_copies = 1 if seqs % SUBLANES == 0 else CONV_W
    assert m % rows == 0 and rows >= halo
    n_items = (m // rows) * N_FF_TILES
    kern = functools.partial(_ffn_ln_kernel, seqs=seqs, rows=rows, n_items=n_items, bf16_copy=bf16_copy,
                             cast_next=next_weights is not None, side_blocks=tuple(n for _, _, n in side_casts))

    def up_item(s):
        item = jnp.minimum(s, n_items - 1)
        return lax.div(item, N_FF_TILES), lax.rem(item, N_FF_TILES)

    def down_item(s):
        item = jnp.maximum(s - 1, 0)
        return lax.div(item, N_FF_TILES), lax.rem(item, N_FF_TILES)

    up_a = lambda s: (0, up_item(s)[1])
    up_v = lambda s: (0, up_item(s)[1] + N_FF_TILES)
    conv_w_a = lambda s: (layer, 0, down_item(s)[1])
    conv_w_v = lambda s: (layer, 0, down_item(s)[1] + N_FF_TILES)
    ln_row = lambda s: (layer, 0, 0)
    down_rows = pl.BlockSpec((rows, D_MODEL), lambda s: (down_item(s)[0], 0))
    new_hist = pl.BlockSpec((N_FF_TILES, halo, FF_TILE), lambda s: (0, 0, 0))
    row_out = [jax.ShapeDtypeStruct((m, D_MODEL), _F32)]
    if bf16_copy:
        row_out.append(jax.ShapeDtypeStruct((m, D_MODEL), _BF16))
    cast_in, cast_specs, cast_out = [], [], []
    if next_weights is not None:
        n_row_tiles = m // rows
        up_blk = (D_MODEL // n_row_tiles, 2 * D_FF // N_FF_TILES)
        down_blk = (D_FF // n_items, D_MODEL)
        assert up_blk[0] * n_row_tiles == D_MODEL and down_blk[0] * n_items == D_FF
        assert up_blk[0] % (2 * SUBLANES) == 0 and down_blk[0] % (2 * SUBLANES) == 0
        cast_in = list(next_weights)
        cast_specs = [pl.BlockSpec((None,) + up_blk, lambda s: (layer + 1,) + up_item(s)),
                      pl.BlockSpec((None,) + down_blk, lambda s: (layer + 1, jnp.minimum(s, n_items - 1), 0))]
        cast_out = [(pl.BlockSpec(up_blk, up_item), jax.ShapeDtypeStruct((D_MODEL, 2 * D_FF), _BF16)),
                    (pl.BlockSpec(down_blk, lambda s: (jnp.minimum(s, n_items - 1), 0)),
                     jax.ShapeDtypeStruct((D_FF, D_MODEL), _BF16))]
    for blocks, first, count in side_casts:
        assert count <= n_items
        blk = (None,) + blocks.shape[1:]
        cast_in.append(blocks)
        cast_specs.append(pl.BlockSpec(blk, lambda s, first=first, count=count: (first + jnp.minimum(s, count - 1), 0, 0)))
        cast_out.append((pl.BlockSpec(blk, lambda s, count=count: (jnp.minimum(s, count - 1), 0, 0)),
                         jax.ShapeDtypeStruct((count,) + blocks.shape[1:], _BF16)))
    return pl.pallas_call(
        kern,
        grid=(n_items + 1,),
        in_specs=[
            pl.BlockSpec((rows, D_MODEL), lambda s: (up_item(s)[0], 0)),
            down_rows,
            pl.BlockSpec((halo, FF_TILE), up_a),
            pl.BlockSpec((halo, FF_TILE), up_v),
            pl.BlockSpec((D_MODEL, FF_TILE), up_a),
            pl.BlockSpec((D_MODEL, FF_TILE), up_v),
            pl.BlockSpec((None, CONV_W, FF_TILE), conv_w_a),
            pl.BlockSpec((None, CONV_W, FF_TILE), conv_w_v),
            pl.BlockSpec((None, 1, FF_TILE), conv_w_a),
            pl.BlockSpec((None, 1, FF_TILE), conv_w_v),
            pl.BlockSpec((FF_TILE, D_MODEL), lambda s: (down_item(s)[1], 0)),
            pl.BlockSpec((None, 1, D_MODEL), ln_row),
            pl.BlockSpec((None, 1, D_MODEL), ln_row),
        ] + cast_specs,
        out_specs=[down_rows] * len(row_out) + [new_hist, new_hist] + [spec for spec, _ in cast_out],
        out_shape=(row_out + [jax.ShapeDtypeStruct((N_FF_TILES, halo, FF_TILE), _F32)] * 2
                   + [shape for _, shape in cast_out]),
        scratch_shapes=[
            pltpu.VMEM((2, n_copies, ext_rows, FF_TILE), _F32),
            pltpu.VMEM((2, n_copies, ext_rows, FF_TILE), _F32),
        ],
        compiler_params=_params(1),
        name="ffn_ln",
    )(xb, x, hist, hist, w_up, w_up, conv_w, conv_w, conv_b, conv_b, w_down, g, b, *cast_in)


def _proj_kernel(x_ref, w_ref, *out_refs, scale):
    r = jnp.dot(x_ref[...], w_ref[...], preferred_element_type=_F32)
    for o_ref in out_refs:
        if o_ref.dtype == _F32:
            o_ref[...] = r.reshape(o_ref.shape)
        else:
            o_ref[...] = (r * scale).astype(o_ref.dtype)


def _proj(x, w, out_dtypes, *, n, col0=0, scale=1.0):
    m, k = x.shape
    rows = min(2 * ROW_TILE, m)
    cols = min(MM_COL_TILE, n)
    assert m % rows == 0 and n % cols == 0 and col0 % cols == 0 and cols % HEAD_W == 0
    w_spec = pl.BlockSpec((k, cols), lambda i, j: (0, j + col0 // cols))
    flat = pl.BlockSpec((rows, cols), lambda i, j: (i, j)), (m, n)
    heads = pl.BlockSpec((rows, cols // HEAD_W, HEAD_W), lambda i, j: (i, j, 0)), (m, n // HEAD_W, HEAD_W)
    outs = [heads if dt == _F32 else flat for dt in out_dtypes]
    return pl.pallas_call(
        functools.partial(_proj_kernel, scale=scale),
        grid=(m // rows, n // cols),
        in_specs=[pl.BlockSpec((rows, k), lambda i, j: (i, 0)), w_spec],
        out_specs=[spec for spec, _ in outs],
        out_shape=[jax.ShapeDtypeStruct(shape, dt) for (_, shape), dt in zip(outs, out_dtypes)],
        compiler_params=_params(2),
        name="proj",
    )(x, w)


def _kv_proj_kernel(x_ref, w_ref, kf_ref, kb_ref, vf_ref, vb_ref, *, k_tiles):
    j = pl.program_id(1)
    r = jnp.dot(x_ref[...], w_ref[...], preferred_element_type=_F32)

    @pl.when(j < k_tiles)
    def _():
        kf_ref[...] = r.reshape(kf_ref.shape)
        kb_ref[...] = r.astype(_BF16)

    @pl.when(j >= k_tiles)
    def _():
        vf_ref[...] = r.reshape(vf_ref.shape)
        vb_ref[...] = r.astype(_BF16)


def _kv_proj(x, w, *, k_width):
    m, k = x.shape
    v_width = w.shape[1] - k_width
    rows, cols = min(2 * ROW_TILE, m), MM_COL_TILE
    assert m % rows == 0 and k_width % cols == 0 and v_width % cols == 0 and cols % HEAD_W == 0
    k_tiles = k_width // cols
    k_col = lambda j: jnp.minimum(j, k_tiles - 1)
    v_col = lambda j: jnp.maximum(j - k_tiles, 0)
    heads = (rows, cols // HEAD_W, HEAD_W)
    return pl.pallas_call(
        functools.partial(_kv_proj_kernel, k_tiles=k_tiles),
        grid=(m // rows, (k_width + v_width) // cols),
        in_specs=[pl.BlockSpec((rows, k), lambda i, j: (i, 0)),
                  pl.BlockSpec((k, cols), lambda i, j: (0, j))],
        out_specs=[pl.BlockSpec(heads, lambda i, j: (i, k_col(j), 0)),
                   pl.BlockSpec((rows, cols), lambda i, j: (i, k_col(j))),
                   pl.BlockSpec(heads, lambda i, j: (i, v_col(j), 0)),
                   pl.BlockSpec((rows, cols), lambda i, j: (i, v_col(j)))],
        out_shape=[jax.ShapeDtypeStruct((m, k_width // HEAD_W, HEAD_W), _F32),
                   jax.ShapeDtypeStruct((m, k_width), _BF16),
                   jax.ShapeDtypeStruct((m, v_width // HEAD_W, HEAD_W), _F32),
                   jax.ShapeDtypeStruct((m, v_width), _BF16)],
        compiler_params=_params(2),
        name="kv_proj",
    )(x, w)


def _diff_lambda(lam_ref, lam_init):
    lam = lam_ref[...]
    e1 = jnp.exp(jnp.sum(lam[0:1, :] * lam[1:2, :], axis=-1, keepdims=True))
    e2 = jnp.exp(jnp.sum(lam[2:3, :] * lam[3:4, :], axis=-1, keepdims=True))
    return e1 - e2 + lam_init


def _split_parts(q):
    first = lax.broadcasted_iota(jnp.int32, (1, HEAD_W), 1) < QK_DIM
    zero = jnp.zeros_like(q)
    return jnp.concatenate([jnp.where(first, q, zero), jnp.where(first, zero, q)], axis=0)


def _sub_norm(o, sg_ref, lam_init):
    return o * lax.rsqrt(jnp.mean(o * o, axis=-1, keepdims=True) + LN_EPS) * sg_ref[...] * (1.0 - lam_init)


def _score(qz, kb):
    return lax.dot_general(qz, kb, (((1,), (1,)), ((), ())), preferred_element_type=_F32)


def _flash_kernel(qt_ref, k_ref, vt_ref, bias_ref, lam_ref, sg_ref, o_ref,
                  qz_ref, s_ref, p_ref, a_ref, m_ref, l_ref, acc_ref, *, lam_init, n_q_blocks):
    lam = _diff_lambda(lam_ref, lam_init)
    n_cols = 2 * Q_TILE
    first_part = lax.broadcasted_iota(jnp.int32, (HEAD_W, 1), 0) < QK_DIM

    l_ref[...] = jnp.zeros_like(l_ref)
    acc_ref[...] = jnp.zeros_like(acc_ref)
    p_ref[1] = jnp.zeros(p_ref.shape[1:], _BF16)
    a_ref[1] = jnp.ones(a_ref.shape[1:], _F32)

    def load_queries(qi):
        qt = qt_ref[qi]
        zero = jnp.zeros_like(qt)
        qz_ref[qi & 1, :, 0:Q_TILE] = jnp.where(first_part, qt, zero)
        qz_ref[qi & 1, :, Q_TILE:n_cols] = jnp.where(first_part, zero, qt)

    def scores(qi, b, slot):
        k0 = pl.multiple_of(b * K_TILE, K_TILE)
        s_ref[slot] = jnp.dot(k_ref[pl.ds(k0, K_TILE), :], qz_ref[qi & 1], preferred_element_type=_F32)

    def softmax(slot, bias):
        s = s_ref[slot]
        if bias is not None:
            s = s + bias
        m_prev = m_ref[...]
        m_new = jnp.maximum(m_prev, jnp.max(s, axis=0, keepdims=True))
        a = jnp.exp2(m_prev - m_new)
        p = jnp.exp2(s - m_new)
        l_ref[...] = a * l_ref[...] + jnp.sum(p, axis=0, keepdims=True)
        m_ref[...] = m_new
        a_ref[slot] = a
        p_ref[slot] = p.astype(_BF16)

    def values(b, slot):
        vb = vt_ref[jnp.maximum(b, 0)]
        acc_ref[...] = a_ref[slot] * acc_ref[...] + jnp.dot(vb, p_ref[slot], preferred_element_type=_F32)

    def step(qi, t, slot):
        values(t - 1, 1 - slot)
        softmax(slot, None)
        scores(qi, t + 1, 1 - slot)

    def q_block(qi, carry):
        m_ref[...] = jnp.full(m_ref.shape, NEG_INF, _F32)
        n_full = lax.div(qi, K_TILE // Q_TILE)

        done = 0
        unroll = STEP_UNROLL
        while unroll > 1:
            def unrolled(u, c, unroll=unroll, done=done):
                for i in range(unroll):
                    step(qi, done + unroll * u + i, i & 1)
                return c

            trips = lax.div(n_full - done, unroll)
            lax.fori_loop(0, trips, unrolled, 0)
            done = done + trips * unroll
            unroll //= 2

        def single(t, c):
            step(qi, t, t & 1)
            return c

        lax.fori_loop(done, n_full, single, 0)

        last = n_full & 1
        values(n_full - 1, 1 - last)
        softmax(last, bias_ref[qi - n_full * (K_TILE // Q_TILE)])
        nxt = jnp.minimum(qi + 1, n_q_blocks - 1)
        load_queries(nxt)
        scores(nxt, 0, 0)
        values(n_full, last)

        o2 = acc_ref[...] / l_ref[...]
        o = o2[:, 0:Q_TILE] - lam * o2[:, Q_TILE:n_cols]
        norm = lax.rsqrt(jnp.mean(o * o, axis=0, keepdims=True) + LN_EPS)
        o_ref[qi] = (o * norm * sg_ref[...] * (1.0 - lam_init)).astype(o_ref.dtype)
        return carry

    load_queries(0)
    scores(0, 0, 0)
    lax.fori_loop(0, n_q_blocks, q_block, 0)


def _flash_prompt(q, k, v, lam, sg, *, layer, lam_init):
    t = q.shape[0]
    assert t % K_TILE == 0 and K_TILE % Q_TILE == 0 and Q_TILE % CHUNK == 0 and STEP_UNROLL % 2 == 0
    nq, nk = t // Q_TILE, t // K_TILE
    qt = jnp.transpose(q.reshape(nq, Q_TILE, N_HEADS, HEAD_W), (2, 0, 3, 1))
    vt = jnp.transpose(v.reshape(nk, K_TILE, N_HEADS, V_DIM), (2, 0, 3, 1))
    sub = K_TILE // Q_TILE
    key_chunk = (jnp.arange(K_TILE) // CHUNK)[None, :, None]
    query = jnp.arange(sub)[:, None, None] * Q_TILE + (jnp.arange(2 * Q_TILE) % Q_TILE)[None, None, :]
    tail_bias = jnp.where(key_chunk <= query // CHUNK, 0.0, NEG_INF).astype(_F32)
    kern = functools.partial(_flash_kernel, lam_init=lam_init, n_q_blocks=nq)
    out = pl.pallas_call(
        kern,
        grid=(N_HEADS,),
        in_specs=[pl.BlockSpec((None, nq, HEAD_W, Q_TILE), lambda h: (h, 0, 0, 0)),
                  pl.BlockSpec((t, HEAD_W), lambda h: (0, h)),
                  pl.BlockSpec((None, nk, V_DIM, K_TILE), lambda h: (h, 0, 0, 0)),
                  pl.BlockSpec((sub, K_TILE, 2 * Q_TILE), lambda h: (0, 0, 0)),
                  pl.BlockSpec((None, 4, QK_DIM), lambda h: (layer, 0, 0)),
                  pl.BlockSpec((None, V_DIM, 1), lambda h: (layer, 0, 0))],
        out_specs=pl.BlockSpec((None, nq, V_DIM, Q_TILE), lambda h: (h, 0, 0, 0)),
        out_shape=jax.ShapeDtypeStruct((N_HEADS, nq, V_DIM, Q_TILE), _BF16),
        scratch_shapes=[pltpu.VMEM((2, HEAD_W, 2 * Q_TILE), _BF16),
                        pltpu.VMEM((2, K_TILE, 2 * Q_TILE), _F32),
                        pltpu.VMEM((2, K_TILE, 2 * Q_TILE), _BF16),
                        pltpu.VMEM((2, 1, 2 * Q_TILE), _F32),
                        pltpu.VMEM((1, 2 * Q_TILE), _F32),
                        pltpu.VMEM((1, 2 * Q_TILE), _F32),
                        pltpu.VMEM((V_DIM, 2 * Q_TILE), _F32)],
        compiler_params=_params(1),
        name="flash_prompt",
    )(qt, k, vt, tail_bias, lam, sg.reshape(sg.shape[0], V_DIM, 1))
    return jnp.transpose(out, (1, 3, 0, 2)).reshape(t, N_HEADS * V_DIM)


def _sample_attn_kernel(q_ref, kn_ref, vn_ref, kc_ref, vc_ref, lam_ref, sg_ref, o_ref, *, lam_init, steps):
    heads = SAMPLE_HEADS_PER_STEP
    per_head = 2 * steps
    past = kc_ref.shape[1]
    lam = _diff_lambda(lam_ref, lam_init)
    head_cols = [slice(h * HEAD_W, (h + 1) * HEAD_W) for h in range(heads)]
    head_rows = [slice(h * per_head, (h + 1) * per_head) for h in range(heads)]

    qz = jnp.concatenate([_split_parts(q_ref[:, c]) for c in head_cols], axis=0)
    kc = kc_ref[0].reshape(past * heads, HEAD_W).astype(_BF16)
    vc = vc_ref[0].reshape(past * heads, V_DIM).astype(_BF16)
    sc = _score(qz, kc)
    row_head = lax.shift_right_logical(lax.broadcasted_iota(jnp.int32, (heads * per_head, 1), 0),
                                       int(math.log2(per_head)))
    col_head = lax.broadcasted_iota(jnp.int32, (1, past * heads), 1) & (heads - 1)
    sc = jnp.where(row_head == col_head, sc, NEG_INF)
    sn = jnp.concatenate([_score(qz[r, :], kn_ref[:, c]) for r, c in zip(head_rows, head_cols)], axis=0)
    m = jnp.maximum(jnp.max(sc, axis=1, keepdims=True), jnp.max(sn, axis=1, keepdims=True))
    pc = jnp.exp2(sc - m)
    pn = jnp.exp2(sn - m)
    l = jnp.sum(pc, axis=1, keepdims=True) + jnp.sum(pn, axis=1, keepdims=True)
    pn = pn.astype(_BF16)
    on = jnp.concatenate([jnp.dot(pn[r, :], vn_ref[:, c], preferred_element_type=_F32)
                          for r, c in zip(head_rows, head_cols)], axis=0)
    o2 = (jnp.dot(pc.astype(_BF16), vc, preferred_element_type=_F32) + on) / l
    for h in range(heads):
        first = h * per_head
        o = o2[first:first + steps, :] - lam * o2[first + steps:first + per_head, :]
        o_ref[:, head_cols[h]] = _sub_norm(o, sg_ref, lam_init).astype(o_ref.dtype)


def _sample_attn(q, k_new, v_new, cache_k, cache_v, lam, sg, *, layer, lam_init, seqs):
    m, width = q.shape
    steps = m // seqs
    past = cache_k.shape[1]
    group_w = SAMPLE_HEADS_PER_STEP * HEAD_W
    groups = width // group_w
    kern = functools.partial(_sample_attn_kernel, lam_init=lam_init, steps=steps)
    new = pl.BlockSpec((steps, group_w), lambda b, g: (0, b * groups + g))
    cache = pl.BlockSpec((1, past, SAMPLE_HEADS_PER_STEP, HEAD_W), lambda b, g: (b, 0, g, 0))
    out = pl.pallas_call(
        kern,
        grid=(seqs, groups),
        in_specs=[new, new, new, cache, cache,
                  pl.BlockSpec((None, 4, QK_DIM), lambda b, g: (layer, 0, 0)),
                  pl.BlockSpec((None, 1, V_DIM), lambda b, g: (layer, 0, 0))],
        out_specs=new,
        out_shape=jax.ShapeDtypeStruct((steps, seqs * width), _BF16),
        compiler_params=_params(2),
        name="sample_attn",
    )(q.reshape(steps, seqs * width), k_new.reshape(steps, seqs * width),
      v_new.reshape(steps, seqs * width), cache_k, cache_v, lam, sg)
    return out.reshape(m, width)


def _oproj_ln_kernel(x_ref, a_ref, w_ref, g_ref, b_ref, o_ref, ob_ref):
    mix = jnp.dot(a_ref[...], w_ref[...], preferred_element_type=_F32)
    out = _layer_norm(ALPHA * x_ref[...] + mix, g_ref[...], b_ref[...])
    o_ref[...] = out
    ob_ref[...] = out.astype(_BF16)


def _oproj_ln(x, attn, w_o, g, b, *, ln_layer):
    m = x.shape[0]
    rows = min(ROW_TILE, m)
    assert m % rows == 0
    const = lambda i: (ln_layer, 0, 0)
    return pl.pallas_call(
        _oproj_ln_kernel,
        grid=(m // rows,),
        in_specs=[pl.BlockSpec((rows, D_MODEL), lambda i: (i, 0)),
                  pl.BlockSpec((rows, N_HEADS * V_DIM), lambda i: (i, 0)),
                  pl.BlockSpec((N_HEADS * V_DIM, D_MODEL), lambda i: (0, 0)),
                  pl.BlockSpec((None, 1, D_MODEL), const),
                  pl.BlockSpec((None, 1, D_MODEL), const)],
        out_specs=[pl.BlockSpec((rows, D_MODEL), lambda i: (i, 0))] * 2,
        out_shape=[jax.ShapeDtypeStruct((m, D_MODEL), _F32), jax.ShapeDtypeStruct((m, D_MODEL), _BF16)],
        compiler_params=_params(1),
        name="oproj_ln",
    )(x, attn, w_o, g, b)


def _run_group(x, pool_hist, conv_hist, caches, pos0, seqs, p):
    layer_inputs, new_conv = [], []
    k_f32 = v_f32 = k_b = v_b = xb = None
    for l in range(DEPTH):
        if l < N_A_LAYERS:
            layer_inputs.append(x)
            x, xb = _pool_ln(x, pool_hist[l], p['w_pool'], p['pool_scale'], p['ln1_g'], p['ln1_b'],
                             layer=l, seqs=seqs, pos0=pos0)
        else:
            j = l - N_A_LAYERS
            lam_init = 0.8 - 0.6 * math.exp(-0.3 * l)
            (q,) = _proj(xb, p['w_q%d' % j], (_BF16,), n=N_HEADS * HEAD_W, scale=Q_SCALE)
            if caches is None:
                attn = _flash_prompt(q, k_b, v_b, p['lam'], p['subln_g'], layer=j, lam_init=lam_init)
            else:
                attn = _sample_attn(q, k_b, v_b, caches[0], caches[1], p['lam'], p['subln_g'],
                                    layer=j, lam_init=lam_init, seqs=seqs)
            x, xb = _oproj_ln(x, attn, p['w_o%d' % j], p['ln1_g'], p['ln1_b'], ln_layer=l)
        bf16_copy = N_A_LAYERS - 1 <= l < DEPTH - 1
        cast_next = l + 1 < DEPTH and len(p['w_up']) == l + 1
        side = [(name, spec) for name, spec in p['to_cast'].get(l, []) if name not in p]
        outs = _ffn_ln(xb, x, conv_hist[l], p['w_up'][l], p['conv_w'], p['conv_b'], p['w_down'][l],
                       p['ln2_g'], p['ln2_b'], layer=l, seqs=seqs, bf16_copy=bf16_copy,
                       next_weights=(p['w_up_f32'], p['w_down_f32']) if cast_next else None,
                       side_casts=[spec for _, spec in side])
        for (name, (blocks, _, _)), cast in zip(side, outs[len(outs) - len(side):]):
            p[name] = cast.reshape(-1, blocks.shape[-1])
        outs = outs[:len(outs) - len(side)]
        if cast_next:
            *outs, w_up_next, w_down_next = outs
            p['w_up'].append(w_up_next)
            p['w_down'].append(w_down_next)
        x, *xb, new_a, new_v = outs
        xb = xb[0] if xb else None
        new_conv.append(jnp.concatenate([_untile_cols(new_a), _untile_cols(new_v)], axis=1))
        if l == N_A_LAYERS - 1:
            k_f32, k_b, v_f32, v_b = _kv_proj(xb, p['w_kv'], k_width=N_HEADS * HEAD_W)
    return x, layer_inputs, new_conv, k_f32, v_f32


def _untile_cols(a):
    tiles, rows, w = a.shape
    return jnp.transpose(a, (1, 0, 2)).reshape(rows, tiles * w)


def _to_time_major(a):
    seqs, steps, w = a.shape
    return jnp.transpose(a, (1, 0, 2)).reshape(steps * seqs, w)


def _to_batch_major(a, seqs):
    return jnp.swapaxes(a.reshape(a.shape[0] // seqs, seqs, *a.shape[1:]), 0, 1)


def kernel(x_prompt, x_sample, state_pool, state_ffn_conv, cache_k, cache_v, ln1_g, ln1_b, ln2_g, ln2_b,
           w_pool, pool_scale, w_up, conv_w, conv_b, w_down, w_kv, w_q, lam, subln_g, w_o):
    rows = lambda a: a[:, None, :]
    p = dict(ln1_g=rows(ln1_g), ln1_b=rows(ln1_b), ln2_g=rows(ln2_g), ln2_b=rows(ln2_b),
             pool_scale=rows(pool_scale), conv_w=conv_w, conv_b=rows(conv_b), lam=lam, subln_g=rows(subln_g),
             w_pool=w_pool.astype(_BF16), w_up_f32=w_up, w_down_f32=w_down,
             w_up=[w_up[0].astype(_BF16)], w_down=[w_down[0].astype(_BF16)])

    ffn_steps = (x_prompt.shape[1] // ROW_TILE) * N_FF_TILES
    block_elems = CAST_BLOCK_ELEMS
    while w_kv.size // block_elems > ffn_steps:
        block_elems *= 2

    def blocks(w):
        return w.reshape(-1, block_elems // w.shape[-1], w.shape[-1])

    per_layer = D_MODEL * N_HEADS * HEAD_W // block_elems
    kv_blocks, q_blocks, o_blocks = blocks(w_kv), blocks(w_q), blocks(w_o)
    p['to_cast'] = {N_A_LAYERS - 2: [('w_kv', (kv_blocks, 0, kv_blocks.shape[0]))]}
    for j in range(DEPTH - N_A_LAYERS):
        p['to_cast'][N_A_LAYERS - 1 + j] = [('w_q%d' % j, (q_blocks, j * per_layer, per_layer)),
                                            ('w_o%d' % j, (o_blocks, j * per_layer, per_layer))]

    def finish(x, layer_inputs, new_conv, k_new, v_new, seqs):
        steps = x.shape[0] // seqs
        y = _to_batch_major(x, seqs)
        pool = jnp.stack([_to_batch_major(u[(steps - POOL_HIST) * seqs:], seqs) for u in layer_inputs])
        conv = jnp.stack([_to_batch_major(c, seqs) for c in new_conv])
        return y, pool, conv, _to_batch_major(k_new, seqs), _to_batch_major(v_new, seqs)

    b, t, _ = x_prompt.shape
    assert b == 1 and t >= POOL_HIST
    zeros_pool = jnp.zeros(((POOL_HIST + 1) * b, D_MODEL), _F32)
    zeros_conv = jnp.zeros(((CONV_W - 1) * b, 2 * D_FF), _F32)
    out_p = _run_group(_to_time_major(x_prompt), [zeros_pool] * N_A_LAYERS, [zeros_conv] * DEPTH,
                       None, 0, b, p)
    y_p, pool_p, conv_p, k_p, v_p = finish(*out_p, b)

    sb, steps, _ = x_sample.shape
    past = cache_k.shape[1]
    assert steps >= POOL_HIST
    pad = jnp.zeros((sb, D_MODEL), _F32)
    pool_hist = [jnp.concatenate([pad, _to_time_major(state_pool[l])], axis=0) for l in range(N_A_LAYERS)]
    conv_hist = [_to_time_major(state_ffn_conv[l]) for l in range(DEPTH)]
    out_s = _run_group(_to_time_major(x_sample), pool_hist, conv_hist, (cache_k, cache_v), past, sb, p)
    y_s, pool_s, conv_s, k_s, v_s = finish(*out_s, sb)

    return (y_p, y_s, pool_p, pool_s, conv_p, conv_s, k_p, v_p, k_s, v_s)
```

```python
import functools
import math

import jax
import jax.numpy as jnp
from jax import lax
from jax.experimental import pallas as pl
from jax.experimental.pallas import tpu as pltpu

D_MODEL = 2048
DEPTH = 4
N_A_LAYERS = DEPTH // 2
CHUNK = 64
POOL_WINDOWS = (2, 4, 8, 16)
POOL_GROUP = D_MODEL // len(POOL_WINDOWS)
POOL_HIST = max(POOL_WINDOWS) - 1
N_HEADS = 16
QK_DIM = 64
V_DIM = 2 * QK_DIM
HEAD_W = 2 * QK_DIM
D_FF = 5632
CONV_W = 3
LN_EPS = 1e-5
ALPHA = (2 * DEPTH) ** 0.25
ATTN_SCALE = QK_DIM ** -0.5
Q_SCALE = ATTN_SCALE * math.log2(math.e)
NEG_INF = -1e30

V7X_VMEM_BYTES = 64 * 1024 * 1024
VMEM_LIMIT_BYTES = V7X_VMEM_BYTES - 6 * 1024 * 1024
SUBLANES = 8

ROW_TILE = 512
FF_TILE = 512
CAST_BLOCK_ELEMS = 64 * 1024
N_FF_TILES = D_FF // FF_TILE
MM_COL_TILE = 1024
Q_TILE = 256
K_TILE = 512
STEP_UNROLL = 4
SAMPLE_HEADS_PER_STEP = 8

_F32 = jnp.float32
_BF16 = jnp.bfloat16


def _params(n_axes):
    return pltpu.CompilerParams(dimension_semantics=("arbitrary",) * n_axes,
                                vmem_limit_bytes=VMEM_LIMIT_BYTES)


def _layer_norm(z, g, b):
    mu = jnp.mean(z, axis=-1, keepdims=True)
    zc = z - mu
    var = jnp.mean(zc * zc, axis=-1, keepdims=True)
    return zc * lax.rsqrt(var + LN_EPS) * g + b


def _gelu_tanh(x):
    cdf = 0.5 * (1.0 + jnp.tanh(math.sqrt(2.0 / math.pi) * (x + 0.044715 * (x * x * x))))
    return x * cdf


def _pool_window_starts(seqs):
    starts, lo, shift = [], 0, seqs
    for _ in POOL_WINDOWS:
        lo = -(-(lo + shift) // SUBLANES) * SUBLANES
        starts.append(lo)
        shift *= 2
    return starts


def _pool_ln_kernel(x_ref, halo_ref, hist_ref, w_ref, scale_ref, g_ref, b_ref, o_ref, ob_ref,
                    ext_ref, sums_ref, z_ref, *, seqs, rows, pos0):
    i = pl.program_id(0)
    halo_rows = ext_ref.shape[0] - rows
    hist_rows = hist_ref.shape[0]

    @pl.when(i == 0)
    def _():
        if halo_rows > hist_rows:
            ext_ref[0:halo_rows - hist_rows, :] = jnp.zeros((halo_rows - hist_rows, D_MODEL), _F32)
        ext_ref[halo_rows - hist_rows:halo_rows, :] = hist_ref[...]

    @pl.when(i > 0)
    def _():
        ext_ref[0:halo_rows, :] = halo_ref[...]

    ext_ref[halo_rows:halo_rows + rows, :] = x_ref[...]

    row = lax.broadcasted_iota(jnp.int32, (rows, 1), 0) + i * rows
    step = lax.shift_right_logical(row, int(math.log2(seqs)))
    pos_plus_1 = (step + (pos0 + 1)).astype(_F32)

    starts = _pool_window_starts(seqs)
    assert POOL_WINDOWS == tuple(2 ** (k + 1) for k in range(len(POOL_WINDOWS))) and starts[-1] <= halo_rows
    window_sums, src, shift = [], ext_ref, seqs
    for k in range(len(POOL_WINDOWS)):
        last = k == len(POOL_WINDOWS) - 1
        first = halo_rows if last else starts[k]
        n = halo_rows + rows - first
        cols = slice(k * POOL_GROUP, D_MODEL)
        total = src[first:first + n, cols] + src[first - shift:first - shift + n, cols]
        if last:
            window_sums.append(total)
        else:
            sums_ref[k, first:first + n, cols] = total
            window_sums.append(None)
            src, shift = sums_ref.at[k], 2 * shift

    for g, w in enumerate(POOL_WINDOWS):
        cols = slice(g * POOL_GROUP, (g + 1) * POOL_GROUP)
        cur = x_ref[:, cols]
        win = window_sums[g] if window_sums[g] is not None else sums_ref[g, halo_rows:halo_rows + rows, cols]
        cnt = jnp.minimum(float(w), pos_plus_1)
        d = win / cnt - cur
        mix = jnp.dot(d.astype(_BF16), w_ref[g], preferred_element_type=_F32)
        z_ref[:, cols] = ALPHA * cur + mix * scale_ref[:, cols]

    out = _layer_norm(z_ref[...], g_ref[...], b_ref[...])
    o_ref[...] = out
    ob_ref[...] = out.astype(_BF16)


def _pool_ln(x, hist, w_pool, scale, g, b, *, layer, seqs, pos0):
    m = x.shape[0]
    rows = min(ROW_TILE, m)
    hist_rows = (POOL_HIST + 1) * seqs
    halo_rows = max(hist_rows, _pool_window_starts(seqs)[-1])
    assert m % rows == 0 and rows % halo_rows == 0 and seqs & (seqs - 1) == 0 and hist.shape[0] == hist_rows
    halo_blocks_per_tile = rows // halo_rows
    kern = functools.partial(_pool_ln_kernel, seqs=seqs, rows=rows, pos0=pos0)
    return pl.pallas_call(
        kern,
        grid=(m // rows,),
        in_specs=[
            pl.BlockSpec((rows, D_MODEL), lambda i: (i, 0)),
            pl.BlockSpec((halo_rows, D_MODEL),
                         lambda i: (jnp.maximum(i * halo_blocks_per_tile - 1, 0), 0)),
            pl.BlockSpec((hist_rows, D_MODEL), lambda i: (0, 0)),
            pl.BlockSpec((None, len(POOL_WINDOWS), POOL_GROUP, POOL_GROUP), lambda i: (layer, 0, 0, 0)),
            pl.BlockSpec((None, 1, D_MODEL), lambda i: (layer, 0, 0)),
            pl.BlockSpec((None, 1, D_MODEL), lambda i: (layer, 0, 0)),
            pl.BlockSpec((None, 1, D_MODEL), lambda i: (layer, 0, 0)),
        ],
        out_specs=[pl.BlockSpec((rows, D_MODEL), lambda i: (i, 0))] * 2,
        out_shape=[jax.ShapeDtypeStruct((m, D_MODEL), _F32), jax.ShapeDtypeStruct((m, D_MODEL), _BF16)],
        scratch_shapes=[pltpu.VMEM((halo_rows + rows, D_MODEL), _F32),
                        pltpu.VMEM((len(POOL_WINDOWS) - 1, halo_rows + rows, D_MODEL), _F32),
                        pltpu.VMEM((rows, D_MODEL), _F32)],
        compiler_params=_params(1),
        name="pool_ln",
    )(x, x, hist, w_pool, scale, g, b)


def _ffn_ln_kernel(xb_ref, x_ref, hist_a_ref, hist_v_ref, wa_ref, wv_ref, cwa_ref, cwv_ref, cba_ref, cbv_ref,
                   wd_ref, g_ref, b_ref, *refs, seqs, rows, n_items, bf16_copy, cast_next, side_blocks):
    refs = list(refs)
    s = pl.program_id(0)
    next_up_ref, next_down_ref = (refs.pop(0), refs.pop(0)) if cast_next else (None, None)
    side_in = [refs.pop(0) for _ in side_blocks]
    o_ref = refs.pop(0)
    ob_ref = refs.pop(0) if bf16_copy else None
    new_a_ref, new_v_ref = refs.pop(0), refs.pop(0)
    next_up_b_ref, next_down_b_ref = (refs.pop(0), refs.pop(0)) if cast_next else (None, None)
    side_out = [refs.pop(0) for _ in side_blocks]
    ext_a_ref, ext_v_ref = refs
    if cast_next:
        next_up_b_ref[...] = next_up_ref[...].astype(_BF16)
        next_down_b_ref[...] = next_down_ref[...].astype(_BF16)
    for src_ref, dst_ref, n_blocks in zip(side_in, side_out, side_blocks):
        @pl.when(s < n_blocks)
        def _(src_ref=src_ref, dst_ref=dst_ref):
            dst_ref[...] = src_ref[...].astype(_BF16)

    halo = (CONV_W - 1) * seqs
    a_item = jnp.minimum(s, n_items - 1)
    i_a = lax.div(a_item, N_FF_TILES)
    j_a = a_item - i_a * N_FF_TILES
    j_b = lax.rem(jnp.maximum(s - 1, 0), N_FF_TILES)
    parity = s & 1

    @pl.when(s == 0)
    def _():
        ext_a_ref[1] = jnp.zeros(ext_a_ref.shape[1:], _F32)
        ext_v_ref[1] = jnp.zeros(ext_v_ref.shape[1:], _F32)

    n_copies = ext_a_ref.shape[1]
    delays = range(CONV_W)
    if n_copies == 1:
        copy_of, write_at, read_at = [0] * CONV_W, [halo] * CONV_W, [halo - d * seqs for d in delays]
    else:
        copy_of, write_at, read_at = list(delays), [d * seqs for d in delays], [0] * CONV_W

    def fill_history(src_a, src_v):
        for c in range(n_copies):
            n = write_at[copy_of.index(c)]
            if n:
                ext_a_ref[parity, c, 0:n, :] = src_a[halo - n:halo, :]
                ext_v_ref[parity, c, 0:n, :] = src_v[halo - n:halo, :]

    @pl.when(i_a == 0)
    def _():
        fill_history(hist_a_ref[...], hist_v_ref[...])

    @pl.when(i_a > 0)
    def _():
        fill_history(new_a_ref[j_a], new_v_ref[j_a])

    @pl.when(j_b == 0)
    def _():
        o_ref[...] = jnp.zeros_like(o_ref)

    def both_halves(slot):
        xb = xb_ref[...]
        ha = jnp.dot(xb, wa_ref[...], preferred_element_type=_F32)
        hv = jnp.dot(xb, wv_ref[...], preferred_element_type=_F32)
        for c in range(n_copies):
            at = write_at[copy_of.index(c)]
            ext_a_ref[slot, c, at:at + rows, :] = ha
            ext_v_ref[slot, c, at:at + rows, :] = hv
        new_a_ref[j_a] = ha[rows - halo:, :]
        new_v_ref[j_a] = hv[rows - halo:, :]

        def conv(ext_ref, cw_ref, cb_ref):
            cw = cw_ref[...]
            c = cb_ref[...]
            for tap in range(CONV_W):
                d = CONV_W - 1 - tap
                c = c + cw[tap:tap + 1, :] * ext_ref[1 - slot, copy_of[d], read_at[d]:read_at[d] + rows, :]
            return c

        ca = conv(ext_a_ref, cwa_ref, cba_ref)
        cv = conv(ext_v_ref, cwv_ref, cbv_ref)
        act = (_gelu_tanh(ca) * cv).astype(_BF16)
        o_ref[...] += jnp.dot(act, wd_ref[...], preferred_element_type=_F32)

    for slot in (0, 1):
        pl.when(parity == slot)(functools.partial(both_halves, slot))

    @pl.when(jnp.logical_and(s > 0, j_b == N_FF_TILES - 1))
    def _():
        out = _layer_norm(ALPHA * x_ref[...] + o_ref[...], g_ref[...], b_ref[...])
        o_ref[...] = out
        if ob_ref is not None:
            ob_ref[...] = out.astype(_BF16)


def _ffn_ln(xb, x, hist, w_up, conv_w, conv_b, w_down, g, b, *, layer, seqs, bf16_copy, next_weights=None,
            side_casts=()):
    m = x.shape[0]
    rows = min(ROW_TILE, m)
    halo = (CONV_W - 1) * seqs
    ext_rows = -(-(rows + halo) // SUBLANES) * SUBLANES
    n_copies = 1 if seqs % SUBLANES == 0 else CONV_W
    assert m % rows == 0 and rows >= halo
    n_items = (m // rows) * N_FF_TILES
    kern = functools.partial(_ffn_ln_kernel, seqs=seqs, rows=rows, n_items=n_items, bf16_copy=bf16_copy,
                             cast_next=next_weights is not None, side_blocks=tuple(n for _, _, n in side_casts))

    def up_item(s):
        item = jnp.minimum(s, n_items - 1)
        return lax.div(item, N_FF_TILES), lax.rem(item, N_FF_TILES)

    def down_item(s):
        item = jnp.maximum(s - 1, 0)
        return lax.div(item, N_FF_TILES), lax.rem(item, N_FF_TILES)

    up_a = lambda s: (0, up_item(s)[1])
    up_v = lambda s: (0, up_item(s)[1] + N_FF_TILES)
    conv_w_a = lambda s: (layer, 0, down_item(s)[1])
    conv_w_v = lambda s: (layer, 0, down_item(s)[1] + N_FF_TILES)
    ln_row = lambda s: (layer, 0, 0)
    down_rows = pl.BlockSpec((rows, D_MODEL), lambda s: (down_item(s)[0], 0))
    new_hist = pl.BlockSpec((N_FF_TILES, halo, FF_TILE), lambda s: (0, 0, 0))
    row_out = [jax.ShapeDtypeStruct((m, D_MODEL), _F32)]
    if bf16_copy:
        row_out.append(jax.ShapeDtypeStruct((m, D_MODEL), _BF16))
    cast_in, cast_specs, cast_out = [], [], []
    if next_weights is not None:
        n_row_tiles = m // rows
        up_blk = (D_MODEL // n_row_tiles, 2 * D_FF // N_FF_TILES)
        down_blk = (D_FF // n_items, D_MODEL)
        assert up_blk[0] * n_row_tiles == D_MODEL and down_blk[0] * n_items == D_FF
        assert up_blk[0] % (2 * SUBLANES) == 0 and down_blk[0] % (2 * SUBLANES) == 0
        cast_in = list(next_weights)
        cast_specs = [pl.BlockSpec((None,) + up_blk, lambda s: (layer + 1,) + up_item(s)),
                      pl.BlockSpec((None,) + down_blk, lambda s: (layer + 1, jnp.minimum(s, n_items - 1), 0))]
        cast_out = [(pl.BlockSpec(up_blk, up_item), jax.ShapeDtypeStruct((D_MODEL, 2 * D_FF), _BF16)),
                    (pl.BlockSpec(down_blk, lambda s: (jnp.minimum(s, n_items - 1), 0)),
                     jax.ShapeDtypeStruct((D_FF, D_MODEL), _BF16))]
    for blocks, first, count in side_casts:
        assert count <= n_items
        blk = (None,) + blocks.shape[1:]
        cast_in.append(blocks)
        cast_specs.append(pl.BlockSpec(blk, lambda s, first=first, count=count: (first + jnp.minimum(s, count - 1), 0, 0)))
        cast_out.append((pl.BlockSpec(blk, lambda s, count=count: (jnp.minimum(s, count - 1), 0, 0)),
                         jax.ShapeDtypeStruct((count,) + blocks.shape[1:], _BF16)))
    return pl.pallas_call(
        kern,
        grid=(n_items + 1,),
        in_specs=[
            pl.BlockSpec((rows, D_MODEL), lambda s: (up_item(s)[0], 0)),
            down_rows,
            pl.BlockSpec((halo, FF_TILE), up_a),
            pl.BlockSpec((halo, FF_TILE), up_v),
            pl.BlockSpec((D_MODEL, FF_TILE), up_a),
            pl.BlockSpec((D_MODEL, FF_TILE), up_v),
            pl.BlockSpec((None, CONV_W, FF_TILE), conv_w_a),
            pl.BlockSpec((None, CONV_W, FF_TILE), conv_w_v),
            pl.BlockSpec((None, 1, FF_TILE), conv_w_a),
            pl.BlockSpec((None, 1, FF_TILE), conv_w_v),
            pl.BlockSpec((FF_TILE, D_MODEL), lambda s: (down_item(s)[1], 0)),
            pl.BlockSpec((None, 1, D_MODEL), ln_row),
            pl.BlockSpec((None, 1, D_MODEL), ln_row),
        ] + cast_specs,
        out_specs=[down_rows] * len(row_out) + [new_hist, new_hist] + [spec for spec, _ in cast_out],
        out_shape=(row_out + [jax.ShapeDtypeStruct((N_FF_TILES, halo, FF_TILE), _F32)] * 2
                   + [shape for _, shape in cast_out]),
        scratch_shapes=[
            pltpu.VMEM((2, n_copies, ext_rows, FF_TILE), _F32),
            pltpu.VMEM((2, n_copies, ext_rows, FF_TILE), _F32),
        ],
        compiler_params=_params(1),
        name="ffn_ln",
    )(xb, x, hist, hist, w_up, w_up, conv_w, conv_w, conv_b, conv_b, w_down, g, b, *cast_in)


def _q_proj_kernel(x_ref, w_ref, o_ref, *, scale):
    r = jnp.dot(x_ref[...], w_ref[...], preferred_element_type=_F32)
    o_ref[...] = (r * scale).astype(o_ref.dtype)


def _q_proj(x, w, *, scale):
    m, k = x.shape
    n = w.shape[1]
    rows, cols = min(2 * ROW_TILE, m), MM_COL_TILE
    assert m % rows == 0 and n % cols == 0
    return pl.pallas_call(
        functools.partial(_q_proj_kernel, scale=scale),
        grid=(m // rows, n // cols),
        in_specs=[pl.BlockSpec((rows, k), lambda i, j: (i, 0)), pl.BlockSpec((k, cols), lambda i, j: (0, j))],
        out_specs=pl.BlockSpec((rows, cols), lambda i, j: (i, j)),
        out_shape=jax.ShapeDtypeStruct((m, n), _BF16),
        compiler_params=_params(2),
        name="q_proj",
    )(x, w)


def _kv_proj_kernel(x_ref, w_ref, kf_ref, kb_ref, vf_ref, vb_ref, *, k_tiles):
    j = pl.program_id(1)
    r = jnp.dot(x_ref[...], w_ref[...], preferred_element_type=_F32)

    @pl.when(j < k_tiles)
    def _():
        kf_ref[...] = r.reshape(kf_ref.shape)
        kb_ref[...] = r.astype(_BF16)

    @pl.when(j >= k_tiles)
    def _():
        vf_ref[...] = r.reshape(vf_ref.shape)
        vb_ref[...] = r.astype(_BF16)


def _kv_proj(x, w, *, k_width):
    m, k = x.shape
    v_width = w.shape[1] - k_width
    rows, cols = min(2 * ROW_TILE, m), MM_COL_TILE
    assert m % rows == 0 and k_width % cols == 0 and v_width % cols == 0 and cols % HEAD_W == 0
    k_tiles = k_width // cols
    k_col = lambda j: jnp.minimum(j, k_tiles - 1)
    v_col = lambda j: jnp.maximum(j - k_tiles, 0)
    heads = (rows, cols // HEAD_W, HEAD_W)
    return pl.pallas_call(
        functools.partial(_kv_proj_kernel, k_tiles=k_tiles),
        grid=(m // rows, (k_width + v_width) // cols),
        in_specs=[pl.BlockSpec((rows, k), lambda i, j: (i, 0)),
                  pl.BlockSpec((k, cols), lambda i, j: (0, j))],
        out_specs=[pl.BlockSpec(heads, lambda i, j: (i, k_col(j), 0)),
                   pl.BlockSpec((rows, cols), lambda i, j: (i, k_col(j))),
                   pl.BlockSpec(heads, lambda i, j: (i, v_col(j), 0)),
                   pl.BlockSpec((rows, cols), lambda i, j: (i, v_col(j)))],
        out_shape=[jax.ShapeDtypeStruct((m, k_width // HEAD_W, HEAD_W), _F32),
                   jax.ShapeDtypeStruct((m, k_width), _BF16),
                   jax.ShapeDtypeStruct((m, v_width // HEAD_W, HEAD_W), _F32),
                   jax.ShapeDtypeStruct((m, v_width), _BF16)],
        compiler_params=_params(2),
        name="kv_proj",
    )(x, w)


def _diff_lambda(lam_ref, lam_init):
    lam = lam_ref[...]
    e1 = jnp.exp(jnp.sum(lam[0:1, :] * lam[1:2, :], axis=-1, keepdims=True))
    e2 = jnp.exp(jnp.sum(lam[2:3, :] * lam[3:4, :], axis=-1, keepdims=True))
    return e1 - e2 + lam_init


def _split_parts(q):
    first = lax.broadcasted_iota(jnp.int32, (1, HEAD_W), 1) < QK_DIM
    zero = jnp.zeros_like(q)
    return jnp.concatenate([jnp.where(first, q, zero), jnp.where(first, zero, q)], axis=0)


def _sub_norm(o, sg_ref, lam_init):
    return o * lax.rsqrt(jnp.mean(o * o, axis=-1, keepdims=True) + LN_EPS) * sg_ref[...] * (1.0 - lam_init)


def _score(qz, kb):
    return lax.dot_general(qz, kb, (((1,), (1,)), ((), ())), preferred_element_type=_F32)


def _flash_kernel(qt_ref, k_ref, vt_ref, bias_ref, lam_ref, sg_ref, o_ref,
                  qz_ref, s_ref, p_ref, a_ref, m_ref, l_ref, acc_ref, *, lam_init, n_q_blocks):
    lam = _diff_lambda(lam_ref, lam_init)
    n_cols = 2 * Q_TILE
    first_part = lax.broadcasted_iota(jnp.int32, (HEAD_W, 1), 0) < QK_DIM

    l_ref[...] = jnp.zeros_like(l_ref)
    acc_ref[...] = jnp.zeros_like(acc_ref)
    p_ref[1] = jnp.zeros(p_ref.shape[1:], _BF16)
    a_ref[1] = jnp.ones(a_ref.shape[1:], _F32)

    def load_queries(qi):
        qt = qt_ref[qi]
        zero = jnp.zeros_like(qt)
        qz_ref[qi & 1, :, 0:Q_TILE] = jnp.where(first_part, qt, zero)
        qz_ref[qi & 1, :, Q_TILE:n_cols] = jnp.where(first_part, zero, qt)

    def scores(qi, b, slot):
        k0 = pl.multiple_of(b * K_TILE, K_TILE)
        s_ref[slot] = jnp.dot(k_ref[pl.ds(k0, K_TILE), :], qz_ref[qi & 1], preferred_element_type=_F32)

    def softmax(slot, bias):
        s = s_ref[slot]
        if bias is not None:
            s = s + bias
        m_prev = m_ref[...]
        m_new = jnp.maximum(m_prev, jnp.max(s, axis=0, keepdims=True))
        a = jnp.exp2(m_prev - m_new)
        p = jnp.exp2(s - m_new)
        l_ref[...] = a * l_ref[...] + jnp.sum(p, axis=0, keepdims=True)
        m_ref[...] = m_new
        a_ref[slot] = a
        p_ref[slot] = p.astype(_BF16)

    def values(b, slot):
        vb = vt_ref[jnp.maximum(b, 0)]
        acc_ref[...] = a_ref[slot] * acc_ref[...] + jnp.dot(vb, p_ref[slot], preferred_element_type=_F32)

    def step(qi, t, slot):
        values(t - 1, 1 - slot)
        softmax(slot, None)
        scores(qi, t + 1, 1 - slot)

    def q_block(qi, carry):
        m_ref[...] = jnp.full(m_ref.shape, NEG_INF, _F32)
        n_full = lax.div(qi, K_TILE // Q_TILE)

        done = 0
        unroll = STEP_UNROLL
        while unroll > 1:
            def unrolled(u, c, unroll=unroll, done=done):
                for i in range(unroll):
                    step(qi, done + unroll * u + i, i & 1)
                return c

            trips = lax.div(n_full - done, unroll)
            lax.fori_loop(0, trips, unrolled, 0)
            done = done + trips * unroll
            unroll //= 2

        def single(t, c):
            step(qi, t, t & 1)
            return c

        lax.fori_loop(done, n_full, single, 0)

        last = n_full & 1
        values(n_full - 1, 1 - last)
        softmax(last, bias_ref[qi - n_full * (K_TILE // Q_TILE)])
        nxt = jnp.minimum(qi + 1, n_q_blocks - 1)
        load_queries(nxt)
        scores(nxt, 0, 0)
        values(n_full, last)

        o2 = acc_ref[...] / l_ref[...]
        o = o2[:, 0:Q_TILE] - lam * o2[:, Q_TILE:n_cols]
        norm = lax.rsqrt(jnp.mean(o * o, axis=0, keepdims=True) + LN_EPS)
        o_ref[qi] = (o * norm * sg_ref[...] * (1.0 - lam_init)).astype(o_ref.dtype)
        return carry

    load_queries(0)
    scores(0, 0, 0)
    lax.fori_loop(0, n_q_blocks, q_block, 0)


def _flash_prompt(q, k, v, lam, sg, *, layer, lam_init):
    t = q.shape[0]
    assert t % K_TILE == 0 and K_TILE % Q_TILE == 0 and Q_TILE % CHUNK == 0 and STEP_UNROLL % 2 == 0
    nq, nk = t // Q_TILE, t // K_TILE
    qt = jnp.transpose(q.reshape(nq, Q_TILE, N_HEADS, HEAD_W), (2, 0, 3, 1))
    vt = jnp.transpose(v.reshape(nk, K_TILE, N_HEADS, V_DIM), (2, 0, 3, 1))
    sub = K_TILE // Q_TILE
    key_chunk = (jnp.arange(K_TILE) // CHUNK)[None, :, None]
    query = jnp.arange(sub)[:, None, None] * Q_TILE + (jnp.arange(2 * Q_TILE) % Q_TILE)[None, None, :]
    tail_bias = jnp.where(key_chunk <= query // CHUNK, 0.0, NEG_INF).astype(_F32)
    kern = functools.partial(_flash_kernel, lam_init=lam_init, n_q_blocks=nq)
    out = pl.pallas_call(
        kern,
        grid=(N_HEADS,),
        in_specs=[pl.BlockSpec((None, nq, HEAD_W, Q_TILE), lambda h: (h, 0, 0, 0)),
                  pl.BlockSpec((t, HEAD_W), lambda h: (0, h)),
                  pl.BlockSpec((None, nk, V_DIM, K_TILE), lambda h: (h, 0, 0, 0)),
                  pl.BlockSpec((sub, K_TILE, 2 * Q_TILE), lambda h: (0, 0, 0)),
                  pl.BlockSpec((None, 4, QK_DIM), lambda h: (layer, 0, 0)),
                  pl.BlockSpec((None, V_DIM, 1), lambda h: (layer, 0, 0))],
        out_specs=pl.BlockSpec((None, nq, V_DIM, Q_TILE), lambda h: (h, 0, 0, 0)),
        out_shape=jax.ShapeDtypeStruct((N_HEADS, nq, V_DIM, Q_TILE), _BF16),
        scratch_shapes=[pltpu.VMEM((2, HEAD_W, 2 * Q_TILE), _BF16),
                        pltpu.VMEM((2, K_TILE, 2 * Q_TILE), _F32),
                        pltpu.VMEM((2, K_TILE, 2 * Q_TILE), _BF16),
                        pltpu.VMEM((2, 1, 2 * Q_TILE), _F32),
                        pltpu.VMEM((1, 2 * Q_TILE), _F32),
                        pltpu.VMEM((1, 2 * Q_TILE), _F32),
                        pltpu.VMEM((V_DIM, 2 * Q_TILE), _F32)],
        compiler_params=_params(1),
        name="flash_prompt",
    )(qt, k, vt, tail_bias, lam, sg.reshape(sg.shape[0], V_DIM, 1))
    return jnp.transpose(out, (1, 3, 0, 2)).reshape(t, N_HEADS * V_DIM)


def _sample_attn_kernel(q_ref, kn_ref, vn_ref, kc_ref, vc_ref, lam_ref, sg_ref, o_ref, *, lam_init, steps):
    heads = SAMPLE_HEADS_PER_STEP
    per_head = 2 * steps
    past = kc_ref.shape[1]
    lam = _diff_lambda(lam_ref, lam_init)
    head_cols = [slice(h * HEAD_W, (h + 1) * HEAD_W) for h in range(heads)]
    head_rows = [slice(h * per_head, (h + 1) * per_head) for h in range(heads)]

    qz = jnp.concatenate([_split_parts(q_ref[:, c]) for c in head_cols], axis=0)
    kc = kc_ref[0].reshape(past * heads, HEAD_W).astype(_BF16)
    vc = vc_ref[0].reshape(past * heads, V_DIM).astype(_BF16)
    sc = _score(qz, kc)
    row_head = lax.shift_right_logical(lax.broadcasted_iota(jnp.int32, (heads * per_head, 1), 0),
                                       int(math.log2(per_head)))
    col_head = lax.broadcasted_iota(jnp.int32, (1, past * heads), 1) & (heads - 1)
    sc = jnp.where(row_head == col_head, sc, NEG_INF)
    sn = jnp.concatenate([_score(qz[r, :], kn_ref[:, c]) for r, c in zip(head_rows, head_cols)], axis=0)
    m = jnp.maximum(jnp.max(sc, axis=1, keepdims=True), jnp.max(sn, axis=1, keepdims=True))
    pc = jnp.exp2(sc - m)
    pn = jnp.exp2(sn - m)
    l = jnp.sum(pc, axis=1, keepdims=True) + jnp.sum(pn, axis=1, keepdims=True)
    pn = pn.astype(_BF16)
    on = jnp.concatenate([jnp.dot(pn[r, :], vn_ref[:, c], preferred_element_type=_F32)
                          for r, c in zip(head_rows, head_cols)], axis=0)
    o2 = (jnp.dot(pc.astype(_BF16), vc, preferred_element_type=_F32) + on) / l
    for h in range(heads):
        first = h * per_head
        o = o2[first:first + steps, :] - lam * o2[first + steps:first + per_head, :]
        o_ref[:, head_cols[h]] = _sub_norm(o, sg_ref, lam_init).astype(o_ref.dtype)


def _sample_attn(q, k_new, v_new, cache_k, cache_v, lam, sg, *, layer, lam_init, seqs):
    m, width = q.shape
    steps = m // seqs
    past = cache_k.shape[1]
    group_w = SAMPLE_HEADS_PER_STEP * HEAD_W
    groups = width // group_w
    kern = functools.partial(_sample_attn_kernel, lam_init=lam_init, steps=steps)
    new = pl.BlockSpec((steps, group_w), lambda b, g: (0, b * groups + g))
    cache = pl.BlockSpec((1, past, SAMPLE_HEADS_PER_STEP, HEAD_W), lambda b, g: (b, 0, g, 0))
    out = pl.pallas_call(
        kern,
        grid=(seqs, groups),
        in_specs=[new, new, new, cache, cache,
                  pl.BlockSpec((None, 4, QK_DIM), lambda b, g: (layer, 0, 0)),
                  pl.BlockSpec((None, 1, V_DIM), lambda b, g: (layer, 0, 0))],
        out_specs=new,
        out_shape=jax.ShapeDtypeStruct((steps, seqs * width), _BF16),
        compiler_params=_params(2),
        name="sample_attn",
    )(q.reshape(steps, seqs * width), k_new.reshape(steps, seqs * width),
      v_new.reshape(steps, seqs * width), cache_k, cache_v, lam, sg)
    return out.reshape(m, width)


def _oproj_ln_kernel(x_ref, a_ref, w_ref, g_ref, b_ref, o_ref, ob_ref):
    mix = jnp.dot(a_ref[...], w_ref[...], preferred_element_type=_F32)
    out = _layer_norm(ALPHA * x_ref[...] + mix, g_ref[...], b_ref[...])
    o_ref[...] = out
    ob_ref[...] = out.astype(_BF16)


def _oproj_ln(x, attn, w_o, g, b, *, ln_layer):
    m = x.shape[0]
    rows = min(ROW_TILE, m)
    assert m % rows == 0
    const = lambda i: (ln_layer, 0, 0)
    return pl.pallas_call(
        _oproj_ln_kernel,
        grid=(m // rows,),
        in_specs=[pl.BlockSpec((rows, D_MODEL), lambda i: (i, 0)),
                  pl.BlockSpec((rows, N_HEADS * V_DIM), lambda i: (i, 0)),
                  pl.BlockSpec((N_HEADS * V_DIM, D_MODEL), lambda i: (0, 0)),
                  pl.BlockSpec((None, 1, D_MODEL), const),
                  pl.BlockSpec((None, 1, D_MODEL), const)],
        out_specs=[pl.BlockSpec((rows, D_MODEL), lambda i: (i, 0))] * 2,
        out_shape=[jax.ShapeDtypeStruct((m, D_MODEL), _F32), jax.ShapeDtypeStruct((m, D_MODEL), _BF16)],
        compiler_params=_params(1),
        name="oproj_ln",
    )(x, attn, w_o, g, b)


def _run_group(x, pool_hist, conv_hist, caches, pos0, seqs, p):
    layer_inputs, new_conv = [], []
    k_f32 = v_f32 = k_b = v_b = xb = None
    for l in range(DEPTH):
        if l < N_A_LAYERS:
            layer_inputs.append(x)
            x, xb = _pool_ln(x, pool_hist[l], p['w_pool'], p['pool_scale'], p['ln1_g'], p['ln1_b'],
                             layer=l, seqs=seqs, pos0=pos0)
        else:
            j = l - N_A_LAYERS
            lam_init = 0.8 - 0.6 * math.exp(-0.3 * l)
            q = _q_proj(xb, p['w_q%d' % j], scale=Q_SCALE)
            if caches is None:
                attn = _flash_prompt(q, k_b, v_b, p['lam'], p['subln_g'], layer=j, lam_init=lam_init)
            else:
                attn = _sample_attn(q, k_b, v_b, caches[0], caches[1], p['lam'], p['subln_g'],
                                    layer=j, lam_init=lam_init, seqs=seqs)
            x, xb = _oproj_ln(x, attn, p['w_o%d' % j], p['ln1_g'], p['ln1_b'], ln_layer=l)
        bf16_copy = N_A_LAYERS - 1 <= l < DEPTH - 1
        cast_next = l + 1 < DEPTH and len(p['w_up']) == l + 1
        side = [(name, spec) for name, spec in p['to_cast'].get(l, []) if name not in p]
        outs = _ffn_ln(xb, x, conv_hist[l], p['w_up'][l], p['conv_w'], p['conv_b'], p['w_down'][l],
                       p['ln2_g'], p['ln2_b'], layer=l, seqs=seqs, bf16_copy=bf16_copy,
                       next_weights=(p['w_up_f32'], p['w_down_f32']) if cast_next else None,
                       side_casts=[spec for _, spec in side])
        for (name, (blocks, _, _)), cast in zip(side, outs[len(outs) - len(side):]):
            p[name] = cast.reshape(-1, blocks.shape[-1])
        outs = outs[:len(outs) - len(side)]
        if cast_next:
            *outs, w_up_next, w_down_next = outs
            p['w_up'].append(w_up_next)
            p['w_down'].append(w_down_next)
        x, *xb, new_a, new_v = outs
        xb = xb[0] if xb else None
        new_conv.append(jnp.concatenate([_untile_cols(new_a), _untile_cols(new_v)], axis=1))
        if l == N_A_LAYERS - 1:
            k_f32, k_b, v_f32, v_b = _kv_proj(xb, p['w_kv'], k_width=N_HEADS * HEAD_W)
    return x, layer_inputs, new_conv, k_f32, v_f32


def _untile_cols(a):
    tiles, rows, w = a.shape
    return jnp.transpose(a, (1, 0, 2)).reshape(rows, tiles * w)


def _to_time_major(a):
    seqs, steps, w = a.shape
    return jnp.transpose(a, (1, 0, 2)).reshape(steps * seqs, w)


def _to_batch_major(a, seqs):
    return jnp.swapaxes(a.reshape(a.shape[0] // seqs, seqs, *a.shape[1:]), 0, 1)


def kernel(x_prompt, x_sample, state_pool, state_ffn_conv, cache_k, cache_v, ln1_g, ln1_b, ln2_g, ln2_b,
           w_pool, pool_scale, w_up, conv_w, conv_b, w_down, w_kv, w_q, lam, subln_g, w_o):
    rows = lambda a: a[:, None, :]
    p = dict(ln1_g=rows(ln1_g), ln1_b=rows(ln1_b), ln2_g=rows(ln2_g), ln2_b=rows(ln2_b),
             pool_scale=rows(pool_scale), conv_w=conv_w, conv_b=rows(conv_b), lam=lam, subln_g=rows(subln_g),
             w_pool=w_pool.astype(_BF16), w_up_f32=w_up, w_down_f32=w_down,
             w_up=[w_up[0].astype(_BF16)], w_down=[w_down[0].astype(_BF16)])

    ffn_steps = (x_prompt.shape[1] // ROW_TILE) * N_FF_TILES
    block_elems = CAST_BLOCK_ELEMS
    while w_kv.size // block_elems > ffn_steps:
        block_elems *= 2

    def blocks(w):
        return w.reshape(-1, block_elems // w.shape[-1], w.shape[-1])

    per_layer = D_MODEL * N_HEADS * HEAD_W // block_elems
    kv_blocks, q_blocks, o_blocks = blocks(w_kv), blocks(w_q), blocks(w_o)
    p['to_cast'] = {N_A_LAYERS - 2: [('w_kv', (kv_blocks, 0, kv_blocks.shape[0]))]}
    for j in range(DEPTH - N_A_LAYERS):
        p['to_cast'][N_A_LAYERS - 1 + j] = [('w_q%d' % j, (q_blocks, j * per_layer, per_layer)),
                                            ('w_o%d' % j, (o_blocks, j * per_layer, per_layer))]

    def finish(x, layer_inputs, new_conv, k_new, v_new, seqs):
        steps = x.shape[0] // seqs
        y = _to_batch_major(x, seqs)
        pool = jnp.stack([_to_batch_major(u[(steps - POOL_HIST) * seqs:], seqs) for u in layer_inputs])
        conv = jnp.stack([_to_batch_major(c, seqs) for c in new_conv])
        return y, pool, conv, _to_batch_major(k_new, seqs), _to_batch_major(v_new, seqs)

    b, t, _ = x_prompt.shape
    assert b == 1 and t >= POOL_HIST
    zeros_pool = jnp.zeros(((POOL_HIST + 1) * b, D_MODEL), _F32)
    zeros_conv = jnp.zeros(((CONV_W - 1) * b, 2 * D_FF), _F32)
    out_p = _run_group(_to_time_major(x_prompt), [zeros_pool] * N_A_LAYERS, [zeros_conv] * DEPTH,
                       None, 0, b, p)
    y_p, pool_p, conv_p, k_p, v_p = finish(*out_p, b)

    sb, steps, _ = x_sample.shape
    past = cache_k.shape[1]
    assert steps >= POOL_HIST
    pad = jnp.zeros((sb, D_MODEL), _F32)
    pool_hist = [jnp.concatenate([pad, _to_time_major(state_pool[l])], axis=0) for l in range(N_A_LAYERS)]
    conv_hist = [_to_time_major(state_ffn_conv[l]) for l in range(DEPTH)]
    out_s = _run_group(_to_time_major(x_sample), pool_hist, conv_hist, (cache_k, cache_v), past, sb, p)
    y_s, pool_s, conv_s, k_s, v_s = finish(*out_s, sb)

    return (y_p, y_s, pool_p, pool_s, conv_p, conv_s, k_p, v_p, k_s, v_s)
```
